```python
import jax, jax.numpy as jnp
from jax import lax
import numpy as np

D_MODEL = 1024
BATCH = 32
SEQ = 2048
DEPTH = 1
DEC_BATCH = 16
DEC_SEQ = 32
PAST_LEN = 1024

CHUNK = 64
D_MIX = D_MODEL
GLA_WIDTH = D_MIX // 2
GLA_HEADS = 4
GLA_DV = GLA_WIDTH // GLA_HEADS
GLA_DK = GLA_DV // 2
GLA_KEY = GLA_HEADS * GLA_DK
GATE_RANK = 16
GATE_NORMALIZER = 16.0
POOL_WIDTH = D_MIX - GLA_WIDTH
POOL_WINDOWS = (2, 4, 8, 16)
POOL_GROUPS = len(POOL_WINDOWS)
POOL_GC = POOL_WIDTH // POOL_GROUPS
POOL_HIST = max(POOL_WINDOWS) - 1
D_IN = 2 * GLA_KEY + 2 * GLA_WIDTH + GATE_RANK + POOL_WIDTH
SPLIT_POINTS = (GLA_KEY, 2 * GLA_KEY, 2 * GLA_KEY + GLA_WIDTH, 2 * GLA_KEY + 2 * GLA_WIDTH,
                2 * GLA_KEY + 2 * GLA_WIDTH + GATE_RANK)
N_MEM = 256
MEM_HEADS = 4
MEM_DH = D_MODEL // MEM_HEADS
N_EXPERTS = 256
N_GROUPS = 8
EXPERTS_PER_GROUP = N_EXPERTS // N_GROUPS
TOPK_GROUPS = 4
TOP_K = 8
ROUTED_SCALE = 2.5
D_EXPERT = D_MODEL // 4
D_SHARED = D_EXPERT
EXPERT_BLOCK = 128
ALPHA = (2 * DEPTH) ** 0.25
BETA = (8 * DEPTH) ** -0.25
EPS = 1e-5
F32 = jnp.float32

kernel_name = 'hymba_gla_pool_moe_stream_step'


def _layernorm(x, g, b):
    xf = x.astype(F32)
    mu = jnp.mean(xf, -1, keepdims=True)
    var = jnp.mean(jnp.square(xf - mu), -1, keepdims=True)
    return ((xf - mu) * lax.rsqrt(var + EPS) * g + b).astype(x.dtype)


def _gla(q, k, v, logf, s0):
    B, L = q.shape[0], q.shape[1]
    nb = -(-L // CHUNK)
    pad = nb * CHUNK - L

    def blocks(t):
        t = jnp.pad(t.astype(F32), ((0, 0), (0, pad), (0, 0), (0, 0)))
        return jnp.moveaxis(t.reshape(B, nb, CHUNK, t.shape[2], t.shape[3]), 1, 0)

    causal = jnp.tril(jnp.ones((CHUNK, CHUNK), bool))

    def step(S, blk):
        qb, kb, vb, fb = blk
        cum = jnp.cumsum(fb, axis=1)
        diff = cum[:, :, None] - cum[:, None, :]
        decay = jnp.exp(jnp.where(causal[None, :, :, None, None], diff, -jnp.inf))
        att = jnp.einsum('bihd,bjhd,bijhd->bhij', qb, kb, decay)
        o = (jnp.einsum('bhij,bjhv->bihv', att, vb)
             + jnp.einsum('bihd,bhdv->bihv', qb * jnp.exp(cum), S))
        last = cum[:, -1]
        S = (jnp.exp(last)[..., None] * S
             + jnp.einsum('bjhd,bjhv->bhdv', kb * jnp.exp(last[:, None] - cum), vb))
        return S, o

    S, o = lax.scan(step, s0.astype(F32), (blocks(q), blocks(k), blocks(v), blocks(logf)))
    o = jnp.moveaxis(o, 0, 1).reshape(B, nb * CHUNK, GLA_HEADS, GLA_DV)[:, :L]
    return o, S


def _pool(u, hist, start_pos, pool_w, pool_scale):
    B, L, _ = u.shape
    full = jnp.concatenate([hist.astype(F32), u.astype(F32)], axis=1)
    cs = jnp.concatenate([jnp.zeros((B, 1, POOL_WIDTH), F32), jnp.cumsum(full, axis=1)], axis=1)
    end = cs[:, POOL_HIST + 1:]
    n_valid = start_pos + jnp.arange(L) + 1
    means = []
    for gi, w in enumerate(POOL_WINDOWS):
        sl = slice(gi * POOL_GC, (gi + 1) * POOL_GC)
        win = end[..., sl] - cs[:, POOL_HIST + 1 - w:POOL_HIST + 1 - w + L, sl]
        cnt = jnp.minimum(w, n_valid).astype(F32)
        means.append(win / cnt[None, :, None])
    r = (jnp.concatenate(means, -1) - full[:, POOL_HIST:]).reshape(B, L, POOL_GROUPS, POOL_GC)
    out = jnp.einsum('blgc,gcd->blgd', r, pool_w.astype(F32)).reshape(B, L, POOL_WIDTH) * pool_scale
    return out.astype(u.dtype), full[:, L:].astype(u.dtype)


def _mem_attn(x, mem_k, mem_v, wq_mem, wo_mem):
    B, L, _ = x.shape
    q = jnp.einsum('bld,de->ble', x, wq_mem).reshape(B, L, MEM_HEADS, MEM_DH)
    s = jnp.einsum('blhd,bmhd->bhlm', q.astype(F32), mem_k.astype(F32)) * (MEM_DH ** -0.5)
    p = jax.nn.softmax(s, axis=-1)
    o = jnp.einsum('bhlm,bmhd->blhd', p, mem_v.astype(F32)).reshape(B, L, D_MODEL).astype(x.dtype)
    return jnp.einsum('ble,ed->bld', o, wo_mem)


def _moe(x, w_router, router_bias, w1_exp, w3_exp, w2_exp, w1_sh, w3_sh, w2_sh):
    B, L, D = x.shape
    n_tok = B * L
    xt = x.reshape(n_tok, D)
    logits = jnp.einsum('nd,de->ne', xt.astype(F32), w_router.astype(F32))
    scores = jax.nn.sigmoid(logits)
    biased = scores + router_bias.astype(F32)
    grp_score = lax.top_k(biased.reshape(n_tok, N_GROUPS, EXPERTS_PER_GROUP), 2)[0].sum(-1)
    _, top_grp = lax.top_k(grp_score, TOPK_GROUPS)
    grp_mask = (top_grp[..., None] == jnp.arange(N_GROUPS)).any(axis=1)
    exp_mask = jnp.repeat(grp_mask, EXPERTS_PER_GROUP, axis=1)
    _, top_e = lax.top_k(jnp.where(exp_mask, biased, -jnp.inf), TOP_K)
    gate = jnp.take_along_axis(scores, top_e, axis=1)
    gate = gate / jnp.sum(gate, -1, keepdims=True) * ROUTED_SCALE
    n_asg = n_tok * TOP_K
    flat_e = top_e.reshape(-1)
    flat_tok = jnp.repeat(jnp.arange(n_tok, dtype=jnp.int32), TOP_K)
    flat_g = gate.reshape(-1)
    order = jnp.argsort(flat_e)
    e_sorted = flat_e[order]
    counts = jnp.bincount(flat_e, length=N_EXPERTS)
    start = jnp.cumsum(counts) - counts
    padded = (counts + EXPERT_BLOCK - 1) // EXPERT_BLOCK * EXPERT_BLOCK
    pad_end = jnp.cumsum(padded)
    pad_start = pad_end - padded
    dest = pad_start[e_sorted] + (jnp.arange(n_asg) - start[e_sorted])
    n_blk = -(-n_asg // EXPERT_BLOCK) + N_EXPERTS
    n_slot = n_blk * EXPERT_BLOCK
    slot_tok = jnp.full((n_slot,), n_tok, jnp.int32).at[dest].set(flat_tok[order])
    slot_gate = jnp.zeros((n_slot,), F32).at[dest].set(flat_g[order])
    blk_expert = jnp.minimum(
        jnp.searchsorted(pad_end, jnp.arange(n_blk) * EXPERT_BLOCK, side='right'), N_EXPERTS - 1)
    x_pad = jnp.concatenate([xt, jnp.zeros((1, D), xt.dtype)], axis=0)

    def blk(acc, inp):
        tok, g, e = inp
        xb = x_pad[tok]
        h = jax.nn.silu(xb @ w1_exp[e]) * (xb @ w3_exp[e])
        yb = (h @ w2_exp[e]).astype(F32) * g[:, None]
        return acc.at[tok].add(yb), None

    acc, _ = lax.scan(blk, jnp.zeros((n_tok + 1, D), F32),
                      (slot_tok.reshape(n_blk, EXPERT_BLOCK), slot_gate.reshape(n_blk, EXPERT_BLOCK), blk_expert))
    shared = (jax.nn.silu(xt @ w1_sh) * (xt @ w3_sh)) @ w2_sh
    return (acc[:n_tok] + shared.astype(F32)).astype(x.dtype).reshape(B, L, D)


def _layer(x, gla_s0, pool_hist, start_pos, mem_k, mem_v,
           w_in, w_gate_up, b_gate, gla_norm_g, pool_w, pool_scale, w_out,
           ln1_g, ln1_b, wq_mem, wo_mem, ln2_g, ln2_b,
           w_router, router_bias, w1_exp, w3_exp, w2_exp, w1_sh, w3_sh, w2_sh, ln3_g, ln3_b):
    B, L, _ = x.shape
    proj = jnp.einsum('bld,de->ble', x, w_in)
    q, k, v, g, gdown, u = jnp.split(proj, SPLIT_POINTS, axis=-1)
    q = q.reshape(B, L, GLA_HEADS, GLA_DK) * (GLA_DK ** -0.5)
    k = k.reshape(B, L, GLA_HEADS, GLA_DK)
    v = v.reshape(B, L, GLA_HEADS, GLA_DV)
    logf = jax.nn.log_sigmoid(jnp.einsum('blr,re->ble', gdown.astype(F32), w_gate_up.astype(F32))
                              + b_gate) / GATE_NORMALIZER
    o, s_new = _gla(q, k, v, logf.reshape(B, L, GLA_HEADS, GLA_DK), gla_s0)
    o = o * lax.rsqrt(jnp.mean(jnp.square(o), -1, keepdims=True) + EPS) * gla_norm_g
    o = (o.reshape(B, L, GLA_WIDTH) * jax.nn.silu(g.astype(F32))).astype(x.dtype)
    p, hist_new = _pool(u, pool_hist, start_pos, pool_w, pool_scale)
    mix = jnp.einsum('ble,ed->bld', jnp.concatenate([o, p], axis=-1), w_out)
    x = _layernorm(x * ALPHA + mix, ln1_g, ln1_b)
    x = _layernorm(x * ALPHA + _mem_attn(x, mem_k, mem_v, wq_mem, wo_mem), ln2_g, ln2_b)
    x = _layernorm(x * ALPHA + _moe(x, w_router, router_bias, w1_exp, w3_exp, w2_exp, w1_sh, w3_sh, w2_sh),
                   ln3_g, ln3_b)
    return x, s_new, hist_new


def setup_inputs(seed: int = 0) -> dict:
    key = jax.random.key(seed)
    ks = list(jax.random.split(key, 40))

    def nrm(i, shape, scale):
        return scale * jax.random.normal(ks[i], shape, F32)

    D = D_MODEL
    return {
        'x_prompt': nrm(0, (BATCH, SEQ, D), 1.0),
        'x_sample': nrm(1, (DEC_BATCH, DEC_SEQ, D), 1.0),
        'state_gla': nrm(2, (DEPTH, DEC_BATCH, GLA_HEADS, GLA_DK, GLA_DV), 0.3),
        'cache_pool': nrm(3, (DEPTH, DEC_BATCH, POOL_HIST, POOL_WIDTH), 1.0),
        'cache_mem_k': nrm(4, (DEPTH, DEC_BATCH, N_MEM, MEM_HEADS, MEM_DH), 1.0),
        'cache_mem_v': nrm(5, (DEPTH, DEC_BATCH, N_MEM, MEM_HEADS, MEM_DH), 1.0),
        'mem_prompt': nrm(6, (BATCH, N_MEM, D), 1.0),
        'ln_in_g': 1.0 + nrm(7, (D,), 0.02),
        'ln_in_b': nrm(8, (D,), 0.02),
        'w_in': nrm(9, (DEPTH, D, D_IN), D ** -0.5),
        'w_gate_up': nrm(10, (DEPTH, GATE_RANK, GLA_KEY), GATE_RANK ** -0.5),
        'b_gate': nrm(11, (DEPTH, GLA_KEY), 0.1),
        'gla_norm_g': 1.0 + nrm(12, (DEPTH, GLA_DV), 0.02),
        'pool_w': nrm(13, (DEPTH, POOL_GROUPS, POOL_GC, POOL_GC), POOL_GC ** -0.5),
        'pool_scale': 1.0 + nrm(14, (DEPTH, POOL_WIDTH), 0.02),
        'w_out': nrm(15, (DEPTH, D_MIX, D), BETA * D_MIX ** -0.5),
        'ln1_g': 1.0 + nrm(16, (DEPTH, D), 0.02),
        'ln1_b': nrm(17, (DEPTH, D), 0.02),
        'wq_mem': nrm(18, (DEPTH, D, D), D ** -0.5),
        'wk_mem': nrm(19, (DEPTH, D, D), D ** -0.5),
        'wv_mem': nrm(20, (DEPTH, D, D), D ** -0.5),
        'wo_mem': nrm(21, (DEPTH, D, D), BETA * D ** -0.5),
        'ln2_g': 1.0 + nrm(22, (DEPTH, D), 0.02),
        'ln2_b': nrm(23, (DEPTH, D), 0.02),
        'w_router': nrm(24, (DEPTH, D, N_EXPERTS), D ** -0.5),
        'router_bias': nrm(25, (DEPTH, N_EXPERTS), 0.01),
        'w1_exp': nrm(26, (DEPTH, N_EXPERTS, D, D_EXPERT), D ** -0.5),
        'w3_exp': nrm(27, (DEPTH, N_EXPERTS, D, D_EXPERT), D ** -0.5),
        'w2_exp': nrm(28, (DEPTH, N_EXPERTS, D_EXPERT, D), BETA * D_EXPERT ** -0.5),
        'w1_sh': nrm(29, (DEPTH, D, D_SHARED), D ** -0.5),
        'w3_sh': nrm(30, (DEPTH, D, D_SHARED), D ** -0.5),
        'w2_sh': nrm(31, (DEPTH, D_SHARED, D), BETA * D_SHARED ** -0.5),
        'ln3_g': 1.0 + nrm(32, (DEPTH, D), 0.02),
        'ln3_b': nrm(33, (DEPTH, D), 0.02),
    }


def reference(x_prompt, x_sample, state_gla, cache_pool, cache_mem_k, cache_mem_v, mem_prompt,
              ln_in_g, ln_in_b, w_in, w_gate_up, b_gate, gla_norm_g, pool_w, pool_scale, w_out,
              ln1_g, ln1_b, wq_mem, wk_mem, wv_mem, wo_mem, ln2_g, ln2_b,
              w_router, router_bias, w1_exp, w3_exp, w2_exp, w1_sh, w3_sh, w2_sh, ln3_g, ln3_b):
    bp = x_prompt.shape[0]
    xp = _layernorm(x_prompt, ln_in_g, ln_in_b)
    xs = _layernorm(x_sample, ln_in_g, ln_in_b)
    sp_l, hp_l, mk_l, mv_l, ss_l, hs_l = [], [], [], [], [], []
    for l in range(DEPTH):
        lp = (w_in[l], w_gate_up[l], b_gate[l], gla_norm_g[l], pool_w[l], pool_scale[l], w_out[l],
              ln1_g[l], ln1_b[l], wq_mem[l], wo_mem[l], ln2_g[l], ln2_b[l],
              w_router[l], router_bias[l], w1_exp[l], w3_exp[l], w2_exp[l],
              w1_sh[l], w3_sh[l], w2_sh[l], ln3_g[l], ln3_b[l])
        mem_k = jnp.einsum('bmd,de->bme', mem_prompt, wk_mem[l]).reshape(bp, N_MEM, MEM_HEADS, MEM_DH)
        mem_v = jnp.einsum('bmd,de->bme', mem_prompt, wv_mem[l]).reshape(bp, N_MEM, MEM_HEADS, MEM_DH)
        s0 = jnp.zeros((bp, GLA_HEADS, GLA_DK, GLA_DV), F32)
        h0 = jnp.zeros((bp, POOL_HIST, POOL_WIDTH), x_prompt.dtype)
        xp, sp, hp = _layer(xp, s0, h0, 0, mem_k, mem_v, *lp)
        xs, ss, hs = _layer(xs, state_gla[l], cache_pool[l], PAST_LEN, cache_mem_k[l], cache_mem_v[l], *lp)
        sp_l.append(sp)
        hp_l.append(hp)
        mk_l.append(mem_k)
        mv_l.append(mem_v)
        ss_l.append(ss)
        hs_l.append(hs)
    return (xp, xs, jnp.stack(sp_l), jnp.stack(hp_l), jnp.stack(mk_l), jnp.stack(mv_l),
            jnp.stack(ss_l), jnp.stack(hs_l))
```

```python
import functools

import jax
import jax.numpy as jnp
from jax import lax
from jax.experimental import pallas as pl
from jax.experimental.pallas import tpu as pltpu

F32 = jnp.float32
BF16 = jnp.bfloat16
I32 = jnp.int32

D_MODEL = 1024
CHUNK = 64
SUB = 16
GLA_HEADS = 4
GLA_DK = 64
GLA_DV = 128
GLA_KEY = GLA_HEADS * GLA_DK
GLA_WIDTH = GLA_HEADS * GLA_DV
GATE_RANK = 16
GATE_PAD = 128
POOL_WIDTH = 512
POOL_WINDOWS = (2, 4, 8, 16)
POOL_GC = 128
POOL_HIST = 15
HIST_ROWS = 16
N_MEM = 256
MEM_HEADS = 4
MEM_DH = 256
N_EXPERTS = 256
N_GROUPS = 8
EXPERTS_PER_GROUP = 32
TOPK_GROUPS = 4
TOP_K = 8
ROUTED_SCALE = 2.5
D_EXPERT = 256
TOKEN_TILE = 256
SUBLANES = 8
ALPHA = 2.0 ** 0.25
EPS = 1e-5
OFF_Q, OFF_K, OFF_V, OFF_G, OFF_U, OFF_GD = 0, 256, 512, 1024, 1536, 2048
D_IN_PAD = OFF_GD + GATE_PAD

VMEM_LIMIT = 56 * 1024 * 1024


def _ln(x, g, b):
    mu = jnp.mean(x, axis=-1, keepdims=True)
    xc = x - mu
    var = jnp.mean(xc * xc, axis=-1, keepdims=True)
    return xc * lax.rsqrt(var + EPS) * g + b


def _dot(a, b):
    return jnp.dot(a, b, preferred_element_type=F32)


def _dot_nt(a, b):
    return lax.dot_general(a, b, (((1,), (1,)), ((), ())), preferred_element_type=F32)


def _dot_tn(a, b):
    return lax.dot_general(a, b, (((0,), (0,)), ((), ())), preferred_element_type=F32)


def _split3(x):
    h = x.astype(BF16)
    r = x - h.astype(F32)
    m = r.astype(BF16)
    l = (r - m.astype(F32)).astype(BF16)
    return h, m, l


def _trunk_a_kernel(x_ref, s0_ref, h0_ref, lng_ref, lnb_ref, win_ref, wgu_ref, bg_ref, gng_ref,
                    pw_ref, ps_ref, wout_ref, l1g_ref, l1b_ref,
                    x1_ref, sn_ref, hn_ref,
                    proj_scr, ext_scr, s_scr, op_scr, *, tl, chunk, start_pos):
    t = pl.program_id(1)
    nt = pl.num_programs(1)

    @pl.when(t == 0)
    def _():
        s_scr[...] = s0_ref[0]
        ext_scr[0:1, :] = jnp.zeros((1, POOL_WIDTH), F32)
        ext_scr[1:HIST_ROWS, :] = h0_ref[0]

    xn = _ln(x_ref[0], lng_ref[...], lnb_ref[...])
    proj_scr[...] = _dot(xn.astype(BF16), win_ref[...])

    c = chunk
    rows = lax.broadcasted_iota(I32, (c, c), 0)
    cols = lax.broadcasted_iota(I32, (c, c), 1)
    tri = (cols <= rows).astype(BF16)
    causal = cols <= rows
    eye_dk = lax.broadcasted_iota(I32, (GLA_DK, GLA_DK), 0) == lax.broadcasted_iota(I32, (GLA_DK, GLA_DK), 1)
    n_sub = c // SUB
    rblk = lax.broadcasted_iota(I32, (c, n_sub * GLA_DK), 0) // SUB
    lblk = lax.broadcasted_iota(I32, (c, n_sub * GLA_DK), 1) // GLA_DK
    lblk_row = lax.broadcasted_iota(I32, (1, n_sub * GLA_DK), 1) // GLA_DK
    mask_q = rblk == lblk
    mask_k = rblk <= lblk

    def tile_lanes(a):
        return jnp.concatenate([a] * n_sub, axis=1)

    def chunk_body(ci, carry):
        r0 = pl.multiple_of(ci * c, c)
        rs = pl.ds(r0, c)
        gd = proj_scr[rs, OFF_GD:OFF_GD + GATE_PAD]
        z = _dot(gd.astype(BF16), wgu_ref[...]) + bg_ref[...]
        lf = (jnp.minimum(z, 0.0) - jnp.log1p(jnp.exp(-jnp.abs(z)))) * (1.0 / 16.0)
        lh, lm, ll = _split3(lf)
        cum = _dot(tri, lh) + _dot(tri, lm) + _dot(tri, ll)
        last = cum[c - 1:c, :]
        q_all = proj_scr[rs, OFF_Q:OFF_Q + GLA_KEY] * (GLA_DK ** -0.5)
        k_all = proj_scr[rs, OFF_K:OFF_K + GLA_KEY]
        qs_all = q_all * jnp.exp(cum)
        ks_all = k_all * jnp.exp(last - cum)
        for h in range(GLA_HEADS):
            ksl = slice(h * GLA_DK, (h + 1) * GLA_DK)
            vsl = slice(h * GLA_DV, (h + 1) * GLA_DV)
            v_h = proj_scr[rs, OFF_V + h * GLA_DV:OFF_V + (h + 1) * GLA_DV].astype(BF16)
            cum_t = tile_lanes(cum[:, ksl])
            q_t = tile_lanes(q_all[:, ksl])
            k_t = tile_lanes(k_all[:, ksl])
            ref_row = jnp.zeros((1, n_sub * GLA_DK), F32)
            for i in range(1, n_sub):
                ref_row = jnp.where(lblk_row == i, cum_t[i * SUB - 1:i * SUB, :], ref_row)
            arg = cum_t - ref_row
            lhs = jnp.where(mask_q, q_t * jnp.exp(jnp.where(mask_q, arg, 0.0)), 0.0)
            rhs = jnp.where(mask_k, k_t * jnp.exp(jnp.where(mask_k, -arg, 0.0)), 0.0)
            att = jnp.where(causal, _dot_nt(lhs.astype(BF16), rhs.astype(BF16)), 0.0)
            s_h = s_scr[h]
            o_h = _dot(att.astype(BF16), v_h) + _dot(qs_all[:, ksl].astype(BF16), s_h.astype(BF16))
            dcol = jnp.sum(jnp.where(eye_dk, jnp.broadcast_to(jnp.exp(last[:, ksl]), (GLA_DK, GLA_DK)), 0.0),
                           axis=1, keepdims=True)
            s_scr[h] = dcol * s_h + _dot_tn(ks_all[:, ksl].astype(BF16), v_h)
            o_h = o_h * lax.rsqrt(jnp.mean(o_h * o_h, axis=-1, keepdims=True) + EPS) * gng_ref[...]
            g_h = proj_scr[rs, OFF_G + h * GLA_DV:OFF_G + (h + 1) * GLA_DV]
            op_scr[rs, vsl] = (o_h * (g_h * jax.nn.sigmoid(g_h))).astype(BF16)
        return carry

    lax.fori_loop(0, tl // c, chunk_body, 0)

    u = proj_scr[:, OFF_U:OFF_U + POOL_WIDTH]
    ext_scr[HIST_ROWS:HIST_ROWS + tl, :] = u
    n_valid = start_pos + t * tl + lax.broadcasted_iota(I32, (tl, 1), 0) + 1
    for gi, w in enumerate(POOL_WINDOWS):
        lsl = slice(gi * POOL_GC, (gi + 1) * POOL_GC)
        win = ext_scr[HIST_ROWS:HIST_ROWS + tl, lsl]
        for s in range(1, w):
            win = win + ext_scr[HIST_ROWS - s:HIST_ROWS - s + tl, lsl]
        cnt = jnp.minimum(w, n_valid).astype(F32)
        r = win / cnt - u[:, lsl]
        p = _dot(r.astype(BF16), pw_ref[gi]) * ps_ref[:, lsl]
        op_scr[:, GLA_WIDTH + gi * POOL_GC:GLA_WIDTH + (gi + 1) * POOL_GC] = p.astype(BF16)
    tail = ext_scr[tl:tl + HIST_ROWS, :]
    ext_scr[0:HIST_ROWS, :] = tail

    mix = _dot(op_scr[...], wout_ref[...])
    x1_ref[0] = _ln(xn * ALPHA + mix, l1g_ref[...], l1b_ref[...])

    @pl.when(t == nt - 1)
    def _():
        sn_ref[0] = s_scr[...]
        hn_ref[0] = ext_scr[1:HIST_ROWS, :]


def _trunk_a(x, s0, h0, wts, *, start_pos):
    b, l, d = x.shape
    tl = min(l, 256)
    chunk = min(tl, CHUNK)
    assert l % tl == 0 and tl % chunk == 0 and chunk % SUB == 0 and l >= HIST_ROWS
    nt = l // tl
    kern = functools.partial(_trunk_a_kernel, tl=tl, chunk=chunk, start_pos=start_pos)
    const = lambda shape: pl.BlockSpec(shape, lambda bi, ti: (0,) * len(shape))
    return pl.pallas_call(
        kern,
        grid=(b, nt),
        in_specs=[
            pl.BlockSpec((1, tl, d), lambda bi, ti: (bi, ti, 0)),
            pl.BlockSpec((1, GLA_HEADS, GLA_DK, GLA_DV), lambda bi, ti: (bi, 0, 0, 0)),
            pl.BlockSpec((1, POOL_HIST, POOL_WIDTH), lambda bi, ti: (bi, 0, 0)),
            const((1, d)), const((1, d)),
            const((d, D_IN_PAD)), const((GATE_PAD, GLA_KEY)), const((1, GLA_KEY)), const((1, GLA_DV)),
            const((len(POOL_WINDOWS), POOL_GC, POOL_GC)), const((1, POOL_WIDTH)),
            const((GLA_WIDTH + POOL_WIDTH, d)), const((1, d)), const((1, d)),
        ],
        out_specs=[
            pl.BlockSpec((1, tl, d), lambda bi, ti: (bi, ti, 0)),
            pl.BlockSpec((1, GLA_HEADS, GLA_DK, GLA_DV), lambda bi, ti: (bi, 0, 0, 0)),
            pl.BlockSpec((1, POOL_HIST, POOL_WIDTH), lambda bi, ti: (bi, 0, 0)),
        ],
        out_shape=[
            jax.ShapeDtypeStruct((b, l, d), F32),
            jax.ShapeDtypeStruct((b, GLA_HEADS, GLA_DK, GLA_DV), F32),
            jax.ShapeDtypeStruct((b, POOL_HIST, POOL_WIDTH), F32),
        ],
        scratch_shapes=[
            pltpu.VMEM((tl, D_IN_PAD), F32),
            pltpu.VMEM((HIST_ROWS + tl, POOL_WIDTH), F32),
            pltpu.VMEM((GLA_HEADS, GLA_DK, GLA_DV), F32),
            pltpu.VMEM((tl, GLA_WIDTH + POOL_WIDTH), BF16),
        ],
        compiler_params=pltpu.CompilerParams(
            dimension_semantics=("arbitrary", "arbitrary"), vmem_limit_bytes=VMEM_LIMIT),
        name="trunk_a",
    )(x, s0, h0, *wts)


def _prep_trunk_a_weights(ln_in_g, ln_in_b, w_in, w_gate_up, b_gate, gla_norm_g, pool_w, pool_scale, w_out,
                          ln1_g, ln1_b):
    d = D_MODEL
    p_gd = 2 * GLA_KEY + 2 * GLA_WIDTH
    w_in_r = jnp.concatenate(
        [w_in[:, :p_gd], w_in[:, p_gd + GATE_RANK:], w_in[:, p_gd:p_gd + GATE_RANK],
         jnp.zeros((d, GATE_PAD - GATE_RANK), w_in.dtype)], axis=1).astype(BF16)
    wgu = jnp.concatenate([w_gate_up, jnp.zeros((GATE_PAD - GATE_RANK, GLA_KEY), w_gate_up.dtype)],
                          axis=0).astype(BF16)
    return (ln_in_g.reshape(1, d), ln_in_b.reshape(1, d), w_in_r, wgu, b_gate.reshape(1, GLA_KEY),
            gla_norm_g.reshape(1, GLA_DV), pool_w.astype(BF16), pool_scale.reshape(1, POOL_WIDTH),
            w_out.astype(BF16), ln1_g.reshape(1, d), ln1_b.reshape(1, d))


def _mem_kv_kernel(m_ref, wk_ref, wv_ref, k_ref, v_ref, kb_ref, vb_ref):
    m = m_ref[...].astype(BF16)
    k = _dot(m, wk_ref[...])
    v = _dot(m, wv_ref[...])
    k_ref[...] = k
    v_ref[...] = v
    kb_ref[...] = k.astype(BF16)
    vb_ref[...] = v.astype(BF16)


def _mem_kv(mem, wk, wv):
    m, d = mem.shape
    tm = min(m, 512)
    assert m % tm == 0
    row = pl.BlockSpec((tm, d), lambda i: (i, 0))
    wspec = pl.BlockSpec((d, d), lambda i: (0, 0))
    return pl.pallas_call(
        _mem_kv_kernel,
        grid=(m // tm,),
        in_specs=[row, wspec, wspec],
        out_specs=[row, row, row, row],
        out_shape=[jax.ShapeDtypeStruct((m, d), F32), jax.ShapeDtypeStruct((m, d), F32),
                   jax.ShapeDtypeStruct((m, d), BF16), jax.ShapeDtypeStruct((m, d), BF16)],
        compiler_params=pltpu.CompilerParams(dimension_semantics=("arbitrary",), vmem_limit_bytes=VMEM_LIMIT),
        name="mem_kv",
    )(mem, wk, wv)


def _attn_kernel(x1_ref, k_ref, v_ref, wq_ref, wo_ref, g_ref, b_ref, x2_ref, o_scr):
    x1 = x1_ref[0]
    q = (_dot(x1.astype(BF16), wq_ref[...]) * (MEM_DH ** -0.5)).astype(BF16)
    for h in range(MEM_HEADS):
        hs = slice(h * MEM_DH, (h + 1) * MEM_DH)
        s = _dot_nt(q[:, hs], k_ref[0, :, hs])
        e = jnp.exp(s - jnp.max(s, axis=-1, keepdims=True))
        p = e / jnp.sum(e, axis=-1, keepdims=True)
        o_scr[:, hs] = _dot(p.astype(BF16), v_ref[0, :, hs]).astype(BF16)
    attn = _dot(o_scr[...], wo_ref[...])
    x2_ref[...] = _ln(x1 * ALPHA + attn, g_ref[...], b_ref[...])


def _attn(x1, mem_k, mem_v, wq, wo, g, b):
    bsz, l, d = x1.shape
    tl = min(l, 256)
    assert l % tl == 0
    nt = l // tl
    const = lambda shape: pl.BlockSpec(shape, lambda bi, ti: (0,) * len(shape))
    return pl.pallas_call(
        _attn_kernel,
        grid=(bsz, nt),
        in_specs=[
            pl.BlockSpec((1, tl, d), lambda bi, ti: (bi, ti, 0)),
            pl.BlockSpec((1, N_MEM, d), lambda bi, ti: (bi, 0, 0)),
            pl.BlockSpec((1, N_MEM, d), lambda bi, ti: (bi, 0, 0)),
            const((d, d)), const((d, d)), const((1, d)), const((1, d)),
        ],
        out_specs=pl.BlockSpec((tl, d), lambda bi, ti: (bi * nt + ti, 0)),
        out_shape=jax.ShapeDtypeStruct((bsz * l, d), F32),
        scratch_shapes=[pltpu.VMEM((tl, d), BF16)],
        compiler_params=pltpu.CompilerParams(
            dimension_semantics=("arbitrary", "arbitrary"), vmem_limit_bytes=VMEM_LIMIT),
        name="attn",
    )(x1, mem_k, mem_v, wq, wo, g, b)


def _two_part_specs(t, d, tiles_a):
    spec_a = pl.BlockSpec((t, d), lambda i, *_: (jnp.minimum(i, tiles_a - 1), 0))
    spec_b = pl.BlockSpec((t, d), lambda i, *_: (jnp.maximum(i - tiles_a, 0), 0))
    return spec_a, spec_b


def _router_kernel(xa_ref, xb_ref, wrt_ref, bias_ref, e_ref, g_ref, r_ref, cnt_ref, cnt_scr, *, tl, tiles_a):
    i = pl.program_id(0)

    @pl.when(i == 0)
    def _():
        cnt_scr[...] = jnp.zeros_like(cnt_scr)

    x = jnp.where(i < tiles_a, xa_ref[...], xb_ref[...])
    xh = x.astype(BF16)
    xm = (x - xh.astype(F32)).astype(BF16)
    wh = wrt_ref[0]
    wm = wrt_ref[1]
    logits = _dot_nt(wh, xh) + (_dot_nt(wh, xm) + _dot_nt(wm, xh))
    scores = jax.nn.sigmoid(logits)
    biased = scores + bias_ref[...]
    ninf = -jnp.inf
    eg = EXPERTS_PER_GROUP
    riota = lax.broadcasted_iota(I32, (eg, tl), 0)
    gs_rows = []
    for g in range(N_GROUPS):
        blk = biased[g * eg:(g + 1) * eg, :]
        m1 = jnp.max(blk, axis=0, keepdims=True)
        i1 = jnp.min(jnp.where(blk == m1, riota, eg), axis=0, keepdims=True)
        m2 = jnp.max(jnp.where(riota == i1, ninf, blk), axis=0, keepdims=True)
        gs_rows.append(m1 + m2)
    gs = jnp.concatenate(gs_rows, axis=0)
    giota = lax.broadcasted_iota(I32, (N_GROUPS, tl), 0)
    sel = jnp.zeros((N_GROUPS, tl), jnp.bool_)
    for _ in range(TOPK_GROUPS):
        m = jnp.max(gs, axis=0, keepdims=True)
        gi = jnp.min(jnp.where(gs == m, giota, N_GROUPS), axis=0, keepdims=True)
        hit = giota == gi
        sel = jnp.logical_or(sel, hit)
        gs = jnp.where(hit, ninf, gs)
    self = jnp.where(sel, 1.0, 0.0)
    masked = jnp.concatenate(
        [jnp.where(self[g:g + 1, :] > 0.5, biased[g * eg:(g + 1) * eg, :], ninf) for g in range(N_GROUPS)], axis=0)
    eiota = lax.broadcasted_iota(I32, (N_EXPERTS, tl), 0)
    idx_rows, sc_rows = [], []
    multi = jnp.zeros((N_EXPERTS, tl), F32)
    for _ in range(TOP_K):
        m = jnp.max(masked, axis=0, keepdims=True)
        idx = jnp.min(jnp.where(masked == m, eiota, N_EXPERTS), axis=0, keepdims=True)
        hit = eiota == idx
        sc_rows.append(jnp.sum(jnp.where(hit, scores, 0.0), axis=0, keepdims=True))
        idx_rows.append(idx)
        multi = jnp.where(hit, 1.0, multi)
        masked = jnp.where(hit, ninf, masked)
    top_e = jnp.concatenate(idx_rows, axis=0)
    sc = jnp.concatenate(sc_rows, axis=0)
    e_ref[...] = top_e
    g_ref[...] = sc / jnp.sum(sc, axis=0, keepdims=True) * ROUTED_SCALE
    mh = multi.astype(BF16)
    before = (lax.broadcasted_iota(I32, (tl, tl), 0) < lax.broadcasted_iota(I32, (tl, tl), 1)).astype(BF16)
    running = cnt_scr[...]
    rankmat = _dot(mh, before) + jnp.concatenate([running] * (tl // 128), axis=1)
    r_rows = [jnp.sum(jnp.where(eiota == idx_rows[k], rankmat, 0.0), axis=0, keepdims=True) for k in range(TOP_K)]
    r_ref[...] = jnp.concatenate(r_rows, axis=0).astype(I32)
    total = running + _dot(mh, jnp.ones((tl, 128), BF16))
    cnt_scr[...] = total
    cnt_ref[...] = total


def _router(x2a, x2b, wrt, bias_col):
    d = x2a.shape[1]
    tl = TOKEN_TILE
    assert x2a.shape[0] % tl == 0 and x2b.shape[0] % tl == 0
    tiles_a = x2a.shape[0] // tl
    n = x2a.shape[0] + x2b.shape[0]
    kspec = pl.BlockSpec((TOP_K, tl), lambda i: (0, i))
    return pl.pallas_call(
        functools.partial(_router_kernel, tl=tl, tiles_a=tiles_a),
        grid=(n // tl,),
        in_specs=[*_two_part_specs(tl, d, tiles_a),
                  pl.BlockSpec((2, N_EXPERTS, d), lambda i: (0, 0, 0)),
                  pl.BlockSpec((N_EXPERTS, 1), lambda i: (0, 0))],
        out_specs=[kspec, kspec, kspec, pl.BlockSpec((N_EXPERTS, 128), lambda i: (0, 0))],
        out_shape=[jax.ShapeDtypeStruct((TOP_K, n), I32), jax.ShapeDtypeStruct((TOP_K, n), F32),
                   jax.ShapeDtypeStruct((TOP_K, n), I32), jax.ShapeDtypeStruct((N_EXPERTS, 128), F32)],
        scratch_shapes=[pltpu.VMEM((N_EXPERTS, 128), F32)],
        compiler_params=pltpu.CompilerParams(dimension_semantics=("arbitrary",), vmem_limit_bytes=VMEM_LIMIT),
        name="router",
    )(x2a, x2b, wrt, bias_col)


def _row_copy(src_ref, src_row, dst_ref, dst_row, sem):
    return pltpu.make_async_copy(src_ref.at[pl.ds(src_row, 1)], dst_ref.at[pl.ds(dst_row, 1)], sem)


def _dispatch_kernel(fs_ref, fl_ref, nu_ref, dest_ref, xa_ref, xb_ref, xs_ref, zbuf, sem, zsem,
                     *, t, tm, tiles_a, n_blk):
    i = pl.program_id(0)

    def zero_fill(start):
        def go(cp):
            if start:
                cp.start()
            else:
                cp.wait()

        def per_expert(e, carry):
            s = fs_ref[e]
            head = (-s) & (SUBLANES - 1)
            for j in range(SUBLANES - 1):
                @pl.when(j < head)
                def _(j=j):
                    go(_row_copy(zbuf, 0, xs_ref, s + j, zsem))
            ln = fl_ref[e] - head
            off = s + head
            bit = tm // 2
            while bit >= SUBLANES:
                @pl.when((ln & bit) != 0)
                def _(off=off, bit=bit):
                    dst = xs_ref.at[pl.ds(pl.multiple_of(off, SUBLANES), bit)]
                    go(pltpu.make_async_copy(zbuf.at[pl.ds(0, bit)], dst, zsem))
                off = off + (ln & bit)
                bit //= 2
            return carry

        lax.fori_loop(0, N_EXPERTS, per_expert, 0)

        def per_block(b, carry):
            go(pltpu.make_async_copy(zbuf, xs_ref.at[pl.ds(pl.multiple_of(b * tm, tm), tm)], zsem))
            return carry

        lax.fori_loop(nu_ref[0], n_blk, per_block, 0)

    @pl.when(i == 0)
    def _():
        zbuf[...] = jnp.zeros_like(zbuf)
        zero_fill(True)

    def scatter(x_ref):
        def issue(n, carry):
            for k in range(TOP_K):
                _row_copy(x_ref, n, xs_ref, dest_ref[k, n], sem).start()
            return carry

        lax.fori_loop(0, t, issue, 0)

        def drain(n, carry):
            for k in range(TOP_K):
                _row_copy(x_ref, n, xs_ref, dest_ref[k, n], sem).wait()
            return carry

        lax.fori_loop(0, t, drain, 0)

    @pl.when(i < tiles_a)
    def _():
        scatter(xa_ref)

    @pl.when(i >= tiles_a)
    def _():
        scatter(xb_ref)

    @pl.when(i == 0)
    def _():
        zero_fill(False)


def _dispatch(fill_start, fill_len, nu, dest, x2a, x2b, *, n_blk, tm):
    d = x2a.shape[1]
    t = TOKEN_TILE
    assert x2a.shape[0] % t == 0 and x2b.shape[0] % t == 0
    tiles_a = x2a.shape[0] // t
    n = x2a.shape[0] + x2b.shape[0]
    return pl.pallas_call(
        functools.partial(_dispatch_kernel, t=t, tm=tm, tiles_a=tiles_a, n_blk=n_blk),
        grid_spec=pltpu.PrefetchScalarGridSpec(
            num_scalar_prefetch=3,
            grid=(n // t,),
            in_specs=[pl.BlockSpec((TOP_K, t), lambda i, *_: (0, i), memory_space=pltpu.SMEM),
                      *_two_part_specs(t, d, tiles_a)],
            out_specs=pl.BlockSpec(memory_space=pl.ANY),
            scratch_shapes=[pltpu.VMEM((tm, d), F32), pltpu.SemaphoreType.DMA(()), pltpu.SemaphoreType.DMA(())],
        ),
        out_shape=jax.ShapeDtypeStruct((n_blk * tm, d), F32),
        compiler_params=pltpu.CompilerParams(dimension_semantics=("arbitrary",), vmem_limit_bytes=VMEM_LIMIT),
        name="dispatch",
    )(fill_start, fill_len, nu, dest, x2a, x2b)


def _experts_kernel(be_ref, bv_ref, nu_ref, x_ref, w1_ref, w3_ref, w2_ref, y_ref, w1b, w3b, w2b, *, tm):
    i = pl.program_id(0)
    e = be_ref[i]
    prev = be_ref[jnp.maximum(i - 1, 0)]

    @pl.when(jnp.logical_or(i == 0, e != prev))
    def _():
        w1b[...] = w1_ref[0].astype(BF16)
        w3b[...] = w3_ref[0].astype(BF16)
        w2b[...] = w2_ref[0].astype(BF16)

    @pl.when(i < nu_ref[0])
    def _():
        live = lax.broadcasted_iota(I32, (tm, 1), 0) < bv_ref[i]
        x = jnp.where(live, x_ref[...], 0.0).astype(BF16)
        a = _dot(x, w1b[...])
        h = (a * jax.nn.sigmoid(a)) * _dot(x, w3b[...])
        y_ref[...] = _dot(h.astype(BF16), w2b[...])

    @pl.when(i >= nu_ref[0])
    def _():
        y_ref[...] = jnp.zeros_like(y_ref)


def _experts(be, bv, nu, xs, w1, w3, w2, *, tm):
    n_slot, d = xs.shape
    n_blk = n_slot // tm
    de = w1.shape[-1]
    xmap = lambda i, be, bv, nu: (jnp.minimum(i, nu[0] - 1), 0)
    ymap = lambda i, be, bv, nu: (i, 0)
    wmap = lambda i, be, bv, nu: (be[i], 0, 0)
    return pl.pallas_call(
        functools.partial(_experts_kernel, tm=tm),
        grid_spec=pltpu.PrefetchScalarGridSpec(
            num_scalar_prefetch=3,
            grid=(n_blk,),
            in_specs=[pl.BlockSpec((tm, d), xmap),
                      pl.BlockSpec((1, d, de), wmap), pl.BlockSpec((1, d, de), wmap),
                      pl.BlockSpec((1, de, d), wmap)],
            out_specs=pl.BlockSpec((tm, d), ymap),
            scratch_shapes=[pltpu.VMEM((d, de), BF16), pltpu.VMEM((d, de), BF16), pltpu.VMEM((de, d), BF16)],
        ),
        out_shape=jax.ShapeDtypeStruct((n_slot, d), F32),
        compiler_params=pltpu.CompilerParams(dimension_semantics=("arbitrary",), vmem_limit_bytes=VMEM_LIMIT),
        name="experts",
    )(be, bv, nu, xs, w1, w3, w2)


def _combine_kernel(dcur_ref, dnxt_ref, x2_ref, g_ref, w1_ref, w3_ref, w2_ref, lg_ref, lb_ref, ys_ref,
                    out_ref, buf, sem, *, t):
    i = pl.program_id(0)
    nsteps = pl.num_programs(0)
    slot = i % 2

    def gather(d_ref, s, start):
        def body(n, carry):
            for k in range(TOP_K):
                cp = pltpu.make_async_copy(ys_ref.at[pl.ds(d_ref[k, n], 1)], buf.at[s, k, pl.ds(n, 1)], sem.at[s])
                if start:
                    cp.start()
                else:
                    cp.wait()
            return carry

        lax.fori_loop(0, t, body, 0)

    @pl.when(i == 0)
    def _():
        gather(dcur_ref, 0, True)

    @pl.when(i + 1 < nsteps)
    def _():
        gather(dnxt_ref, 1 - slot, True)

    gather(dcur_ref, slot, False)

    x2 = x2_ref[...]
    g = g_ref[...]
    acc = g[:, 0:1] * buf[slot, 0]
    for k in range(1, TOP_K):
        acc = acc + g[:, k:k + 1] * buf[slot, k]
    xb = x2.astype(BF16)
    a = _dot(xb, w1_ref[...])
    hs = (a * jax.nn.sigmoid(a)) * _dot(xb, w3_ref[...])
    shared = _dot(hs.astype(BF16), w2_ref[...])
    out_ref[...] = _ln(x2 * ALPHA + (acc + shared), lg_ref[...], lb_ref[...])


def _combine(dest, x2, gate_t, w1s, w3s, w2s, lg, lb, ys, *, row_off):
    n_rows, d = x2.shape
    n = dest.shape[1]
    t = TOKEN_TILE
    assert row_off % t == 0 and n_rows % t == 0 and n % t == 0
    off = row_off // t
    last = n // t - 1
    de = w1s.shape[-1]
    const = lambda shape: pl.BlockSpec(shape, lambda i: (0,) * len(shape))
    return pl.pallas_call(
        functools.partial(_combine_kernel, t=t),
        grid=(n_rows // t,),
        in_specs=[pl.BlockSpec((TOP_K, t), lambda i: (0, off + i), memory_space=pltpu.SMEM),
                  pl.BlockSpec((TOP_K, t), lambda i: (0, jnp.minimum(off + i + 1, last)), memory_space=pltpu.SMEM),
                  pl.BlockSpec((t, d), lambda i: (i, 0)),
                  pl.BlockSpec((t, TOP_K), lambda i: (off + i, 0)),
                  const((d, de)), const((d, de)), const((de, d)), const((1, d)), const((1, d)),
                  pl.BlockSpec(memory_space=pl.ANY)],
        out_specs=pl.BlockSpec((t, d), lambda i: (i, 0)),
        out_shape=jax.ShapeDtypeStruct((n_rows, d), F32),
        scratch_shapes=[pltpu.VMEM((2, TOP_K, t, d), F32), pltpu.SemaphoreType.DMA((2,))],
        compiler_params=pltpu.CompilerParams(dimension_semantics=("arbitrary",), vmem_limit_bytes=VMEM_LIMIT),
        name="combine",
    )(dest, dest, x2, gate_t, w1s, w3s, w2s, lg, lb, ys)


EXPERT_TM = 256


def _moe_plan(top_e, rank, cnt, n_blk):
    tm = EXPERT_TM
    counts = cnt[:, 0].astype(I32)
    padded = (counts + tm - 1) // tm * tm
    pad_end = jnp.cumsum(padded)
    pad_start = pad_end - padded
    dest = pad_start[top_e] + rank
    n_used = pad_end[-1] // tm
    blk = jnp.minimum(jnp.arange(n_blk, dtype=I32), n_used - 1)
    be = jnp.minimum(jnp.searchsorted(pad_end, blk * tm, side='right'), N_EXPERTS - 1).astype(I32)
    bv = jnp.clip(pad_start[be] + counts[be] - blk * tm, 0, tm).astype(I32)
    fill_start = (pad_start + counts).astype(I32)
    fill_len = (padded - counts).astype(I32)
    return dest.astype(I32), be, bv, n_used.reshape(1).astype(I32), fill_start, fill_len


PAST_LEN = 1024


def kernel(x_prompt, x_sample, state_gla, cache_pool, cache_mem_k, cache_mem_v, mem_prompt, ln_in_g, ln_in_b, w_in, w_gate_up, b_gate, gla_norm_g, pool_w, pool_scale, w_out, ln1_g, ln1_b, wq_mem, wk_mem, wv_mem, wo_mem, ln2_g, ln2_b, w_router, router_bias, w1_exp, w3_exp, w2_exp, w1_sh, w3_sh, w2_sh, ln3_g, ln3_b):
    assert w_in.shape[0] == 1, "single-layer trunk"
    bp, lp, d = x_prompt.shape
    bs, ls, _ = x_sample.shape
    n_p, n_s = bp * lp, bs * ls
    n_all = n_p + n_s

    wts_a = _prep_trunk_a_weights(ln_in_g, ln_in_b, w_in[0], w_gate_up[0], b_gate[0], gla_norm_g[0], pool_w[0],
                                  pool_scale[0], w_out[0], ln1_g[0], ln1_b[0])
    s0 = jnp.zeros((bp, GLA_HEADS, GLA_DK, GLA_DV), F32)
    h0 = jnp.zeros((bp, POOL_HIST, POOL_WIDTH), F32)
    x1p, sp, hp = _trunk_a(x_prompt, s0, h0, wts_a, start_pos=0)
    x1s, ss, hs = _trunk_a(x_sample, state_gla[0], cache_pool[0], wts_a, start_pos=PAST_LEN)

    mk, mv, mkb, mvb = _mem_kv(mem_prompt.reshape(bp * N_MEM, d), wk_mem[0].astype(BF16), wv_mem[0].astype(BF16))
    wq, wo = wq_mem[0].astype(BF16), wo_mem[0].astype(BF16)
    g2, b2 = ln2_g[0].reshape(1, d), ln2_b[0].reshape(1, d)
    x2p = _attn(x1p, mkb.reshape(bp, N_MEM, d), mvb.reshape(bp, N_MEM, d), wq, wo, g2, b2)
    x2s = _attn(x1s, cache_mem_k[0].reshape(bs, N_MEM, d).astype(BF16),
                cache_mem_v[0].reshape(bs, N_MEM, d).astype(BF16), wq, wo, g2, b2)

    wrt = w_router[0].T
    wrt_h = wrt.astype(BF16)
    wrt_m = (wrt - wrt_h.astype(F32)).astype(BF16)
    top_e, gate, rank, cnt = _router(x2p, x2s, jnp.stack([wrt_h, wrt_m]), router_bias[0].reshape(N_EXPERTS, 1))

    assert (n_all * TOP_K) % EXPERT_TM == 0
    n_blk = n_all * TOP_K // EXPERT_TM + N_EXPERTS
    dest, be, bv, nu, fill_start, fill_len = _moe_plan(top_e, rank, cnt, n_blk)
    xs = _dispatch(fill_start, fill_len, nu, dest, x2p, x2s, n_blk=n_blk, tm=EXPERT_TM)
    ys = _experts(be, bv, nu, xs, w1_exp[0], w3_exp[0], w2_exp[0], tm=EXPERT_TM)
    sh = (w1_sh[0].astype(BF16), w3_sh[0].astype(BF16), w2_sh[0].astype(BF16),
          ln3_g[0].reshape(1, d), ln3_b[0].reshape(1, d))
    gate_t = gate.T
    yp = _combine(dest, x2p, gate_t, *sh, ys, row_off=0)
    ysm = _combine(dest, x2s, gate_t, *sh, ys, row_off=n_p)

    return (yp.reshape(bp, lp, d), ysm.reshape(bs, ls, d), sp[None], hp[None],
            mk.reshape(1, bp, N_MEM, MEM_HEADS, MEM_DH), mv.reshape(1, bp, N_MEM, MEM_HEADS, MEM_DH),
            ss[None], hs[None])
```

```python
import functools

import jax
import jax.numpy as jnp
from jax import lax
from jax.experimental import pallas as pl
from jax.experimental.pallas import tpu as pltpu

F32 = jnp.float32
BF16 = jnp.bfloat16
I32 = jnp.int32

D_MODEL = 1024
CHUNK = 64
SUB = 16
GLA_HEADS = 4
GLA_DK = 64
GLA_DV = 128
GLA_KEY = GLA_HEADS * GLA_DK
GLA_WIDTH = GLA_HEADS * GLA_DV
GATE_RANK = 16
GATE_PAD = 128
POOL_WIDTH = 512
POOL_WINDOWS = (2, 4, 8, 16)
POOL_GC = 128
POOL_HIST = 15
HIST_ROWS = 16
N_MEM = 256
MEM_HEADS = 4
MEM_DH = 256
N_EXPERTS = 256
N_GROUPS = 8
EXPERTS_PER_GROUP = 32
TOPK_GROUPS = 4
TOP_K = 8
ROUTED_SCALE = 2.5
D_EXPERT = 256
TOKEN_TILE = 256
SUBLANES, LANES = 8, 128
assert D_MODEL == SUBLANES * LANES
ALPHA = 2.0 ** 0.25
EPS = 1e-5
OFF_Q, OFF_K, OFF_V, OFF_G, OFF_U, OFF_GD = 0, 256, 512, 1024, 1536, 2048
D_IN_PAD = OFF_GD + GATE_PAD

VMEM_LIMIT = 56 * 1024 * 1024


def _ln(x, g, b):
    mu = jnp.mean(x, axis=-1, keepdims=True)
    xc = x - mu
    var = jnp.mean(xc * xc, axis=-1, keepdims=True)
    return xc * lax.rsqrt(var + EPS) * g + b


def _dot(a, b):
    return jnp.dot(a, b, preferred_element_type=F32)


def _dot_nt(a, b):
    return lax.dot_general(a, b, (((1,), (1,)), ((), ())), preferred_element_type=F32)


def _dot_tn(a, b):
    return lax.dot_general(a, b, (((0,), (0,)), ((), ())), preferred_element_type=F32)


def _split3(x):
    h = x.astype(BF16)
    r = x - h.astype(F32)
    m = r.astype(BF16)
    l = (r - m.astype(F32)).astype(BF16)
    return h, m, l


def _trunk_a_kernel(x_ref, s0_ref, h0_ref, lng_ref, lnb_ref, win_ref, wgu_ref, bg_ref, gng_ref,
                    pw_ref, ps_ref, wout_ref, l1g_ref, l1b_ref,
                    x1_ref, sn_ref, hn_ref,
                    proj_scr, ext_scr, s_scr, op_scr, *, tl, chunk, start_pos):
    t = pl.program_id(1)
    nt = pl.num_programs(1)

    @pl.when(t == 0)
    def _():
        s_scr[...] = s0_ref[0]
        ext_scr[0:1, :] = jnp.zeros((1, POOL_WIDTH), F32)
        ext_scr[1:HIST_ROWS, :] = h0_ref[0]

    xn = _ln(x_ref[0], lng_ref[...], lnb_ref[...])
    proj_scr[...] = _dot(xn.astype(BF16), win_ref[...])

    c = chunk
    rows = lax.broadcasted_iota(I32, (c, c), 0)
    cols = lax.broadcasted_iota(I32, (c, c), 1)
    tri = (cols <= rows).astype(BF16)
    causal = cols <= rows
    eye_dk = lax.broadcasted_iota(I32, (GLA_DK, GLA_DK), 0) == lax.broadcasted_iota(I32, (GLA_DK, GLA_DK), 1)
    n_sub = c // SUB
    rblk = lax.broadcasted_iota(I32, (c, n_sub * GLA_DK), 0) // SUB
    lblk = lax.broadcasted_iota(I32, (c, n_sub * GLA_DK), 1) // GLA_DK
    lblk_row = lax.broadcasted_iota(I32, (1, n_sub * GLA_DK), 1) // GLA_DK
    mask_q = rblk == lblk
    mask_k = rblk <= lblk

    def tile_lanes(a):
        return jnp.concatenate([a] * n_sub, axis=1)

    def chunk_body(ci, carry):
        r0 = pl.multiple_of(ci * c, c)
        rs = pl.ds(r0, c)
        gd = proj_scr[rs, OFF_GD:OFF_GD + GATE_PAD]
        z = _dot(gd.astype(BF16), wgu_ref[...]) + bg_ref[...]
        lf = (jnp.minimum(z, 0.0) - jnp.log1p(jnp.exp(-jnp.abs(z)))) * (1.0 / 16.0)
        lh, lm, ll = _split3(lf)
        cum = _dot(tri, lh) + _dot(tri, lm) + _dot(tri, ll)
        last = cum[c - 1:c, :]
        q_all = proj_scr[rs, OFF_Q:OFF_Q + GLA_KEY] * (GLA_DK ** -0.5)
        k_all = proj_scr[rs, OFF_K:OFF_K + GLA_KEY]
        qs_all = q_all * jnp.exp(cum)
        ks_all = k_all * jnp.exp(last - cum)
        for h in range(GLA_HEADS):
            ksl = slice(h * GLA_DK, (h + 1) * GLA_DK)
            vsl = slice(h * GLA_DV, (h + 1) * GLA_DV)
            v_h = proj_scr[rs, OFF_V + h * GLA_DV:OFF_V + (h + 1) * GLA_DV].astype(BF16)
            cum_t = tile_lanes(cum[:, ksl])
            q_t = tile_lanes(q_all[:, ksl])
            k_t = tile_lanes(k_all[:, ksl])
            ref_row = jnp.zeros((1, n_sub * GLA_DK), F32)
            for i in range(1, n_sub):
                ref_row = jnp.where(lblk_row == i, cum_t[i * SUB - 1:i * SUB, :], ref_row)
            arg = cum_t - ref_row
            lhs = jnp.where(mask_q, q_t * jnp.exp(jnp.where(mask_q, arg, 0.0)), 0.0)
            rhs = jnp.where(mask_k, k_t * jnp.exp(jnp.where(mask_k, -arg, 0.0)), 0.0)
            att = jnp.where(causal, _dot_nt(lhs.astype(BF16), rhs.astype(BF16)), 0.0)
            s_h = s_scr[h]
            o_h = _dot(att.astype(BF16), v_h) + _dot(qs_all[:, ksl].astype(BF16), s_h.astype(BF16))
            dcol = jnp.sum(jnp.where(eye_dk, jnp.broadcast_to(jnp.exp(last[:, ksl]), (GLA_DK, GLA_DK)), 0.0),
                           axis=1, keepdims=True)
            s_scr[h] = dcol * s_h + _dot_tn(ks_all[:, ksl].astype(BF16), v_h)
            o_h = o_h * lax.rsqrt(jnp.mean(o_h * o_h, axis=-1, keepdims=True) + EPS) * gng_ref[...]
            g_h = proj_scr[rs, OFF_G + h * GLA_DV:OFF_G + (h + 1) * GLA_DV]
            op_scr[rs, vsl] = (o_h * (g_h * jax.nn.sigmoid(g_h))).astype(BF16)
        return carry

    lax.fori_loop(0, tl // c, chunk_body, 0)

    u = proj_scr[:, OFF_U:OFF_U + POOL_WIDTH]
    ext_scr[HIST_ROWS:HIST_ROWS + tl, :] = u
    n_valid = start_pos + t * tl + lax.broadcasted_iota(I32, (tl, 1), 0) + 1
    for gi, w in enumerate(POOL_WINDOWS):
        lsl = slice(gi * POOL_GC, (gi + 1) * POOL_GC)
        win = ext_scr[HIST_ROWS:HIST_ROWS + tl, lsl]
        for s in range(1, w):
            win = win + ext_scr[HIST_ROWS - s:HIST_ROWS - s + tl, lsl]
        cnt = jnp.minimum(w, n_valid).astype(F32)
        r = win / cnt - u[:, lsl]
        p = _dot(r.astype(BF16), pw_ref[gi]) * ps_ref[:, lsl]
        op_scr[:, GLA_WIDTH + gi * POOL_GC:GLA_WIDTH + (gi + 1) * POOL_GC] = p.astype(BF16)
    tail = ext_scr[tl:tl + HIST_ROWS, :]
    ext_scr[0:HIST_ROWS, :] = tail

    mix = _dot(op_scr[...], wout_ref[...])
    x1_ref[0] = _ln(xn * ALPHA + mix, l1g_ref[...], l1b_ref[...])

    @pl.when(t == nt - 1)
    def _():
        sn_ref[0] = s_scr[...]
        hn_ref[0] = ext_scr[1:HIST_ROWS, :]


def _trunk_a(x, s0, h0, wts, *, start_pos):
    b, l, d = x.shape
    tl = min(l, 256)
    chunk = min(tl, CHUNK)
    assert l % tl == 0 and tl % chunk == 0 and chunk % SUB == 0 and l >= HIST_ROWS
    nt = l // tl
    kern = functools.partial(_trunk_a_kernel, tl=tl, chunk=chunk, start_pos=start_pos)
    const = lambda shape: pl.BlockSpec(shape, lambda bi, ti: (0,) * len(shape))
    return pl.pallas_call(
        kern,
        grid=(b, nt),
        in_specs=[
            pl.BlockSpec((1, tl, d), lambda bi, ti: (bi, ti, 0)),
            pl.BlockSpec((1, GLA_HEADS, GLA_DK, GLA_DV), lambda bi, ti: (bi, 0, 0, 0)),
            pl.BlockSpec((1, POOL_HIST, POOL_WIDTH), lambda bi, ti: (bi, 0, 0)),
            const((1, d)), const((1, d)),
            const((d, D_IN_PAD)), const((GATE_PAD, GLA_KEY)), const((1, GLA_KEY)), const((1, GLA_DV)),
            const((len(POOL_WINDOWS), POOL_GC, POOL_GC)), const((1, POOL_WIDTH)),
            const((GLA_WIDTH + POOL_WIDTH, d)), const((1, d)), const((1, d)),
        ],
        out_specs=[
            pl.BlockSpec((1, tl, d), lambda bi, ti: (bi, ti, 0)),
            pl.BlockSpec((1, GLA_HEADS, GLA_DK, GLA_DV), lambda bi, ti: (bi, 0, 0, 0)),
            pl.BlockSpec((1, POOL_HIST, POOL_WIDTH), lambda bi, ti: (bi, 0, 0)),
        ],
        out_shape=[
            jax.ShapeDtypeStruct((b, l, d), F32),
            jax.ShapeDtypeStruct((b, GLA_HEADS, GLA_DK, GLA_DV), F32),
            jax.ShapeDtypeStruct((b, POOL_HIST, POOL_WIDTH), F32),
        ],
        scratch_shapes=[
            pltpu.VMEM((tl, D_IN_PAD), F32),
            pltpu.VMEM((HIST_ROWS + tl, POOL_WIDTH), F32),
            pltpu.VMEM((GLA_HEADS, GLA_DK, GLA_DV), F32),
            pltpu.VMEM((tl, GLA_WIDTH + POOL_WIDTH), BF16),
        ],
        compiler_params=pltpu.CompilerParams(
            dimension_semantics=("arbitrary", "arbitrary"), vmem_limit_bytes=VMEM_LIMIT),
        name="trunk_a",
    )(x, s0, h0, *wts)


def _prep_trunk_a_weights(ln_in_g, ln_in_b, w_in, w_gate_up, b_gate, gla_norm_g, pool_w, pool_scale, w_out,
                          ln1_g, ln1_b):
    d = D_MODEL
    p_gd = 2 * GLA_KEY + 2 * GLA_WIDTH
    w_in_r = jnp.concatenate(
        [w_in[:, :p_gd], w_in[:, p_gd + GATE_RANK:], w_in[:, p_gd:p_gd + GATE_RANK],
         jnp.zeros((d, GATE_PAD - GATE_RANK), w_in.dtype)], axis=1).astype(BF16)
    wgu = jnp.concatenate([w_gate_up, jnp.zeros((GATE_PAD - GATE_RANK, GLA_KEY), w_gate_up.dtype)],
                          axis=0).astype(BF16)
    return (ln_in_g.reshape(1, d), ln_in_b.reshape(1, d), w_in_r, wgu, b_gate.reshape(1, GLA_KEY),
            gla_norm_g.reshape(1, GLA_DV), pool_w.astype(BF16), pool_scale.reshape(1, POOL_WIDTH),
            w_out.astype(BF16), ln1_g.reshape(1, d), ln1_b.reshape(1, d))


def _mem_kv_kernel(m_ref, wk_ref, wv_ref, k_ref, v_ref, kb_ref, vb_ref):
    m = m_ref[...].astype(BF16)
    k = _dot(m, wk_ref[...])
    v = _dot(m, wv_ref[...])
    k_ref[...] = k
    v_ref[...] = v
    kb_ref[...] = k.astype(BF16)
    vb_ref[...] = v.astype(BF16)


def _mem_kv(mem, wk, wv):
    m, d = mem.shape
    tm = min(m, 512)
    assert m % tm == 0
    row = pl.BlockSpec((tm, d), lambda i: (i, 0))
    wspec = pl.BlockSpec((d, d), lambda i: (0, 0))
    return pl.pallas_call(
        _mem_kv_kernel,
        grid=(m // tm,),
        in_specs=[row, wspec, wspec],
        out_specs=[row, row, row, row],
        out_shape=[jax.ShapeDtypeStruct((m, d), F32), jax.ShapeDtypeStruct((m, d), F32),
                   jax.ShapeDtypeStruct((m, d), BF16), jax.ShapeDtypeStruct((m, d), BF16)],
        compiler_params=pltpu.CompilerParams(dimension_semantics=("arbitrary",), vmem_limit_bytes=VMEM_LIMIT),
        name="mem_kv",
    )(mem, wk, wv)


def _attn_kernel(x1_ref, k_ref, v_ref, wq_ref, wo_ref, g_ref, b_ref, x2_ref, o_scr):
    x1 = x1_ref[0]
    q = (_dot(x1.astype(BF16), wq_ref[...]) * (MEM_DH ** -0.5)).astype(BF16)
    for h in range(MEM_HEADS):
        hs = slice(h * MEM_DH, (h + 1) * MEM_DH)
        s = _dot_nt(q[:, hs], k_ref[0, :, hs])
        e = jnp.exp(s - jnp.max(s, axis=-1, keepdims=True))
        p = e / jnp.sum(e, axis=-1, keepdims=True)
        o_scr[:, hs] = _dot(p.astype(BF16), v_ref[0, :, hs]).astype(BF16)
    attn = _dot(o_scr[...], wo_ref[...])
    x2_ref[...] = _ln(x1 * ALPHA + attn, g_ref[...], b_ref[...])


def _attn(x1, mem_k, mem_v, wq, wo, g, b):
    bsz, l, d = x1.shape
    tl = min(l, 256)
    assert l % tl == 0
    nt = l // tl
    const = lambda shape: pl.BlockSpec(shape, lambda bi, ti: (0,) * len(shape))
    return pl.pallas_call(
        _attn_kernel,
        grid=(bsz, nt),
        in_specs=[
            pl.BlockSpec((1, tl, d), lambda bi, ti: (bi, ti, 0)),
            pl.BlockSpec((1, N_MEM, d), lambda bi, ti: (bi, 0, 0)),
            pl.BlockSpec((1, N_MEM, d), lambda bi, ti: (bi, 0, 0)),
            const((d, d)), const((d, d)), const((1, d)), const((1, d)),
        ],
        out_specs=pl.BlockSpec((tl, d), lambda bi, ti: (bi * nt + ti, 0)),
        out_shape=jax.ShapeDtypeStruct((bsz * l, d), F32),
        scratch_shapes=[pltpu.VMEM((tl, d), BF16)],
        compiler_params=pltpu.CompilerParams(
            dimension_semantics=("arbitrary", "arbitrary"), vmem_limit_bytes=VMEM_LIMIT),
        name="attn",
    )(x1, mem_k, mem_v, wq, wo, g, b)


def _two_part_specs(t, d, tiles_a):
    spec_a = pl.BlockSpec((t, d), lambda i, *_: (jnp.minimum(i, tiles_a - 1), 0))
    spec_b = pl.BlockSpec((t, d), lambda i, *_: (jnp.maximum(i - tiles_a, 0), 0))
    return spec_a, spec_b


def _router_kernel(xa_ref, xb_ref, wrt_ref, bias_ref, e_ref, g_ref, r_ref, cnt_ref, cnt_scr, *, tl, tiles_a):
    i = pl.program_id(0)

    @pl.when(i == 0)
    def _():
        cnt_scr[...] = jnp.zeros_like(cnt_scr)

    x = jnp.where(i < tiles_a, xa_ref[...], xb_ref[...])
    xh = x.astype(BF16)
    xm = (x - xh.astype(F32)).astype(BF16)
    wh = wrt_ref[0]
    wm = wrt_ref[1]
    logits = _dot_nt(wh, xh) + (_dot_nt(wh, xm) + _dot_nt(wm, xh))
    scores = jax.nn.sigmoid(logits)
    biased = scores + bias_ref[...]
    ninf = -jnp.inf
    eg = EXPERTS_PER_GROUP
    riota = lax.broadcasted_iota(I32, (eg, tl), 0)
    gs_rows = []
    for g in range(N_GROUPS):
        blk = biased[g * eg:(g + 1) * eg, :]
        m1 = jnp.max(blk, axis=0, keepdims=True)
        i1 = jnp.min(jnp.where(blk == m1, riota, eg), axis=0, keepdims=True)
        m2 = jnp.max(jnp.where(riota == i1, ninf, blk), axis=0, keepdims=True)
        gs_rows.append(m1 + m2)
    gs = jnp.concatenate(gs_rows, axis=0)
    giota = lax.broadcasted_iota(I32, (N_GROUPS, tl), 0)
    sel = jnp.zeros((N_GROUPS, tl), jnp.bool_)
    for _ in range(TOPK_GROUPS):
        m = jnp.max(gs, axis=0, keepdims=True)
        gi = jnp.min(jnp.where(gs == m, giota, N_GROUPS), axis=0, keepdims=True)
        hit = giota == gi
        sel = jnp.logical_or(sel, hit)
        gs = jnp.where(hit, ninf, gs)
    self = jnp.where(sel, 1.0, 0.0)
    masked = jnp.concatenate(
        [jnp.where(self[g:g + 1, :] > 0.5, biased[g * eg:(g + 1) * eg, :], ninf) for g in range(N_GROUPS)], axis=0)
    eiota = lax.broadcasted_iota(I32, (N_EXPERTS, tl), 0)
    idx_rows, sc_rows = [], []
    multi = jnp.zeros((N_EXPERTS, tl), F32)
    for _ in range(TOP_K):
        m = jnp.max(masked, axis=0, keepdims=True)
        idx = jnp.min(jnp.where(masked == m, eiota, N_EXPERTS), axis=0, keepdims=True)
        hit = eiota == idx
        sc_rows.append(jnp.sum(jnp.where(hit, scores, 0.0), axis=0, keepdims=True))
        idx_rows.append(idx)
        multi = jnp.where(hit, 1.0, multi)
        masked = jnp.where(hit, ninf, masked)
    top_e = jnp.concatenate(idx_rows, axis=0)
    sc = jnp.concatenate(sc_rows, axis=0)
    e_ref[...] = top_e
    g_ref[...] = sc / jnp.sum(sc, axis=0, keepdims=True) * ROUTED_SCALE
    mh = multi.astype(BF16)
    before = (lax.broadcasted_iota(I32, (tl, tl), 0) < lax.broadcasted_iota(I32, (tl, tl), 1)).astype(BF16)
    running = cnt_scr[...]
    rankmat = _dot(mh, before) + jnp.concatenate([running] * (tl // 128), axis=1)
    r_rows = [jnp.sum(jnp.where(eiota == idx_rows[k], rankmat, 0.0), axis=0, keepdims=True) for k in range(TOP_K)]
    r_ref[...] = jnp.concatenate(r_rows, axis=0).astype(I32)
    total = running + _dot(mh, jnp.ones((tl, 128), BF16))
    cnt_scr[...] = total
    cnt_ref[...] = total


def _router(x2a, x2b, wrt, bias_col):
    d = x2a.shape[1]
    tl = TOKEN_TILE
    assert x2a.shape[0] % tl == 0 and x2b.shape[0] % tl == 0
    tiles_a = x2a.shape[0] // tl
    n = x2a.shape[0] + x2b.shape[0]
    kspec = pl.BlockSpec((TOP_K, tl), lambda i: (0, i))
    return pl.pallas_call(
        functools.partial(_router_kernel, tl=tl, tiles_a=tiles_a),
        grid=(n // tl,),
        in_specs=[*_two_part_specs(tl, d, tiles_a),
                  pl.BlockSpec((2, N_EXPERTS, d), lambda i: (0, 0, 0)),
                  pl.BlockSpec((N_EXPERTS, 1), lambda i: (0, 0))],
        out_specs=[kspec, kspec, kspec, pl.BlockSpec((N_EXPERTS, 128), lambda i: (0, 0))],
        out_shape=[jax.ShapeDtypeStruct((TOP_K, n), I32), jax.ShapeDtypeStruct((TOP_K, n), F32),
                   jax.ShapeDtypeStruct((TOP_K, n), I32), jax.ShapeDtypeStruct((N_EXPERTS, 128), F32)],
        scratch_shapes=[pltpu.VMEM((N_EXPERTS, 128), F32)],
        compiler_params=pltpu.CompilerParams(dimension_semantics=("arbitrary",), vmem_limit_bytes=VMEM_LIMIT),
        name="router",
    )(x2a, x2b, wrt, bias_col)


def _to_token_tiles(x, dst_ref, t):
    for s in range(SUBLANES):
        dst_ref[pl.ds(s, t, stride=SUBLANES), :] = x[:, s * LANES:(s + 1) * LANES]


def _from_token_tiles(src_ref, t):
    return jnp.concatenate([src_ref[pl.ds(s, t, stride=SUBLANES), :] for s in range(SUBLANES)], axis=1)


def _tile_rows(ref, row0, n_tok=1):
    return ref.at[pl.ds(pl.multiple_of(row0, SUBLANES), n_tok * SUBLANES)]


def _dispatch_kernel(fs_ref, fl_ref, nu_ref, dest_ref, xa_ref, xb_ref, xs_ref, stage, zbuf, sem, zsem,
                     *, t, tm, tiles_a, n_blk):
    i = pl.program_id(0)

    def zero_fill(start):
        def go(cp):
            if start:
                cp.start()
            else:
                cp.wait()

        def per_expert(e, carry):
            ln = fl_ref[e]
            off = fs_ref[e]
            bit = tm // 2
            while bit:
                @pl.when((ln & bit) != 0)
                def _(off=off, bit=bit):
                    go(pltpu.make_async_copy(_tile_rows(zbuf, 0, bit), _tile_rows(xs_ref, off * SUBLANES, bit), zsem))
                off = off + (ln & bit)
                bit //= 2
            return carry

        lax.fori_loop(0, N_EXPERTS, per_expert, 0)

        def per_block(b, carry):
            go(pltpu.make_async_copy(zbuf, _tile_rows(xs_ref, b * (tm * SUBLANES), tm), zsem))
            return carry

        lax.fori_loop(nu_ref[0], n_blk, per_block, 0)

    @pl.when(i == 0)
    def _():
        zbuf[...] = jnp.zeros_like(zbuf)
        zero_fill(True)

    _to_token_tiles(jnp.where(i < tiles_a, xa_ref[...], xb_ref[...]), stage, t)

    def scatter(start):
        def body(n, carry):
            src = _tile_rows(stage, n * SUBLANES)
            for k in range(TOP_K):
                cp = pltpu.make_async_copy(src, _tile_rows(xs_ref, dest_ref[n * TOP_K + k]), sem)
                if start:
                    cp.start(priority=k % 2)
                else:
                    cp.wait()
            return carry

        lax.fori_loop(0, t, body, 0)

    scatter(True)
    scatter(False)

    @pl.when(i == 0)
    def _():
        zero_fill(False)


def _dispatch(fill_start, fill_len, nu, dest, x2a, x2b, *, n_blk, tm):
    d = x2a.shape[1]
    t = TOKEN_TILE
    assert x2a.shape[0] % t == 0 and x2b.shape[0] % t == 0
    tiles_a = x2a.shape[0] // t
    n = x2a.shape[0] + x2b.shape[0]
    return pl.pallas_call(
        functools.partial(_dispatch_kernel, t=t, tm=tm, tiles_a=tiles_a, n_blk=n_blk),
        grid_spec=pltpu.PrefetchScalarGridSpec(
            num_scalar_prefetch=3,
            grid=(n // t,),
            in_specs=[pl.BlockSpec((t * TOP_K,), lambda i, *_: (i,), memory_space=pltpu.SMEM),
                      *_two_part_specs(t, d, tiles_a)],
            out_specs=pl.BlockSpec(memory_space=pl.ANY),
            scratch_shapes=[pltpu.VMEM((t * SUBLANES, LANES), F32), pltpu.VMEM((tm * SUBLANES, LANES), F32),
                            pltpu.SemaphoreType.DMA(()), pltpu.SemaphoreType.DMA(())],
        ),
        out_shape=jax.ShapeDtypeStruct((n_blk * tm * SUBLANES, LANES), F32),
        compiler_params=pltpu.CompilerParams(dimension_semantics=("arbitrary",), vmem_limit_bytes=VMEM_LIMIT),
        name="dispatch",
    )(fill_start, fill_len, nu, dest, x2a, x2b)


def _experts_kernel(be_ref, bv_ref, nu_ref, x_ref, w1_ref, w3_ref, w2_ref, y_ref, w1b, w3b, w2b, *, tm):
    i = pl.program_id(0)
    e = be_ref[i]
    prev = be_ref[jnp.maximum(i - 1, 0)]

    @pl.when(jnp.logical_or(i == 0, e != prev))
    def _():
        w1b[...] = w1_ref[0].astype(BF16)
        w3b[...] = w3_ref[0].astype(BF16)
        w2b[...] = w2_ref[0].astype(BF16)

    @pl.when(i < nu_ref[0])
    def _():
        live = lax.broadcasted_iota(I32, (tm, 1), 0) < bv_ref[i]
        x = jnp.where(live, _from_token_tiles(x_ref, tm), 0.0).astype(BF16)
        a = _dot(x, w1b[...])
        h = (a * jax.nn.sigmoid(a)) * _dot(x, w3b[...])
        _to_token_tiles(_dot(h.astype(BF16), w2b[...]), y_ref, tm)

    @pl.when(i >= nu_ref[0])
    def _():
        y_ref[...] = jnp.zeros_like(y_ref)


def _experts(be, bv, nu, xs, w1, w3, w2, *, tm):
    blk_rows = tm * SUBLANES
    n_blk = xs.shape[0] // blk_rows
    _, d, de = w1.shape
    xmap = lambda i, be, bv, nu: (jnp.minimum(i, nu[0] - 1), 0)
    ymap = lambda i, be, bv, nu: (i, 0)
    wmap = lambda i, be, bv, nu: (be[i], 0, 0)
    return pl.pallas_call(
        functools.partial(_experts_kernel, tm=tm),
        grid_spec=pltpu.PrefetchScalarGridSpec(
            num_scalar_prefetch=3,
            grid=(n_blk,),
            in_specs=[pl.BlockSpec((blk_rows, LANES), xmap),
                      pl.BlockSpec((1, d, de), wmap), pl.BlockSpec((1, d, de), wmap),
                      pl.BlockSpec((1, de, d), wmap)],
            out_specs=pl.BlockSpec((blk_rows, LANES), ymap),
            scratch_shapes=[pltpu.VMEM((d, de), BF16), pltpu.VMEM((d, de), BF16), pltpu.VMEM((de, d), BF16)],
        ),
        out_shape=jax.ShapeDtypeStruct(xs.shape, F32),
        compiler_params=pltpu.CompilerParams(dimension_semantics=("arbitrary",), vmem_limit_bytes=VMEM_LIMIT),
        name="experts",
    )(be, bv, nu, xs, w1, w3, w2)


def _combine_kernel(dcur_ref, dnxt_ref, x2_ref, g_ref, w1_ref, w3_ref, w2_ref, lg_ref, lb_ref, ys_ref,
                    out_ref, buf, sem, *, t):
    i = pl.program_id(0)
    nsteps = pl.num_programs(0)
    slot = i % 2

    def gather(d_ref, s, start):
        def body(n, carry):
            for k in range(TOP_K):
                cp = pltpu.make_async_copy(_tile_rows(ys_ref, d_ref[n * TOP_K + k]),
                                           _tile_rows(buf.at[s, k], n * SUBLANES), sem.at[s])
                if start:
                    cp.start(priority=k % 2)
                else:
                    cp.wait()
            return carry

        lax.fori_loop(0, t, body, 0)

    @pl.when(i == 0)
    def _():
        gather(dcur_ref, 0, True)

    @pl.when(i + 1 < nsteps)
    def _():
        gather(dnxt_ref, 1 - slot, True)

    gather(dcur_ref, slot, False)

    x2 = x2_ref[...]
    g = g_ref[...]
    acc = g[:, 0:1] * _from_token_tiles(buf.at[slot, 0], t)
    for k in range(1, TOP_K):
        acc = acc + g[:, k:k + 1] * _from_token_tiles(buf.at[slot, k], t)
    xb = x2.astype(BF16)
    a = _dot(xb, w1_ref[...])
    hs = (a * jax.nn.sigmoid(a)) * _dot(xb, w3_ref[...])
    shared = _dot(hs.astype(BF16), w2_ref[...])
    out_ref[...] = _ln(x2 * ALPHA + (acc + shared), lg_ref[...], lb_ref[...])


def _combine(dest, x2, gate_t, w1s, w3s, w2s, lg, lb, ys, *, row_off):
    n_rows, d = x2.shape
    n = dest.shape[0] // TOP_K
    t = TOKEN_TILE
    assert row_off % t == 0 and n_rows % t == 0 and n % t == 0
    off = row_off // t
    last = n // t - 1
    de = w1s.shape[-1]
    const = lambda shape: pl.BlockSpec(shape, lambda i: (0,) * len(shape))
    return pl.pallas_call(
        functools.partial(_combine_kernel, t=t),
        grid=(n_rows // t,),
        in_specs=[pl.BlockSpec((t * TOP_K,), lambda i: (off + i,), memory_space=pltpu.SMEM),
                  pl.BlockSpec((t * TOP_K,), lambda i: (jnp.minimum(off + i + 1, last),), memory_space=pltpu.SMEM),
                  pl.BlockSpec((t, d), lambda i: (i, 0)),
                  pl.BlockSpec((t, TOP_K), lambda i: (off + i, 0)),
                  const((d, de)), const((d, de)), const((de, d)), const((1, d)), const((1, d)),
                  pl.BlockSpec(memory_space=pl.ANY)],
        out_specs=pl.BlockSpec((t, d), lambda i: (i, 0)),
        out_shape=jax.ShapeDtypeStruct((n_rows, d), F32),
        scratch_shapes=[pltpu.VMEM((2, TOP_K, t * SUBLANES, LANES), F32), pltpu.SemaphoreType.DMA((2,))],
        compiler_params=pltpu.CompilerParams(dimension_semantics=("arbitrary",), vmem_limit_bytes=VMEM_LIMIT),
        name="combine",
    )(dest, dest, x2, gate_t, w1s, w3s, w2s, lg, lb, ys)


EXPERT_TM = 256


def _moe_plan(cnt, n_blk):
    tm = EXPERT_TM
    counts = cnt[:, 0].astype(I32)
    padded = (counts + tm - 1) // tm * tm
    pad_end = jnp.cumsum(padded)
    pad_start = pad_end - padded
    n_used = pad_end[-1] // tm
    blk = jnp.minimum(jnp.arange(n_blk, dtype=I32), n_used - 1)
    be = jnp.minimum(jnp.sum((pad_end[None, :] <= (blk * tm)[:, None]).astype(I32), axis=1), N_EXPERTS - 1)
    bv = jnp.clip(pad_start[be] + counts[be] - blk * tm, 0, tm).astype(I32)
    fill_start = (pad_start + counts).astype(I32)
    fill_len = (padded - counts).astype(I32)
    return pad_start.astype(I32), be.astype(I32), bv, n_used.reshape(1).astype(I32), fill_start, fill_len


def _slots_kernel(ps_ref, e_ref, r_ref, d_ref):
    e = e_ref[...]

    def body(j, base):
        return jnp.where(e == j, ps_ref[j], base)

    base = lax.fori_loop(0, N_EXPERTS, body, jnp.zeros_like(e), unroll=8)
    d_ref[...] = (base + r_ref[...]) * SUBLANES


def _slots(pad_start, top_e, rank):
    k, n = top_e.shape
    tl = next(c for c in (2048, 1536, 1024, 512, 256, 128) if n % c == 0)
    spec = pl.BlockSpec((k, tl), lambda i, ps: (0, i))
    return pl.pallas_call(
        _slots_kernel,
        grid_spec=pltpu.PrefetchScalarGridSpec(num_scalar_prefetch=1, grid=(n // tl,), in_specs=[spec, spec],
                                               out_specs=spec),
        out_shape=jax.ShapeDtypeStruct((k, n), I32),
        compiler_params=pltpu.CompilerParams(dimension_semantics=("arbitrary",)),
        name="slots",
    )(pad_start, top_e, rank)


PAST_LEN = 1024


def kernel(x_prompt, x_sample, state_gla, cache_pool, cache_mem_k, cache_mem_v, mem_prompt, ln_in_g, ln_in_b, w_in, w_gate_up, b_gate, gla_norm_g, pool_w, pool_scale, w_out, ln1_g, ln1_b, wq_mem, wk_mem, wv_mem, wo_mem, ln2_g, ln2_b, w_router, router_bias, w1_exp, w3_exp, w2_exp, w1_sh, w3_sh, w2_sh, ln3_g, ln3_b):
    assert w_in.shape[0] == 1, "single-layer trunk"
    bp, lp, d = x_prompt.shape
    bs, ls, _ = x_sample.shape
    n_p, n_s = bp * lp, bs * ls
    n_all = n_p + n_s

    wts_a = _prep_trunk_a_weights(ln_in_g, ln_in_b, w_in[0], w_gate_up[0], b_gate[0], gla_norm_g[0], pool_w[0],
                                  pool_scale[0], w_out[0], ln1_g[0], ln1_b[0])
    s0 = jnp.zeros((bp, GLA_HEADS, GLA_DK, GLA_DV), F32)
    h0 = jnp.zeros((bp, POOL_HIST, POOL_WIDTH), F32)
    x1p, sp, hp = _trunk_a(x_prompt, s0, h0, wts_a, start_pos=0)
    x1s, ss, hs = _trunk_a(x_sample, state_gla[0], cache_pool[0], wts_a, start_pos=PAST_LEN)

    mk, mv, mkb, mvb = _mem_kv(mem_prompt.reshape(bp * N_MEM, d), wk_mem[0].astype(BF16), wv_mem[0].astype(BF16))
    wq, wo = wq_mem[0].astype(BF16), wo_mem[0].astype(BF16)
    g2, b2 = ln2_g[0].reshape(1, d), ln2_b[0].reshape(1, d)
    x2p = _attn(x1p, mkb.reshape(bp, N_MEM, d), mvb.reshape(bp, N_MEM, d), wq, wo, g2, b2)
    x2s = _attn(x1s, cache_mem_k[0].reshape(bs, N_MEM, d).astype(BF16),
                cache_mem_v[0].reshape(bs, N_MEM, d).astype(BF16), wq, wo, g2, b2)

    wrt = w_router[0].T
    wrt_h = wrt.astype(BF16)
    wrt_m = (wrt - wrt_h.astype(F32)).astype(BF16)
    top_e, gate, rank, cnt = _router(x2p, x2s, jnp.stack([wrt_h, wrt_m]), router_bias[0].reshape(N_EXPERTS, 1))

    assert (n_all * TOP_K) % EXPERT_TM == 0
    n_blk = n_all * TOP_K // EXPERT_TM + N_EXPERTS
    pad_start, be, bv, nu, fill_start, fill_len = _moe_plan(cnt, n_blk)
    dest = _slots(pad_start, top_e, rank).T.reshape(-1)
    xs = _dispatch(fill_start, fill_len, nu, dest, x2p, x2s, n_blk=n_blk, tm=EXPERT_TM)
    ys = _experts(be, bv, nu, xs, w1_exp[0], w3_exp[0], w2_exp[0], tm=EXPERT_TM)
    sh = (w1_sh[0].astype(BF16), w3_sh[0].astype(BF16), w2_sh[0].astype(BF16),
          ln3_g[0].reshape(1, d), ln3_b[0].reshape(1, d))
    gate_t = gate.T
    yp = _combine(dest, x2p, gate_t, *sh, ys, row_off=0)
    ysm = _combine(dest, x2s, gate_t, *sh, ys, row_off=n_p)

    return (yp.reshape(bp, lp, d), ysm.reshape(bs, ls, d), sp[None], hp[None],
            mk.reshape(1, bp, N_MEM, MEM_HEADS, MEM_DH), mv.reshape(1, bp, N_MEM, MEM_HEADS, MEM_DH),
            ss[None], hs[None])
```

```python
import functools

import jax
import jax.numpy as jnp
from jax import lax
from jax.experimental import pallas as pl
from jax.experimental.pallas import tpu as pltpu

F32 = jnp.float32
BF16 = jnp.bfloat16
I32 = jnp.int32

D_MODEL = 1024
CHUNK = 64
SUB = 16
GLA_HEADS = 4
GLA_DK = 64
GLA_DV = 128
GLA_KEY = GLA_HEADS * GLA_DK
GLA_WIDTH = GLA_HEADS * GLA_DV
GATE_RANK = 16
GATE_PAD = 128
POOL_WIDTH = 512
POOL_WINDOWS = (2, 4, 8, 16)
POOL_GC = 128
POOL_HIST = 15
HIST_ROWS = 16
N_MEM = 256
MEM_HEADS = 4
MEM_DH = 256
N_EXPERTS = 256
N_GROUPS = 8
EXPERTS_PER_GROUP = 32
TOPK_GROUPS = 4
TOP_K = 8
ROUTED_SCALE = 2.5
D_EXPERT = 256
TOKEN_TILE = 256
SUBLANES, LANES = 8, 128
assert D_MODEL == SUBLANES * LANES
ALPHA = 2.0 ** 0.25
EPS = 1e-5
OFF_Q, OFF_K, OFF_V, OFF_G, OFF_U, OFF_GD = 0, 256, 512, 1024, 1536, 2048
D_IN_PAD = OFF_GD + GATE_PAD

VMEM_LIMIT = 56 * 1024 * 1024


def _ln(x, g, b):
    mu = jnp.mean(x, axis=-1, keepdims=True)
    xc = x - mu
    var = jnp.mean(xc * xc, axis=-1, keepdims=True)
    return xc * lax.rsqrt(var + EPS) * g + b


def _dot(a, b):
    return jnp.dot(a, b, preferred_element_type=F32)


def _dot_nt(a, b):
    return lax.dot_general(a, b, (((1,), (1,)), ((), ())), preferred_element_type=F32)


def _dot_tn(a, b):
    return lax.dot_general(a, b, (((0,), (0,)), ((), ())), preferred_element_type=F32)


def _split3(x):
    h = x.astype(BF16)
    r = x - h.astype(F32)
    m = r.astype(BF16)
    l = (r - m.astype(F32)).astype(BF16)
    return h, m, l


def _trunk_a_kernel(x_ref, s0_ref, h0_ref, lng_ref, lnb_ref, win_ref, wgu_ref, bg_ref, gng_ref,
                    pw_ref, ps_ref, wout_ref, l1g_ref, l1b_ref,
                    x1_ref, sn_ref, hn_ref,
                    proj_scr, ext_scr, s_scr, op_scr, *, tl, chunk, start_pos):
    t = pl.program_id(1)
    nt = pl.num_programs(1)

    @pl.when(t == 0)
    def _():
        s_scr[...] = s0_ref[0]
        ext_scr[0:1, :] = jnp.zeros((1, POOL_WIDTH), F32)
        ext_scr[1:HIST_ROWS, :] = h0_ref[0]

    xn = _ln(x_ref[0], lng_ref[...], lnb_ref[...])
    proj_scr[...] = _dot(xn.astype(BF16), win_ref[...])

    c = chunk
    shift = lambda a, n: lax.shift_right_logical(a, n.bit_length() - 1)
    gd = proj_scr[:, OFF_GD:OFF_GD + GATE_PAD]
    z = _dot(gd.astype(BF16), wgu_ref[...]) + bg_ref[...]
    lf = (jnp.minimum(z, 0.0) - jnp.log1p(jnp.exp(-jnp.abs(z)))) * (1.0 / 16.0)
    trow = lax.broadcasted_iota(I32, (tl, tl), 0)
    tcol = lax.broadcasted_iota(I32, (tl, tl), 1)
    tri = jnp.logical_and(tcol <= trow, shift(tcol, c) == shift(trow, c)).astype(BF16)
    lh, lm, ll = _split3(lf)
    cum = _dot(tri, lh) + _dot(tri, lm) + _dot(tri, ll)
    q_all = proj_scr[:, OFF_Q:OFF_Q + GLA_KEY] * (GLA_DK ** -0.5)
    k_all = proj_scr[:, OFF_K:OFF_K + GLA_KEY]
    qs_all = q_all * jnp.exp(cum)

    causal = lax.broadcasted_iota(I32, (c, c), 1) <= lax.broadcasted_iota(I32, (c, c), 0)
    eye_dk = lax.broadcasted_iota(I32, (GLA_DK, GLA_DK), 0) == lax.broadcasted_iota(I32, (GLA_DK, GLA_DK), 1)
    n_sub = c // SUB
    rblk = shift(lax.broadcasted_iota(I32, (c, n_sub * GLA_DK), 0), SUB)
    lblk = shift(lax.broadcasted_iota(I32, (c, n_sub * GLA_DK), 1), GLA_DK)
    lblk_row = shift(lax.broadcasted_iota(I32, (1, n_sub * GLA_DK), 1), GLA_DK)
    mask_q = rblk == lblk
    mask_k = rblk <= lblk

    def tile_lanes(a):
        return jnp.concatenate([a] * n_sub, axis=1)

    states = [s_scr[h] for h in range(GLA_HEADS)]
    for ci in range(tl // c):
        rs = slice(ci * c, (ci + 1) * c)
        cum_c = cum[rs]
        last = cum_c[c - 1:c, :]
        ks_c = k_all[rs] * jnp.exp(last - cum_c)
        for h in range(GLA_HEADS):
            ksl = slice(h * GLA_DK, (h + 1) * GLA_DK)
            vsl = slice(h * GLA_DV, (h + 1) * GLA_DV)
            v_h = proj_scr[rs, OFF_V + h * GLA_DV:OFF_V + (h + 1) * GLA_DV].astype(BF16)
            cum_t = tile_lanes(cum_c[:, ksl])
            q_t = tile_lanes(q_all[rs, ksl])
            k_t = tile_lanes(k_all[rs, ksl])
            ref_row = jnp.zeros((1, n_sub * GLA_DK), F32)
            for i in range(1, n_sub):
                ref_row = jnp.where(lblk_row == i, cum_t[i * SUB - 1:i * SUB, :], ref_row)
            arg = cum_t - ref_row
            lhs = jnp.where(mask_q, q_t * jnp.exp(jnp.where(mask_q, arg, 0.0)), 0.0)
            rhs = jnp.where(mask_k, k_t * jnp.exp(jnp.where(mask_k, -arg, 0.0)), 0.0)
            att = jnp.where(causal, _dot_nt(lhs.astype(BF16), rhs.astype(BF16)), 0.0)
            s_h = states[h]
            o_h = _dot(att.astype(BF16), v_h) + _dot(qs_all[rs, ksl].astype(BF16), s_h.astype(BF16))
            dcol = jnp.sum(jnp.where(eye_dk, jnp.broadcast_to(jnp.exp(last[:, ksl]), (GLA_DK, GLA_DK)), 0.0),
                           axis=1, keepdims=True)
            states[h] = dcol * s_h + _dot_tn(ks_c[:, ksl].astype(BF16), v_h)
            o_h = o_h * lax.rsqrt(jnp.mean(o_h * o_h, axis=-1, keepdims=True) + EPS) * gng_ref[...]
            g_h = proj_scr[rs, OFF_G + h * GLA_DV:OFF_G + (h + 1) * GLA_DV]
            op_scr[rs, vsl] = (o_h * (g_h * jax.nn.sigmoid(g_h))).astype(BF16)
    for h in range(GLA_HEADS):
        s_scr[h] = states[h]

    u = proj_scr[:, OFF_U:OFF_U + POOL_WIDTH]
    ext_scr[HIST_ROWS:HIST_ROWS + tl, :] = u
    n_valid = start_pos + t * tl + lax.broadcasted_iota(I32, (tl, 1), 0) + 1
    for gi, w in enumerate(POOL_WINDOWS):
        lsl = slice(gi * POOL_GC, (gi + 1) * POOL_GC)
        win = ext_scr[HIST_ROWS:HIST_ROWS + tl, lsl]
        for s in range(1, w):
            win = win + ext_scr[HIST_ROWS - s:HIST_ROWS - s + tl, lsl]
        cnt = jnp.minimum(w, n_valid).astype(F32)
        r = win / cnt - u[:, lsl]
        p = _dot(r.astype(BF16), pw_ref[gi]) * ps_ref[:, lsl]
        op_scr[:, GLA_WIDTH + gi * POOL_GC:GLA_WIDTH + (gi + 1) * POOL_GC] = p.astype(BF16)
    tail = ext_scr[tl:tl + HIST_ROWS, :]
    ext_scr[0:HIST_ROWS, :] = tail

    mix = _dot(op_scr[...], wout_ref[...])
    x1_ref[0] = _ln(xn * ALPHA + mix, l1g_ref[...], l1b_ref[...])

    @pl.when(t == nt - 1)
    def _():
        sn_ref[0] = s_scr[...]
        hn_ref[0] = ext_scr[1:HIST_ROWS, :]


def _trunk_a(x, s0, h0, wts, *, start_pos):
    b, l, d = x.shape
    tl = min(l, 256)
    chunk = min(tl, CHUNK)
    assert l % tl == 0 and tl % chunk == 0 and chunk % SUB == 0 and l >= HIST_ROWS
    nt = l // tl
    kern = functools.partial(_trunk_a_kernel, tl=tl, chunk=chunk, start_pos=start_pos)
    const = lambda shape: pl.BlockSpec(shape, lambda bi, ti: (0,) * len(shape))
    return pl.pallas_call(
        kern,
        grid=(b, nt),
        in_specs=[
            pl.BlockSpec((1, tl, d), lambda bi, ti: (bi, ti, 0)),
            pl.BlockSpec((1, GLA_HEADS, GLA_DK, GLA_DV), lambda bi, ti: (bi, 0, 0, 0)),
            pl.BlockSpec((1, POOL_HIST, POOL_WIDTH), lambda bi, ti: (bi, 0, 0)),
            const((1, d)), const((1, d)),
            const((d, D_IN_PAD)), const((GATE_PAD, GLA_KEY)), const((1, GLA_KEY)), const((1, GLA_DV)),
            const((len(POOL_WINDOWS), POOL_GC, POOL_GC)), const((1, POOL_WIDTH)),
            const((GLA_WIDTH + POOL_WIDTH, d)), const((1, d)), const((1, d)),
        ],
        out_specs=[
            pl.BlockSpec((1, tl, d), lambda bi, ti: (bi, ti, 0)),
            pl.BlockSpec((1, GLA_HEADS, GLA_DK, GLA_DV), lambda bi, ti: (bi, 0, 0, 0)),
            pl.BlockSpec((1, POOL_HIST, POOL_WIDTH), lambda bi, ti: (bi, 0, 0)),
        ],
        out_shape=[
            jax.ShapeDtypeStruct((b, l, d), F32),
            jax.ShapeDtypeStruct((b, GLA_HEADS, GLA_DK, GLA_DV), F32),
            jax.ShapeDtypeStruct((b, POOL_HIST, POOL_WIDTH), F32),
        ],
        scratch_shapes=[
            pltpu.VMEM((tl, D_IN_PAD), F32),
            pltpu.VMEM((HIST_ROWS + tl, POOL_WIDTH), F32),
            pltpu.VMEM((GLA_HEADS, GLA_DK, GLA_DV), F32),
            pltpu.VMEM((tl, GLA_WIDTH + POOL_WIDTH), BF16),
        ],
        compiler_params=pltpu.CompilerParams(
            dimension_semantics=("arbitrary", "arbitrary"), vmem_limit_bytes=VMEM_LIMIT),
        name="trunk_a",
    )(x, s0, h0, *wts)


def _prep_trunk_a_weights(ln_in_g, ln_in_b, w_in, w_gate_up, b_gate, gla_norm_g, pool_w, pool_scale, w_out,
                          ln1_g, ln1_b):
    d = D_MODEL
    p_gd = 2 * GLA_KEY + 2 * GLA_WIDTH
    w_in_r = jnp.concatenate(
        [w_in[:, :p_gd], w_in[:, p_gd + GATE_RANK:], w_in[:, p_gd:p_gd + GATE_RANK],
         jnp.zeros((d, GATE_PAD - GATE_RANK), w_in.dtype)], axis=1).astype(BF16)
    wgu = jnp.concatenate([w_gate_up, jnp.zeros((GATE_PAD - GATE_RANK, GLA_KEY), w_gate_up.dtype)],
                          axis=0).astype(BF16)
    return (ln_in_g.reshape(1, d), ln_in_b.reshape(1, d), w_in_r, wgu, b_gate.reshape(1, GLA_KEY),
            gla_norm_g.reshape(1, GLA_DV), pool_w.astype(BF16), pool_scale.reshape(1, POOL_WIDTH),
            w_out.astype(BF16), ln1_g.reshape(1, d), ln1_b.reshape(1, d))


def _mem_kv_kernel(m_ref, wk_ref, wv_ref, k_ref, v_ref, kb_ref, vb_ref):
    m = m_ref[...].astype(BF16)
    k = _dot(m, wk_ref[...])
    v = _dot(m, wv_ref[...])
    k_ref[...] = k
    v_ref[...] = v
    kb_ref[...] = k.astype(BF16)
    vb_ref[...] = v.astype(BF16)


def _mem_kv(mem, wk, wv):
    m, d = mem.shape
    tm = min(m, 512)
    assert m % tm == 0
    row = pl.BlockSpec((tm, d), lambda i: (i, 0))
    wspec = pl.BlockSpec((d, d), lambda i: (0, 0))
    return pl.pallas_call(
        _mem_kv_kernel,
        grid=(m // tm,),
        in_specs=[row, wspec, wspec],
        out_specs=[row, row, row, row],
        out_shape=[jax.ShapeDtypeStruct((m, d), F32), jax.ShapeDtypeStruct((m, d), F32),
                   jax.ShapeDtypeStruct((m, d), BF16), jax.ShapeDtypeStruct((m, d), BF16)],
        compiler_params=pltpu.CompilerParams(dimension_semantics=("arbitrary",), vmem_limit_bytes=VMEM_LIMIT),
        name="mem_kv",
    )(mem, wk, wv)


def _attn_kernel(x1_ref, k_ref, v_ref, wq_ref, wo_ref, g_ref, b_ref, x2_ref, o_scr):
    x1 = x1_ref[0]
    q = (_dot(x1.astype(BF16), wq_ref[...]) * (MEM_DH ** -0.5)).astype(BF16)
    for h in range(MEM_HEADS):
        hs = slice(h * MEM_DH, (h + 1) * MEM_DH)
        s = _dot_nt(q[:, hs], k_ref[0, :, hs])
        e = jnp.exp(s - jnp.max(s, axis=-1, keepdims=True))
        p = e / jnp.sum(e, axis=-1, keepdims=True)
        o_scr[:, hs] = _dot(p.astype(BF16), v_ref[0, :, hs]).astype(BF16)
    attn = _dot(o_scr[...], wo_ref[...])
    x2_ref[...] = _ln(x1 * ALPHA + attn, g_ref[...], b_ref[...])


def _attn(x1, mem_k, mem_v, wq, wo, g, b):
    bsz, l, d = x1.shape
    tl = min(l, 256)
    assert l % tl == 0
    nt = l // tl
    const = lambda shape: pl.BlockSpec(shape, lambda bi, ti: (0,) * len(shape))
    return pl.pallas_call(
        _attn_kernel,
        grid=(bsz, nt),
        in_specs=[
            pl.BlockSpec((1, tl, d), lambda bi, ti: (bi, ti, 0)),
            pl.BlockSpec((1, N_MEM, d), lambda bi, ti: (bi, 0, 0)),
            pl.BlockSpec((1, N_MEM, d), lambda bi, ti: (bi, 0, 0)),
            const((d, d)), const((d, d)), const((1, d)), const((1, d)),
        ],
        out_specs=pl.BlockSpec((tl, d), lambda bi, ti: (bi * nt + ti, 0)),
        out_shape=jax.ShapeDtypeStruct((bsz * l, d), F32),
        scratch_shapes=[pltpu.VMEM((tl, d), BF16)],
        compiler_params=pltpu.CompilerParams(
            dimension_semantics=("arbitrary", "arbitrary"), vmem_limit_bytes=VMEM_LIMIT),
        name="attn",
    )(x1, mem_k, mem_v, wq, wo, g, b)


def _two_part_specs(t, d, tiles_a):
    spec_a = pl.BlockSpec((t, d), lambda i, *_: (jnp.minimum(i, tiles_a - 1), 0))
    spec_b = pl.BlockSpec((t, d), lambda i, *_: (jnp.maximum(i - tiles_a, 0), 0))
    return spec_a, spec_b


def _router_kernel(xa_ref, xb_ref, wrt_ref, bias_ref, e_ref, g_ref, r_ref, cnt_ref, cnt_scr, *, tl, tiles_a):
    i = pl.program_id(0)

    @pl.when(i == 0)
    def _():
        cnt_scr[...] = jnp.zeros_like(cnt_scr)

    x = jnp.where(i < tiles_a, xa_ref[...], xb_ref[...])
    xh = x.astype(BF16)
    xm = (x - xh.astype(F32)).astype(BF16)
    wh = wrt_ref[0]
    wm = wrt_ref[1]
    logits = _dot_nt(wh, xh) + (_dot_nt(wh, xm) + _dot_nt(wm, xh))
    scores = jax.nn.sigmoid(logits)
    biased = scores + bias_ref[...]
    ninf = -jnp.inf
    eg = EXPERTS_PER_GROUP
    riota = lax.broadcasted_iota(I32, (eg, tl), 0)
    gs_rows = []
    for g in range(N_GROUPS):
        blk = biased[g * eg:(g + 1) * eg, :]
        m1 = jnp.max(blk, axis=0, keepdims=True)
        i1 = jnp.min(jnp.where(blk == m1, riota, eg), axis=0, keepdims=True)
        m2 = jnp.max(jnp.where(riota == i1, ninf, blk), axis=0, keepdims=True)
        gs_rows.append(m1 + m2)
    gs = jnp.concatenate(gs_rows, axis=0)
    giota = lax.broadcasted_iota(I32, (N_GROUPS, tl), 0)
    sel = jnp.zeros((N_GROUPS, tl), jnp.bool_)
    for _ in range(TOPK_GROUPS):
        m = jnp.max(gs, axis=0, keepdims=True)
        gi = jnp.min(jnp.where(gs == m, giota, N_GROUPS), axis=0, keepdims=True)
        hit = giota == gi
        sel = jnp.logical_or(sel, hit)
        gs = jnp.where(hit, ninf, gs)
    self = jnp.where(sel, 1.0, 0.0)
    masked = jnp.concatenate(
        [jnp.where(self[g:g + 1, :] > 0.5, biased[g * eg:(g + 1) * eg, :], ninf) for g in range(N_GROUPS)], axis=0)
    eiota = lax.broadcasted_iota(I32, (N_EXPERTS, tl), 0)
    idx_rows, sc_rows = [], []
    multi = jnp.zeros((N_EXPERTS, tl), F32)
    for _ in range(TOP_K):
        m = jnp.max(masked, axis=0, keepdims=True)
        idx = jnp.min(jnp.where(masked == m, eiota, N_EXPERTS), axis=0, keepdims=True)
        hit = eiota == idx
        sc_rows.append(jnp.sum(jnp.where(hit, scores, 0.0), axis=0, keepdims=True))
        idx_rows.append(idx)
        multi = jnp.where(hit, 1.0, multi)
        masked = jnp.where(hit, ninf, masked)
    top_e = jnp.concatenate(idx_rows, axis=0)
    sc = jnp.concatenate(sc_rows, axis=0)
    e_ref[...] = top_e
    g_ref[...] = sc / jnp.sum(sc, axis=0, keepdims=True) * ROUTED_SCALE
    mh = multi.astype(BF16)
    before = (lax.broadcasted_iota(I32, (tl, tl), 0) < lax.broadcasted_iota(I32, (tl, tl), 1)).astype(BF16)
    running = cnt_scr[...]
    rankmat = _dot(mh, before) + jnp.concatenate([running] * (tl // 128), axis=1)
    r_rows = [jnp.sum(jnp.where(eiota == idx_rows[k], rankmat, 0.0), axis=0, keepdims=True) for k in range(TOP_K)]
    r_ref[...] = jnp.concatenate(r_rows, axis=0).astype(I32)
    total = running + _dot(mh, jnp.ones((tl, 128), BF16))
    cnt_scr[...] = total
    cnt_ref[...] = total


def _router(x2a, x2b, wrt, bias_col):
    d = x2a.shape[1]
    tl = TOKEN_TILE
    assert x2a.shape[0] % tl == 0 and x2b.shape[0] % tl == 0
    tiles_a = x2a.shape[0] // tl
    n = x2a.shape[0] + x2b.shape[0]
    kspec = pl.BlockSpec((TOP_K, tl), lambda i: (0, i))
    return pl.pallas_call(
        functools.partial(_router_kernel, tl=tl, tiles_a=tiles_a),
        grid=(n // tl,),
        in_specs=[*_two_part_specs(tl, d, tiles_a),
                  pl.BlockSpec((2, N_EXPERTS, d), lambda i: (0, 0, 0)),
                  pl.BlockSpec((N_EXPERTS, 1), lambda i: (0, 0))],
        out_specs=[kspec, kspec, kspec, pl.BlockSpec((N_EXPERTS, 128), lambda i: (0, 0))],
        out_shape=[jax.ShapeDtypeStruct((TOP_K, n), I32), jax.ShapeDtypeStruct((TOP_K, n), F32),
                   jax.ShapeDtypeStruct((TOP_K, n), I32), jax.ShapeDtypeStruct((N_EXPERTS, 128), F32)],
        scratch_shapes=[pltpu.VMEM((N_EXPERTS, 128), F32)],
        compiler_params=pltpu.CompilerParams(dimension_semantics=("arbitrary",), vmem_limit_bytes=VMEM_LIMIT),
        name="router",
    )(x2a, x2b, wrt, bias_col)


def _to_token_tiles(x, dst_ref, t):
    for s in range(SUBLANES):
        dst_ref[pl.ds(s, t, stride=SUBLANES), :] = x[:, s * LANES:(s + 1) * LANES]


def _from_token_tiles(src_ref, t):
    return jnp.concatenate([src_ref[pl.ds(s, t, stride=SUBLANES), :] for s in range(SUBLANES)], axis=1)


def _tile_rows(ref, row0, n_tok=1):
    return ref.at[pl.ds(pl.multiple_of(row0, SUBLANES), n_tok * SUBLANES)]


def _dispatch_kernel(fs_ref, fl_ref, nu_ref, dest_ref, xa_ref, xb_ref, xs_ref, stage, zbuf, sem, zsem,
                     *, t, tm, tiles_a, n_blk):
    i = pl.program_id(0)

    def zero_fill(start):
        def go(cp):
            if start:
                cp.start()
            else:
                cp.wait()

        def per_expert(e, carry):
            ln = fl_ref[e]
            off = fs_ref[e]
            bit = tm // 2
            while bit:
                @pl.when((ln & bit) != 0)
                def _(off=off, bit=bit):
                    go(pltpu.make_async_copy(_tile_rows(zbuf, 0, bit), _tile_rows(xs_ref, off * SUBLANES, bit), zsem))
                off = off + (ln & bit)
                bit //= 2
            return carry

        lax.fori_loop(0, N_EXPERTS, per_expert, 0)

        def per_block(b, carry):
            go(pltpu.make_async_copy(zbuf, _tile_rows(xs_ref, b * (tm * SUBLANES), tm), zsem))
            return carry

        lax.fori_loop(nu_ref[0], n_blk, per_block, 0)

    @pl.when(i == 0)
    def _():
        zbuf[...] = jnp.zeros_like(zbuf)
        zero_fill(True)

    _to_token_tiles(jnp.where(i < tiles_a, xa_ref[...], xb_ref[...]), stage, t)

    def scatter(start):
        def body(n, carry):
            src = _tile_rows(stage, n * SUBLANES)
            for k in range(TOP_K):
                cp = pltpu.make_async_copy(src, _tile_rows(xs_ref, dest_ref[n * TOP_K + k]), sem)
                if start:
                    cp.start(priority=k % 2)
                else:
                    cp.wait()
            return carry

        lax.fori_loop(0, t, body, 0)

    scatter(True)
    scatter(False)

    @pl.when(i == 0)
    def _():
        zero_fill(False)


def _dispatch(fill_start, fill_len, nu, dest, x2a, x2b, *, n_blk, tm):
    d = x2a.shape[1]
    t = TOKEN_TILE
    assert x2a.shape[0] % t == 0 and x2b.shape[0] % t == 0
    tiles_a = x2a.shape[0] // t
    n = x2a.shape[0] + x2b.shape[0]
    return pl.pallas_call(
        functools.partial(_dispatch_kernel, t=t, tm=tm, tiles_a=tiles_a, n_blk=n_blk),
        grid_spec=pltpu.PrefetchScalarGridSpec(
            num_scalar_prefetch=3,
            grid=(n // t,),
            in_specs=[pl.BlockSpec((t * TOP_K,), lambda i, *_: (i,), memory_space=pltpu.SMEM),
                      *_two_part_specs(t, d, tiles_a)],
            out_specs=pl.BlockSpec(memory_space=pl.ANY),
            scratch_shapes=[pltpu.VMEM((t * SUBLANES, LANES), F32), pltpu.VMEM((tm * SUBLANES, LANES), F32),
                            pltpu.SemaphoreType.DMA(()), pltpu.SemaphoreType.DMA(())],
        ),
        out_shape=jax.ShapeDtypeStruct((n_blk * tm * SUBLANES, LANES), F32),
        compiler_params=pltpu.CompilerParams(dimension_semantics=("arbitrary",), vmem_limit_bytes=VMEM_LIMIT),
        name="dispatch",
    )(fill_start, fill_len, nu, dest, x2a, x2b)


def _experts_kernel(b0_ref, nb_ref, cnt_ref, nu_ref, w1_ref, w3_ref, w2_ref, xs_ref, ys_ref,
                    w1b, w3b, w2b, xbuf, ybuf, in_sem, out_sem, *, tm, n_blk):
    e = pl.program_id(0)
    n_used = nu_ref[0]
    blk_rows = tm * SUBLANES

    def x_copy(b):
        return pltpu.make_async_copy(_tile_rows(xs_ref, b * blk_rows, tm), xbuf.at[b % 2], in_sem.at[b % 2])

    def y_copy(b):
        return pltpu.make_async_copy(ybuf.at[b % 2], _tile_rows(ys_ref, b * blk_rows, tm), out_sem.at[b % 2])

    @pl.when(e == 0)
    def _():
        x_copy(0).start()

    w1b[...] = w1_ref[0].astype(BF16)
    w3b[...] = w3_ref[0].astype(BF16)
    w2b[...] = w2_ref[0].astype(BF16)
    b0 = b0_ref[e]
    cnt = cnt_ref[e]

    def block(j, carry):
        b = b0 + j
        x_copy(b).wait()

        @pl.when(b + 1 < n_used)
        def _():
            x_copy(b + 1).start()

        live = lax.broadcasted_iota(I32, (tm, 1), 0) < cnt - j * tm
        x = jnp.where(live, _from_token_tiles(xbuf.at[b % 2], tm), 0.0).astype(BF16)
        a = _dot(x, w1b[...])
        h = (a * jax.nn.sigmoid(a)) * _dot(x, w3b[...])
        y = _dot(h.astype(BF16), w2b[...])

        @pl.when(b >= 2)
        def _():
            y_copy(b - 2).wait()

        _to_token_tiles(y, ybuf.at[b % 2], tm)
        y_copy(b).start()
        return carry

    lax.fori_loop(0, nb_ref[e], block, 0)

    @pl.when(e == pl.num_programs(0) - 1)
    def _():
        @pl.when(n_used >= 2)
        def _():
            y_copy(n_used - 2).wait()

        y_copy(n_used - 1).wait()
        ybuf[0] = jnp.zeros(ybuf.shape[1:], F32)

        def zero(b, carry):
            pltpu.make_async_copy(ybuf.at[0], _tile_rows(ys_ref, b * blk_rows, tm), out_sem.at[0]).start()
            return carry

        lax.fori_loop(n_used, n_blk, zero, 0)

        def zero_wait(b, carry):
            pltpu.make_async_copy(ybuf.at[0], _tile_rows(ys_ref, b * blk_rows, tm), out_sem.at[0]).wait()
            return carry

        lax.fori_loop(n_used, n_blk, zero_wait, 0)


def _experts(blk0, nblk, counts, nu, xs, w1, w3, w2, *, tm):
    blk_rows = tm * SUBLANES
    n_blk = xs.shape[0] // blk_rows
    n_exp, d, de = w1.shape
    wmap = lambda e, *_: (e, 0, 0)
    return pl.pallas_call(
        functools.partial(_experts_kernel, tm=tm, n_blk=n_blk),
        grid_spec=pltpu.PrefetchScalarGridSpec(
            num_scalar_prefetch=4,
            grid=(n_exp,),
            in_specs=[pl.BlockSpec((1, d, de), wmap), pl.BlockSpec((1, d, de), wmap), pl.BlockSpec((1, de, d), wmap),
                      pl.BlockSpec(memory_space=pl.ANY)],
            out_specs=pl.BlockSpec(memory_space=pl.ANY),
            scratch_shapes=[pltpu.VMEM((d, de), BF16), pltpu.VMEM((d, de), BF16), pltpu.VMEM((de, d), BF16),
                            pltpu.VMEM((2, blk_rows, LANES), F32), pltpu.VMEM((2, blk_rows, LANES), F32),
                            pltpu.SemaphoreType.DMA((2,)), pltpu.SemaphoreType.DMA((2,))],
        ),
        out_shape=jax.ShapeDtypeStruct(xs.shape, F32),
        compiler_params=pltpu.CompilerParams(dimension_semantics=("arbitrary",), vmem_limit_bytes=VMEM_LIMIT),
        name="experts",
    )(blk0, nblk, counts, nu, w1, w3, w2, xs)


def _combine_kernel(dcur_ref, dnxt_ref, x2_ref, g_ref, w1_ref, w3_ref, w2_ref, lg_ref, lb_ref, ys_ref,
                    out_ref, buf, sem, *, t):
    i = pl.program_id(0)
    nsteps = pl.num_programs(0)
    slot = i % 2

    def gather(d_ref, s, start):
        def body(n, carry):
            for k in range(TOP_K):
                cp = pltpu.make_async_copy(_tile_rows(ys_ref, d_ref[n * TOP_K + k]),
                                           _tile_rows(buf.at[s, k], n * SUBLANES), sem.at[s])
                if start:
                    cp.start(priority=k % 2)
                else:
                    cp.wait()
            return carry

        lax.fori_loop(0, t, body, 0)

    @pl.when(i == 0)
    def _():
        gather(dcur_ref, 0, True)

    @pl.when(i + 1 < nsteps)
    def _():
        gather(dnxt_ref, 1 - slot, True)

    gather(dcur_ref, slot, False)

    x2 = x2_ref[...]
    g = g_ref[...]
    acc = g[:, 0:1] * _from_token_tiles(buf.at[slot, 0], t)
    for k in range(1, TOP_K):
        acc = acc + g[:, k:k + 1] * _from_token_tiles(buf.at[slot, k], t)
    xb = x2.astype(BF16)
    a = _dot(xb, w1_ref[...])
    hs = (a * jax.nn.sigmoid(a)) * _dot(xb, w3_ref[...])
    shared = _dot(hs.astype(BF16), w2_ref[...])
    out_ref[...] = _ln(x2 * ALPHA + (acc + shared), lg_ref[...], lb_ref[...])


def _combine(dest, x2, gate_t, w1s, w3s, w2s, lg, lb, ys, *, row_off):
    n_rows, d = x2.shape
    n = dest.shape[0] // TOP_K
    t = TOKEN_TILE
    assert row_off % t == 0 and n_rows % t == 0 and n % t == 0
    off = row_off // t
    last = n // t - 1
    de = w1s.shape[-1]
    const = lambda shape: pl.BlockSpec(shape, lambda i: (0,) * len(shape))
    return pl.pallas_call(
        functools.partial(_combine_kernel, t=t),
        grid=(n_rows // t,),
        in_specs=[pl.BlockSpec((t * TOP_K,), lambda i: (off + i,), memory_space=pltpu.SMEM),
                  pl.BlockSpec((t * TOP_K,), lambda i: (jnp.minimum(off + i + 1, last),), memory_space=pltpu.SMEM),
                  pl.BlockSpec((t, d), lambda i: (i, 0)),
                  pl.BlockSpec((t, TOP_K), lambda i: (off + i, 0)),
                  const((d, de)), const((d, de)), const((de, d)), const((1, d)), const((1, d)),
                  pl.BlockSpec(memory_space=pl.ANY)],
        out_specs=pl.BlockSpec((t, d), lambda i: (i, 0)),
        out_shape=jax.ShapeDtypeStruct((n_rows, d), F32),
        scratch_shapes=[pltpu.VMEM((2, TOP_K, t * SUBLANES, LANES), F32), pltpu.SemaphoreType.DMA((2,))],
        compiler_params=pltpu.CompilerParams(dimension_semantics=("arbitrary",), vmem_limit_bytes=VMEM_LIMIT),
        name="combine",
    )(dest, dest, x2, gate_t, w1s, w3s, w2s, lg, lb, ys)


EXPERT_TM = 256


def _moe_plan(cnt):
    tm = EXPERT_TM
    counts = cnt[:, 0].astype(I32)
    padded = (counts + tm - 1) // tm * tm
    pad_end = jnp.cumsum(padded)
    pad_start = pad_end - padded
    n_used = pad_end[-1] // tm
    fill_start = (pad_start + counts).astype(I32)
    fill_len = (padded - counts).astype(I32)
    return (pad_start.astype(I32), (pad_start // tm).astype(I32), (padded // tm).astype(I32), counts,
            n_used.reshape(1).astype(I32), fill_start, fill_len)


def _slots_kernel(ps_ref, e_ref, r_ref, d_ref):
    e = e_ref[...]

    def body(j, base):
        return jnp.where(e == j, ps_ref[j], base)

    base = lax.fori_loop(0, N_EXPERTS, body, jnp.zeros_like(e), unroll=8)
    d_ref[...] = (base + r_ref[...]) * SUBLANES


def _slots(pad_start, top_e, rank):
    k, n = top_e.shape
    tl = next(c for c in (2048, 1536, 1024, 512, 256, 128) if n % c == 0)
    spec = pl.BlockSpec((k, tl), lambda i, ps: (0, i))
    return pl.pallas_call(
        _slots_kernel,
        grid_spec=pltpu.PrefetchScalarGridSpec(num_scalar_prefetch=1, grid=(n // tl,), in_specs=[spec, spec],
                                               out_specs=spec),
        out_shape=jax.ShapeDtypeStruct((k, n), I32),
        compiler_params=pltpu.CompilerParams(dimension_semantics=("arbitrary",)),
        name="slots",
    )(pad_start, top_e, rank)


PAST_LEN = 1024


def kernel(x_prompt, x_sample, state_gla, cache_pool, cache_mem_k, cache_mem_v, mem_prompt, ln_in_g, ln_in_b, w_in, w_gate_up, b_gate, gla_norm_g, pool_w, pool_scale, w_out, ln1_g, ln1_b, wq_mem, wk_mem, wv_mem, wo_mem, ln2_g, ln2_b, w_router, router_bias, w1_exp, w3_exp, w2_exp, w1_sh, w3_sh, w2_sh, ln3_g, ln3_b):
    assert w_in.shape[0] == 1, "single-layer trunk"
    bp, lp, d = x_prompt.shape
    bs, ls, _ = x_sample.shape
    n_p, n_s = bp * lp, bs * ls
    n_all = n_p + n_s

    wts_a = _prep_trunk_a_weights(ln_in_g, ln_in_b, w_in[0], w_gate_up[0], b_gate[0], gla_norm_g[0], pool_w[0],
                                  pool_scale[0], w_out[0], ln1_g[0], ln1_b[0])
    s0 = jnp.zeros((bp, GLA_HEADS, GLA_DK, GLA_DV), F32)
    h0 = jnp.zeros((bp, POOL_HIST, POOL_WIDTH), F32)
    x1p, sp, hp = _trunk_a(x_prompt, s0, h0, wts_a, start_pos=0)
    x1s, ss, hs = _trunk_a(x_sample, state_gla[0], cache_pool[0], wts_a, start_pos=PAST_LEN)

    mk, mv, mkb, mvb = _mem_kv(mem_prompt.reshape(bp * N_MEM, d), wk_mem[0].astype(BF16), wv_mem[0].astype(BF16))
    wq, wo = wq_mem[0].astype(BF16), wo_mem[0].astype(BF16)
    g2, b2 = ln2_g[0].reshape(1, d), ln2_b[0].reshape(1, d)
    x2p = _attn(x1p, mkb.reshape(bp, N_MEM, d), mvb.reshape(bp, N_MEM, d), wq, wo, g2, b2)
    x2s = _attn(x1s, cache_mem_k[0].reshape(bs, N_MEM, d).astype(BF16),
                cache_mem_v[0].reshape(bs, N_MEM, d).astype(BF16), wq, wo, g2, b2)

    wrt = w_router[0].T
    wrt_h = wrt.astype(BF16)
    wrt_m = (wrt - wrt_h.astype(F32)).astype(BF16)
    top_e, gate, rank, cnt = _router(x2p, x2s, jnp.stack([wrt_h, wrt_m]), router_bias[0].reshape(N_EXPERTS, 1))

    assert (n_all * TOP_K) % EXPERT_TM == 0
    n_blk = n_all * TOP_K // EXPERT_TM + N_EXPERTS
    pad_start, blk0, nblk, counts, nu, fill_start, fill_len = _moe_plan(cnt)
    dest = _slots(pad_start, top_e, rank).T.reshape(-1)
    xs = _dispatch(fill_start, fill_len, nu, dest, x2p, x2s, n_blk=n_blk, tm=EXPERT_TM)
    ys = _experts(blk0, nblk, counts, nu, xs, w1_exp[0], w3_exp[0], w2_exp[0], tm=EXPERT_TM)
    sh = (w1_sh[0].astype(BF16), w3_sh[0].astype(BF16), w2_sh[0].astype(BF16),
          ln3_g[0].reshape(1, d), ln3_b[0].reshape(1, d))
    gate_t = gate.T
    yp = _combine(dest, x2p, gate_t, *sh, ys, row_off=0)
    ysm = _combine(dest, x2s, gate_t, *sh, ys, row_off=n_p)

    return (yp.reshape(bp, lp, d), ysm.reshape(bs, ls, d), sp[None], hp[None],
            mk.reshape(1, bp, N_MEM, MEM_HEADS, MEM_DH), mv.reshape(1, bp, N_MEM, MEM_HEADS, MEM_DH),
            ss[None], hs[None])
```

```python
import functools

import jax
import jax.numpy as jnp
from jax import lax
from jax.experimental import pallas as pl
from jax.experimental.pallas import tpu as pltpu

F32 = jnp.float32
BF16 = jnp.bfloat16
I32 = jnp.int32

D_MODEL = 1024
CHUNK = 64
SUB = 16
GLA_HEADS = 4
GLA_DK = 64
GLA_DV = 128
GLA_KEY = GLA_HEADS * GLA_DK
GLA_WIDTH = GLA_HEADS * GLA_DV
GATE_RANK = 16
GATE_PAD = 128
POOL_WIDTH = 512
POOL_WINDOWS = (2, 4, 8, 16)
POOL_GC = 128
POOL_HIST = 15
HIST_ROWS = 16
N_MEM = 256
MEM_HEADS = 4
MEM_DH = 256
N_EXPERTS = 256
N_GROUPS = 8
EXPERTS_PER_GROUP = 32
TOPK_GROUPS = 4
TOP_K = 8
ROUTED_SCALE = 2.5
D_EXPERT = 256
TOKEN_TILE = 256
SUBLANES, LANES = 8, 128
assert D_MODEL == SUBLANES * LANES
ALPHA = 2.0 ** 0.25
EPS = 1e-5
OFF_Q, OFF_K, OFF_V, OFF_G, OFF_U, OFF_GD = 0, 256, 512, 1024, 1536, 2048
D_IN_PAD = OFF_GD + GATE_PAD

VMEM_LIMIT = 56 * 1024 * 1024


def _ln(x, g, b):
    mu = jnp.mean(x, axis=-1, keepdims=True)
    xc = x - mu
    var = jnp.mean(xc * xc, axis=-1, keepdims=True)
    return xc * lax.rsqrt(var + EPS) * g + b


def _dot(a, b):
    return jnp.dot(a, b, preferred_element_type=F32)


def _dot_nt(a, b):
    return lax.dot_general(a, b, (((1,), (1,)), ((), ())), preferred_element_type=F32)


def _dot_tn(a, b):
    return lax.dot_general(a, b, (((0,), (0,)), ((), ())), preferred_element_type=F32)


def _split3(x):
    h = x.astype(BF16)
    r = x - h.astype(F32)
    m = r.astype(BF16)
    l = (r - m.astype(F32)).astype(BF16)
    return h, m, l


def _trunk_a_kernel(x_ref, s0_ref, h0_ref, lng_ref, lnb_ref, win_ref, wgu_ref, bg_ref, gng_ref,
                    pw_ref, ps_ref, wout_ref, l1g_ref, l1b_ref,
                    x1_ref, sn_ref, hn_ref,
                    proj_scr, ext_scr, s_scr, op_scr, *, tl, chunk, start_pos):
    t = pl.program_id(1)
    nt = pl.num_programs(1)

    @pl.when(t == 0)
    def _():
        s_scr[...] = s0_ref[0]
        ext_scr[0:1, :] = jnp.zeros((1, POOL_WIDTH), F32)
        ext_scr[1:HIST_ROWS, :] = h0_ref[0]

    xn = _ln(x_ref[0], lng_ref[...], lnb_ref[...])
    proj_scr[...] = _dot(xn.astype(BF16), win_ref[...])

    c = chunk
    shift = lambda a, n: lax.shift_right_logical(a, n.bit_length() - 1)
    gd = proj_scr[:, OFF_GD:OFF_GD + GATE_PAD]
    z = _dot(gd.astype(BF16), wgu_ref[...]) + bg_ref[...]
    lf = (jnp.minimum(z, 0.0) - jnp.log1p(jnp.exp(-jnp.abs(z)))) * (1.0 / 16.0)
    trow = lax.broadcasted_iota(I32, (tl, tl), 0)
    tcol = lax.broadcasted_iota(I32, (tl, tl), 1)
    tri = jnp.logical_and(tcol <= trow, shift(tcol, c) == shift(trow, c)).astype(BF16)
    lh, lm, ll = _split3(lf)
    cum = _dot(tri, lh) + _dot(tri, lm) + _dot(tri, ll)
    q_all = proj_scr[:, OFF_Q:OFF_Q + GLA_KEY] * (GLA_DK ** -0.5)
    k_all = proj_scr[:, OFF_K:OFF_K + GLA_KEY]
    qs_all = q_all * jnp.exp(cum)

    causal = lax.broadcasted_iota(I32, (c, c), 1) <= lax.broadcasted_iota(I32, (c, c), 0)
    eye_dk = lax.broadcasted_iota(I32, (GLA_DK, GLA_DK), 0) == lax.broadcasted_iota(I32, (GLA_DK, GLA_DK), 1)
    n_sub = c // SUB
    rblk = shift(lax.broadcasted_iota(I32, (c, n_sub * GLA_DK), 0), SUB)
    lblk = shift(lax.broadcasted_iota(I32, (c, n_sub * GLA_DK), 1), GLA_DK)
    lblk_row = shift(lax.broadcasted_iota(I32, (1, n_sub * GLA_DK), 1), GLA_DK)
    mask_q = rblk == lblk
    mask_k = rblk <= lblk

    def tile_lanes(a):
        return jnp.concatenate([a] * n_sub, axis=1)

    states = [s_scr[h] for h in range(GLA_HEADS)]
    for ci in range(tl // c):
        rs = slice(ci * c, (ci + 1) * c)
        cum_c = cum[rs]
        last = cum_c[c - 1:c, :]
        ks_c = k_all[rs] * jnp.exp(last - cum_c)
        for h in range(GLA_HEADS):
            ksl = slice(h * GLA_DK, (h + 1) * GLA_DK)
            vsl = slice(h * GLA_DV, (h + 1) * GLA_DV)
            v_h = proj_scr[rs, OFF_V + h * GLA_DV:OFF_V + (h + 1) * GLA_DV].astype(BF16)
            cum_t = tile_lanes(cum_c[:, ksl])
            q_t = tile_lanes(q_all[rs, ksl])
            k_t = tile_lanes(k_all[rs, ksl])
            ref_row = jnp.zeros((1, n_sub * GLA_DK), F32)
            for i in range(1, n_sub):
                ref_row = jnp.where(lblk_row == i, cum_t[i * SUB - 1:i * SUB, :], ref_row)
            arg = cum_t - ref_row
            lhs = jnp.where(mask_q, q_t * jnp.exp(jnp.where(mask_q, arg, 0.0)), 0.0)
            rhs = jnp.where(mask_k, k_t * jnp.exp(jnp.where(mask_k, -arg, 0.0)), 0.0)
            att = jnp.where(causal, _dot_nt(lhs.astype(BF16), rhs.astype(BF16)), 0.0)
            s_h = states[h]
            o_h = _dot(att.astype(BF16), v_h) + _dot(qs_all[rs, ksl].astype(BF16), s_h.astype(BF16))
            dcol = jnp.sum(jnp.where(eye_dk, jnp.broadcast_to(jnp.exp(last[:, ksl]), (GLA_DK, GLA_DK)), 0.0),
                           axis=1, keepdims=True)
            states[h] = dcol * s_h + _dot_tn(ks_c[:, ksl].astype(BF16), v_h)
            o_h = o_h * lax.rsqrt(jnp.mean(o_h * o_h, axis=-1, keepdims=True) + EPS) * gng_ref[...]
            g_h = proj_scr[rs, OFF_G + h * GLA_DV:OFF_G + (h + 1) * GLA_DV]
            op_scr[rs, vsl] = (o_h * (g_h * jax.nn.sigmoid(g_h))).astype(BF16)
    for h in range(GLA_HEADS):
        s_scr[h] = states[h]

    u = proj_scr[:, OFF_U:OFF_U + POOL_WIDTH]
    ext_scr[HIST_ROWS:HIST_ROWS + tl, :] = u
    n_valid = start_pos + t * tl + lax.broadcasted_iota(I32, (tl, 1), 0) + 1
    for gi, w in enumerate(POOL_WINDOWS):
        lsl = slice(gi * POOL_GC, (gi + 1) * POOL_GC)
        win = ext_scr[HIST_ROWS:HIST_ROWS + tl, lsl]
        for s in range(1, w):
            win = win + ext_scr[HIST_ROWS - s:HIST_ROWS - s + tl, lsl]
        cnt = jnp.minimum(w, n_valid).astype(F32)
        r = win / cnt - u[:, lsl]
        p = _dot(r.astype(BF16), pw_ref[gi]) * ps_ref[:, lsl]
        op_scr[:, GLA_WIDTH + gi * POOL_GC:GLA_WIDTH + (gi + 1) * POOL_GC] = p.astype(BF16)
    tail = ext_scr[tl:tl + HIST_ROWS, :]
    ext_scr[0:HIST_ROWS, :] = tail

    mix = _dot(op_scr[...], wout_ref[...])
    x1_ref[0] = _ln(xn * ALPHA + mix, l1g_ref[...], l1b_ref[...])

    @pl.when(t == nt - 1)
    def _():
        sn_ref[0] = s_scr[...]
        hn_ref[0] = ext_scr[1:HIST_ROWS, :]


def _trunk_a(x, s0, h0, wts, *, start_pos):
    b, l, d = x.shape
    tl = min(l, 256)
    chunk = min(tl, CHUNK)
    assert l % tl == 0 and tl % chunk == 0 and chunk % SUB == 0 and l >= HIST_ROWS
    nt = l // tl
    kern = functools.partial(_trunk_a_kernel, tl=tl, chunk=chunk, start_pos=start_pos)
    const = lambda shape: pl.BlockSpec(shape, lambda bi, ti: (0,) * len(shape))
    return pl.pallas_call(
        kern,
        grid=(b, nt),
        in_specs=[
            pl.BlockSpec((1, tl, d), lambda bi, ti: (bi, ti, 0)),
            pl.BlockSpec((1, GLA_HEADS, GLA_DK, GLA_DV), lambda bi, ti: (bi, 0, 0, 0)),
            pl.BlockSpec((1, POOL_HIST, POOL_WIDTH), lambda bi, ti: (bi, 0, 0)),
            const((1, d)), const((1, d)),
            const((d, D_IN_PAD)), const((GATE_PAD, GLA_KEY)), const((1, GLA_KEY)), const((1, GLA_DV)),
            const((len(POOL_WINDOWS), POOL_GC, POOL_GC)), const((1, POOL_WIDTH)),
            const((GLA_WIDTH + POOL_WIDTH, d)), const((1, d)), const((1, d)),
        ],
        out_specs=[
            pl.BlockSpec((1, tl, d), lambda bi, ti: (bi, ti, 0)),
            pl.BlockSpec((1, GLA_HEADS, GLA_DK, GLA_DV), lambda bi, ti: (bi, 0, 0, 0)),
            pl.BlockSpec((1, POOL_HIST, POOL_WIDTH), lambda bi, ti: (bi, 0, 0)),
        ],
        out_shape=[
            jax.ShapeDtypeStruct((b, l, d), F32),
            jax.ShapeDtypeStruct((b, GLA_HEADS, GLA_DK, GLA_DV), F32),
            jax.ShapeDtypeStruct((b, POOL_HIST, POOL_WIDTH), F32),
        ],
        scratch_shapes=[
            pltpu.VMEM((tl, D_IN_PAD), F32),
            pltpu.VMEM((HIST_ROWS + tl, POOL_WIDTH), F32),
            pltpu.VMEM((GLA_HEADS, GLA_DK, GLA_DV), F32),
            pltpu.VMEM((tl, GLA_WIDTH + POOL_WIDTH), BF16),
        ],
        compiler_params=pltpu.CompilerParams(
            dimension_semantics=("arbitrary", "arbitrary"), vmem_limit_bytes=VMEM_LIMIT),
        name="trunk_a",
    )(x, s0, h0, *wts)


def _prep_trunk_a_weights(ln_in_g, ln_in_b, w_in, w_gate_up, b_gate, gla_norm_g, pool_w, pool_scale, w_out,
                          ln1_g, ln1_b):
    d = D_MODEL
    p_gd = 2 * GLA_KEY + 2 * GLA_WIDTH
    w_in_r = jnp.concatenate(
        [w_in[:, :p_gd], w_in[:, p_gd + GATE_RANK:], w_in[:, p_gd:p_gd + GATE_RANK],
         jnp.zeros((d, GATE_PAD - GATE_RANK), w_in.dtype)], axis=1).astype(BF16)
    wgu = jnp.concatenate([w_gate_up, jnp.zeros((GATE_PAD - GATE_RANK, GLA_KEY), w_gate_up.dtype)],
                          axis=0).astype(BF16)
    return (ln_in_g.reshape(1, d), ln_in_b.reshape(1, d), w_in_r, wgu, b_gate.reshape(1, GLA_KEY),
            gla_norm_g.reshape(1, GLA_DV), pool_w.astype(BF16), pool_scale.reshape(1, POOL_WIDTH),
            w_out.astype(BF16), ln1_g.reshape(1, d), ln1_b.reshape(1, d))


def _mem_kv_kernel(m_ref, wk_ref, wv_ref, k_ref, v_ref, kb_ref, vb_ref):
    m = m_ref[...].astype(BF16)
    k = _dot(m, wk_ref[...])
    v = _dot(m, wv_ref[...])
    k_ref[...] = k
    v_ref[...] = v
    kb_ref[...] = k.astype(BF16)
    vb_ref[...] = v.astype(BF16)


def _mem_kv(mem, wk, wv):
    m, d = mem.shape
    tm = min(m, 512)
    assert m % tm == 0
    row = pl.BlockSpec((tm, d), lambda i: (i, 0))
    wspec = pl.BlockSpec((d, d), lambda i: (0, 0))
    return pl.pallas_call(
        _mem_kv_kernel,
        grid=(m // tm,),
        in_specs=[row, wspec, wspec],
        out_specs=[row, row, row, row],
        out_shape=[jax.ShapeDtypeStruct((m, d), F32), jax.ShapeDtypeStruct((m, d), F32),
                   jax.ShapeDtypeStruct((m, d), BF16), jax.ShapeDtypeStruct((m, d), BF16)],
        compiler_params=pltpu.CompilerParams(dimension_semantics=("arbitrary",), vmem_limit_bytes=VMEM_LIMIT),
        name="mem_kv",
    )(mem, wk, wv)


def _attn_kernel(x1_ref, k_ref, v_ref, wq_ref, wo_ref, g_ref, b_ref, x2_ref, o_scr):
    x1 = x1_ref[0]
    q = (_dot(x1.astype(BF16), wq_ref[...]) * (MEM_DH ** -0.5)).astype(BF16)
    for h in range(MEM_HEADS):
        hs = slice(h * MEM_DH, (h + 1) * MEM_DH)
        s = _dot_nt(q[:, hs], k_ref[0, :, hs])
        e = jnp.exp(s - jnp.max(s, axis=-1, keepdims=True))
        p = e / jnp.sum(e, axis=-1, keepdims=True)
        o_scr[:, hs] = _dot(p.astype(BF16), v_ref[0, :, hs]).astype(BF16)
    attn = _dot(o_scr[...], wo_ref[...])
    x2_ref[...] = _ln(x1 * ALPHA + attn, g_ref[...], b_ref[...])


def _attn(x1, mem_k, mem_v, wq, wo, g, b):
    bsz, l, d = x1.shape
    tl = min(l, 256)
    assert l % tl == 0
    nt = l // tl
    const = lambda shape: pl.BlockSpec(shape, lambda bi, ti: (0,) * len(shape))
    return pl.pallas_call(
        _attn_kernel,
        grid=(bsz, nt),
        in_specs=[
            pl.BlockSpec((1, tl, d), lambda bi, ti: (bi, ti, 0)),
            pl.BlockSpec((1, N_MEM, d), lambda bi, ti: (bi, 0, 0)),
            pl.BlockSpec((1, N_MEM, d), lambda bi, ti: (bi, 0, 0)),
            const((d, d)), const((d, d)), const((1, d)), const((1, d)),
        ],
        out_specs=pl.BlockSpec((tl, d), lambda bi, ti: (bi * nt + ti, 0)),
        out_shape=jax.ShapeDtypeStruct((bsz * l, d), F32),
        scratch_shapes=[pltpu.VMEM((tl, d), BF16)],
        compiler_params=pltpu.CompilerParams(
            dimension_semantics=("arbitrary", "arbitrary"), vmem_limit_bytes=VMEM_LIMIT),
        name="attn",
    )(x1, mem_k, mem_v, wq, wo, g, b)


def _two_part_specs(t, d, tiles_a):
    spec_a = pl.BlockSpec((t, d), lambda i, *_: (jnp.minimum(i, tiles_a - 1), 0))
    spec_b = pl.BlockSpec((t, d), lambda i, *_: (jnp.maximum(i - tiles_a, 0), 0))
    return spec_a, spec_b


def _router_kernel(xa_ref, xb_ref, wrt_ref, bias_ref, e_ref, g_ref, r_ref, cnt_ref, cnt_scr, *, tl, tiles_a):
    i = pl.program_id(0)

    @pl.when(i == 0)
    def _():
        cnt_scr[...] = jnp.zeros_like(cnt_scr)

    x = jnp.where(i < tiles_a, xa_ref[...], xb_ref[...])
    xh = x.astype(BF16)
    xm = (x - xh.astype(F32)).astype(BF16)
    wh = wrt_ref[0]
    wm = wrt_ref[1]
    logits = _dot_nt(wh, xh) + (_dot_nt(wh, xm) + _dot_nt(wm, xh))
    scores = jax.nn.sigmoid(logits)
    biased = scores + bias_ref[...]
    ninf = -jnp.inf
    eg = EXPERTS_PER_GROUP
    riota = lax.broadcasted_iota(I32, (eg, tl), 0)
    gs_rows = []
    for g in range(N_GROUPS):
        blk = biased[g * eg:(g + 1) * eg, :]
        m1 = jnp.max(blk, axis=0, keepdims=True)
        i1 = jnp.min(jnp.where(blk == m1, riota, eg), axis=0, keepdims=True)
        m2 = jnp.max(jnp.where(riota == i1, ninf, blk), axis=0, keepdims=True)
        gs_rows.append(m1 + m2)
    gs = jnp.concatenate(gs_rows, axis=0)
    giota = lax.broadcasted_iota(I32, (N_GROUPS, tl), 0)
    sel = jnp.zeros((N_GROUPS, tl), jnp.bool_)
    for _ in range(TOPK_GROUPS):
        m = jnp.max(gs, axis=0, keepdims=True)
        gi = jnp.min(jnp.where(gs == m, giota, N_GROUPS), axis=0, keepdims=True)
        hit = giota == gi
        sel = jnp.logical_or(sel, hit)
        gs = jnp.where(hit, ninf, gs)
    self = jnp.where(sel, 1.0, 0.0)
    masked = jnp.concatenate(
        [jnp.where(self[g:g + 1, :] > 0.5, biased[g * eg:(g + 1) * eg, :], ninf) for g in range(N_GROUPS)], axis=0)
    eiota = lax.broadcasted_iota(I32, (N_EXPERTS, tl), 0)
    idx_rows, sc_rows = [], []
    multi = jnp.zeros((N_EXPERTS, tl), F32)
    for _ in range(TOP_K):
        m = jnp.max(masked, axis=0, keepdims=True)
        idx = jnp.min(jnp.where(masked == m, eiota, N_EXPERTS), axis=0, keepdims=True)
        hit = eiota == idx
        sc_rows.append(jnp.sum(jnp.where(hit, scores, 0.0), axis=0, keepdims=True))
        idx_rows.append(idx)
        multi = jnp.where(hit, 1.0, multi)
        masked = jnp.where(hit, ninf, masked)
    top_e = jnp.concatenate(idx_rows, axis=0)
    sc = jnp.concatenate(sc_rows, axis=0)
    e_ref[...] = top_e
    g_ref[...] = sc / jnp.sum(sc, axis=0, keepdims=True) * ROUTED_SCALE
    mh = multi.astype(BF16)
    before = (lax.broadcasted_iota(I32, (tl, tl), 0) < lax.broadcasted_iota(I32, (tl, tl), 1)).astype(BF16)
    running = cnt_scr[...]
    rankmat = _dot(mh, before) + jnp.concatenate([running] * (tl // 128), axis=1)
    r_rows = [jnp.sum(jnp.where(eiota == idx_rows[k], rankmat, 0.0), axis=0, keepdims=True) for k in range(TOP_K)]
    r_ref[...] = jnp.concatenate(r_rows, axis=0).astype(I32)
    total = running + _dot(mh, jnp.ones((tl, 128), BF16))
    cnt_scr[...] = total
    cnt_ref[...] = total


def _router(x2a, x2b, wrt, bias_col):
    d = x2a.shape[1]
    tl = TOKEN_TILE
    assert x2a.shape[0] % tl == 0 and x2b.shape[0] % tl == 0
    tiles_a = x2a.shape[0] // tl
    n = x2a.shape[0] + x2b.shape[0]
    kspec = pl.BlockSpec((TOP_K, tl), lambda i: (0, i))
    return pl.pallas_call(
        functools.partial(_router_kernel, tl=tl, tiles_a=tiles_a),
        grid=(n // tl,),
        in_specs=[*_two_part_specs(tl, d, tiles_a),
                  pl.BlockSpec((2, N_EXPERTS, d), lambda i: (0, 0, 0)),
                  pl.BlockSpec((N_EXPERTS, 1), lambda i: (0, 0))],
        out_specs=[kspec, kspec, kspec, pl.BlockSpec((N_EXPERTS, 128), lambda i: (0, 0))],
        out_shape=[jax.ShapeDtypeStruct((TOP_K, n), I32), jax.ShapeDtypeStruct((TOP_K, n), F32),
                   jax.ShapeDtypeStruct((TOP_K, n), I32), jax.ShapeDtypeStruct((N_EXPERTS, 128), F32)],
        scratch_shapes=[pltpu.VMEM((N_EXPERTS, 128), F32)],
        compiler_params=pltpu.CompilerParams(dimension_semantics=("arbitrary",), vmem_limit_bytes=VMEM_LIMIT),
        name="router",
    )(x2a, x2b, wrt, bias_col)


def _to_token_tiles(x, dst_ref, t):
    for s in range(SUBLANES):
        dst_ref[pl.ds(s, t, stride=SUBLANES), :] = x[:, s * LANES:(s + 1) * LANES]


def _from_token_tiles(src_ref, t):
    return jnp.concatenate([src_ref[pl.ds(s, t, stride=SUBLANES), :] for s in range(SUBLANES)], axis=1)


def _tile_rows(ref, row0, n_tok=1):
    return ref.at[pl.ds(pl.multiple_of(row0, SUBLANES), n_tok * SUBLANES)]


def _dispatch_kernel(fs_ref, fl_ref, nu_ref, dest_ref, xa_ref, xb_ref, xs_ref, stage, zbuf, sem, zsem,
                     *, t, tm, tiles_a, n_blk):
    i = pl.program_id(0)

    def zero_fill(start):
        def go(cp):
            if start:
                cp.start()
            else:
                cp.wait()

        def per_expert(e, carry):
            ln = fl_ref[e]
            off = fs_ref[e]
            bit = tm // 2
            while bit:
                @pl.when((ln & bit) != 0)
                def _(off=off, bit=bit):
                    go(pltpu.make_async_copy(_tile_rows(zbuf, 0, bit), _tile_rows(xs_ref, off * SUBLANES, bit), zsem))
                off = off + (ln & bit)
                bit //= 2
            return carry

        lax.fori_loop(0, N_EXPERTS, per_expert, 0)

        def per_block(b, carry):
            go(pltpu.make_async_copy(zbuf, _tile_rows(xs_ref, b * (tm * SUBLANES), tm), zsem))
            return carry

        lax.fori_loop(nu_ref[0], n_blk, per_block, 0)

    @pl.when(i == 0)
    def _():
        zbuf[...] = jnp.zeros_like(zbuf)
        zero_fill(True)

    _to_token_tiles(jnp.where(i < tiles_a, xa_ref[...], xb_ref[...]), stage, t)

    def scatter(start):
        def body(n, carry):
            src = _tile_rows(stage, n * SUBLANES)
            for k in range(TOP_K):
                cp = pltpu.make_async_copy(src, _tile_rows(xs_ref, dest_ref[n * TOP_K + k]), sem)
                if start:
                    cp.start(priority=k % 2)
                else:
                    cp.wait()
            return carry

        lax.fori_loop(0, t, body, 0)

    scatter(True)
    scatter(False)

    @pl.when(i == 0)
    def _():
        zero_fill(False)


def _dispatch(fill_start, fill_len, nu, dest, x2a, x2b, *, n_blk, tm):
    d = x2a.shape[1]
    t = TOKEN_TILE
    assert x2a.shape[0] % t == 0 and x2b.shape[0] % t == 0
    tiles_a = x2a.shape[0] // t
    n = x2a.shape[0] + x2b.shape[0]
    return pl.pallas_call(
        functools.partial(_dispatch_kernel, t=t, tm=tm, tiles_a=tiles_a, n_blk=n_blk),
        grid_spec=pltpu.PrefetchScalarGridSpec(
            num_scalar_prefetch=3,
            grid=(n // t,),
            in_specs=[pl.BlockSpec((t * TOP_K,), lambda i, *_: (i,), memory_space=pltpu.SMEM),
                      *_two_part_specs(t, d, tiles_a)],
            out_specs=pl.BlockSpec(memory_space=pl.ANY),
            scratch_shapes=[pltpu.VMEM((t * SUBLANES, LANES), F32), pltpu.VMEM((tm * SUBLANES, LANES), F32),
                            pltpu.SemaphoreType.DMA(()), pltpu.SemaphoreType.DMA(())],
        ),
        out_shape=jax.ShapeDtypeStruct((n_blk * tm * SUBLANES, LANES), F32),
        compiler_params=pltpu.CompilerParams(dimension_semantics=("arbitrary",), vmem_limit_bytes=VMEM_LIMIT),
        name="dispatch",
    )(fill_start, fill_len, nu, dest, x2a, x2b)


def _experts_kernel(b0_ref, nb_ref, cnt_ref, nu_ref, w1_ref, w3_ref, w2_ref, xs_ref, ys_ref,
                    w1b, w3b, w2b, xbuf, ybuf, in_sem, out_sem, *, tm, n_blk):
    e = pl.program_id(0)
    n_used = nu_ref[0]
    blk_rows = tm * SUBLANES

    def x_copy(b):
        return pltpu.make_async_copy(_tile_rows(xs_ref, b * blk_rows, tm), xbuf.at[b % 2], in_sem.at[b % 2])

    def y_copy(b):
        return pltpu.make_async_copy(ybuf.at[b % 2], _tile_rows(ys_ref, b * blk_rows, tm), out_sem.at[b % 2])

    @pl.when(e == 0)
    def _():
        x_copy(0).start()

    w1b[...] = w1_ref[0].astype(BF16)
    w3b[...] = w3_ref[0].astype(BF16)
    w2b[...] = w2_ref[0].astype(BF16)
    b0 = b0_ref[e]
    cnt = cnt_ref[e]

    def block(j, carry):
        b = b0 + j
        x_copy(b).wait()

        @pl.when(b + 1 < n_used)
        def _():
            x_copy(b + 1).start()

        @pl.when(b >= 2)
        def _():
            y_copy(b - 2).wait()

        rows = tm // EXPERT_SPLIT
        for r0 in range(0, tm, rows):
            part = pl.ds(r0 * SUBLANES, rows * SUBLANES)
            live = lax.broadcasted_iota(I32, (rows, 1), 0) < cnt - j * tm - r0
            x = jnp.where(live, _from_token_tiles(xbuf.at[b % 2, part], rows), 0.0).astype(BF16)
            a = _dot(x, w1b[...])
            h = (a * jax.nn.sigmoid(a)) * _dot(x, w3b[...])
            _to_token_tiles(_dot(h.astype(BF16), w2b[...]), ybuf.at[b % 2, part], rows)
        y_copy(b).start()
        return carry

    lax.fori_loop(0, nb_ref[e], block, 0)

    @pl.when(e == pl.num_programs(0) - 1)
    def _():
        @pl.when(n_used >= 2)
        def _():
            y_copy(n_used - 2).wait()

        y_copy(n_used - 1).wait()
        ybuf[0] = jnp.zeros(ybuf.shape[1:], F32)

        def zero(b, carry):
            pltpu.make_async_copy(ybuf.at[0], _tile_rows(ys_ref, b * blk_rows, tm), out_sem.at[0]).start()
            return carry

        lax.fori_loop(n_used, n_blk, zero, 0)

        def zero_wait(b, carry):
            pltpu.make_async_copy(ybuf.at[0], _tile_rows(ys_ref, b * blk_rows, tm), out_sem.at[0]).wait()
            return carry

        lax.fori_loop(n_used, n_blk, zero_wait, 0)


def _experts(blk0, nblk, counts, nu, xs, w1, w3, w2, *, tm):
    blk_rows = tm * SUBLANES
    n_blk = xs.shape[0] // blk_rows
    n_exp, d, de = w1.shape
    wmap = lambda e, *_: (e, 0, 0)
    return pl.pallas_call(
        functools.partial(_experts_kernel, tm=tm, n_blk=n_blk),
        grid_spec=pltpu.PrefetchScalarGridSpec(
            num_scalar_prefetch=4,
            grid=(n_exp,),
            in_specs=[pl.BlockSpec((1, d, de), wmap), pl.BlockSpec((1, d, de), wmap), pl.BlockSpec((1, de, d), wmap),
                      pl.BlockSpec(memory_space=pl.ANY)],
            out_specs=pl.BlockSpec(memory_space=pl.ANY),
            scratch_shapes=[pltpu.VMEM((d, de), BF16), pltpu.VMEM((d, de), BF16), pltpu.VMEM((de, d), BF16),
                            pltpu.VMEM((2, blk_rows, LANES), F32), pltpu.VMEM((2, blk_rows, LANES), F32),
                            pltpu.SemaphoreType.DMA((2,)), pltpu.SemaphoreType.DMA((2,))],
        ),
        out_shape=jax.ShapeDtypeStruct(xs.shape, F32),
        compiler_params=pltpu.CompilerParams(dimension_semantics=("arbitrary",), vmem_limit_bytes=VMEM_LIMIT),
        name="experts",
    )(blk0, nblk, counts, nu, w1, w3, w2, xs)


def _combine_kernel(dcur_ref, dnxt_ref, x2_ref, g_ref, w1_ref, w3_ref, w2_ref, lg_ref, lb_ref, ys_ref,
                    out_ref, buf, sem, *, t):
    i = pl.program_id(0)
    nsteps = pl.num_programs(0)
    slot = i % 2

    def gather(d_ref, s, start):
        def body(n, carry):
            for k in range(TOP_K):
                cp = pltpu.make_async_copy(_tile_rows(ys_ref, d_ref[n * TOP_K + k]),
                                           _tile_rows(buf.at[s, k], n * SUBLANES), sem.at[s])
                if start:
                    cp.start(priority=k % 2)
                else:
                    cp.wait()
            return carry

        lax.fori_loop(0, t, body, 0)

    @pl.when(i == 0)
    def _():
        gather(dcur_ref, 0, True)

    @pl.when(i + 1 < nsteps)
    def _():
        gather(dnxt_ref, 1 - slot, True)

    gather(dcur_ref, slot, False)

    x2 = x2_ref[...]
    g = g_ref[...]
    acc = g[:, 0:1] * _from_token_tiles(buf.at[slot, 0], t)
    for k in range(1, TOP_K):
        acc = acc + g[:, k:k + 1] * _from_token_tiles(buf.at[slot, k], t)
    xb = x2.astype(BF16)
    a = _dot(xb, w1_ref[...])
    hs = (a * jax.nn.sigmoid(a)) * _dot(xb, w3_ref[...])
    shared = _dot(hs.astype(BF16), w2_ref[...])
    out_ref[...] = _ln(x2 * ALPHA + (acc + shared), lg_ref[...], lb_ref[...])


def _combine(dest, x2, gate_t, w1s, w3s, w2s, lg, lb, ys, *, row_off):
    n_rows, d = x2.shape
    n = dest.shape[0] // TOP_K
    t = TOKEN_TILE
    assert row_off % t == 0 and n_rows % t == 0 and n % t == 0
    off = row_off // t
    last = n // t - 1
    de = w1s.shape[-1]
    const = lambda shape: pl.BlockSpec(shape, lambda i: (0,) * len(shape))
    return pl.pallas_call(
        functools.partial(_combine_kernel, t=t),
        grid=(n_rows // t,),
        in_specs=[pl.BlockSpec((t * TOP_K,), lambda i: (off + i,), memory_space=pltpu.SMEM),
                  pl.BlockSpec((t * TOP_K,), lambda i: (jnp.minimum(off + i + 1, last),), memory_space=pltpu.SMEM),
                  pl.BlockSpec((t, d), lambda i: (i, 0)),
                  pl.BlockSpec((t, TOP_K), lambda i: (off + i, 0)),
                  const((d, de)), const((d, de)), const((de, d)), const((1, d)), const((1, d)),
                  pl.BlockSpec(memory_space=pl.ANY)],
        out_specs=pl.BlockSpec((t, d), lambda i: (i, 0)),
        out_shape=jax.ShapeDtypeStruct((n_rows, d), F32),
        scratch_shapes=[pltpu.VMEM((2, TOP_K, t * SUBLANES, LANES), F32), pltpu.SemaphoreType.DMA((2,))],
        compiler_params=pltpu.CompilerParams(dimension_semantics=("arbitrary",), vmem_limit_bytes=VMEM_LIMIT),
        name="combine",
    )(dest, dest, x2, gate_t, w1s, w3s, w2s, lg, lb, ys)


EXPERT_TM = 512
EXPERT_SPLIT = 2


def _moe_plan(cnt):
    tm = EXPERT_TM
    counts = cnt[:, 0].astype(I32)
    padded = (counts + tm - 1) // tm * tm
    pad_end = jnp.cumsum(padded)
    pad_start = pad_end - padded
    n_used = pad_end[-1] // tm
    fill_start = (pad_start + counts).astype(I32)
    fill_len = (padded - counts).astype(I32)
    return (pad_start.astype(I32), (pad_start // tm).astype(I32), (padded // tm).astype(I32), counts,
            n_used.reshape(1).astype(I32), fill_start, fill_len)


def _slots_kernel(ps_ref, e_ref, r_ref, d_ref):
    e = e_ref[...]

    def body(j, base):
        return jnp.where(e == j, ps_ref[j], base)

    base = lax.fori_loop(0, N_EXPERTS, body, jnp.zeros_like(e), unroll=8)
    d_ref[...] = (base + r_ref[...]) * SUBLANES


def _slots(pad_start, top_e, rank):
    k, n = top_e.shape
    tl = next(c for c in (2048, 1536, 1024, 512, 256, 128) if n % c == 0)
    spec = pl.BlockSpec((k, tl), lambda i, ps: (0, i))
    return pl.pallas_call(
        _slots_kernel,
        grid_spec=pltpu.PrefetchScalarGridSpec(num_scalar_prefetch=1, grid=(n // tl,), in_specs=[spec, spec],
                                               out_specs=spec),
        out_shape=jax.ShapeDtypeStruct((k, n), I32),
        compiler_params=pltpu.CompilerParams(dimension_semantics=("arbitrary",)),
        name="slots",
    )(pad_start, top_e, rank)


PAST_LEN = 1024


def kernel(x_prompt, x_sample, state_gla, cache_pool, cache_mem_k, cache_mem_v, mem_prompt, ln_in_g, ln_in_b, w_in, w_gate_up, b_gate, gla_norm_g, pool_w, pool_scale, w_out, ln1_g, ln1_b, wq_mem, wk_mem, wv_mem, wo_mem, ln2_g, ln2_b, w_router, router_bias, w1_exp, w3_exp, w2_exp, w1_sh, w3_sh, w2_sh, ln3_g, ln3_b):
    assert w_in.shape[0] == 1, "single-layer trunk"
    bp, lp, d = x_prompt.shape
    bs, ls, _ = x_sample.shape
    n_p, n_s = bp * lp, bs * ls
    n_all = n_p + n_s

    wts_a = _prep_trunk_a_weights(ln_in_g, ln_in_b, w_in[0], w_gate_up[0], b_gate[0], gla_norm_g[0], pool_w[0],
                                  pool_scale[0], w_out[0], ln1_g[0], ln1_b[0])
    s0 = jnp.zeros((bp, GLA_HEADS, GLA_DK, GLA_DV), F32)
    h0 = jnp.zeros((bp, POOL_HIST, POOL_WIDTH), F32)
    x1p, sp, hp = _trunk_a(x_prompt, s0, h0, wts_a, start_pos=0)
    x1s, ss, hs = _trunk_a(x_sample, state_gla[0], cache_pool[0], wts_a, start_pos=PAST_LEN)

    mk, mv, mkb, mvb = _mem_kv(mem_prompt.reshape(bp * N_MEM, d), wk_mem[0].astype(BF16), wv_mem[0].astype(BF16))
    wq, wo = wq_mem[0].astype(BF16), wo_mem[0].astype(BF16)
    g2, b2 = ln2_g[0].reshape(1, d), ln2_b[0].reshape(1, d)
    x2p = _attn(x1p, mkb.reshape(bp, N_MEM, d), mvb.reshape(bp, N_MEM, d), wq, wo, g2, b2)
    x2s = _attn(x1s, cache_mem_k[0].reshape(bs, N_MEM, d).astype(BF16),
                cache_mem_v[0].reshape(bs, N_MEM, d).astype(BF16), wq, wo, g2, b2)

    wrt = w_router[0].T
    wrt_h = wrt.astype(BF16)
    wrt_m = (wrt - wrt_h.astype(F32)).astype(BF16)
    top_e, gate, rank, cnt = _router(x2p, x2s, jnp.stack([wrt_h, wrt_m]), router_bias[0].reshape(N_EXPERTS, 1))

    assert (n_all * TOP_K) % EXPERT_TM == 0
    n_blk = n_all * TOP_K // EXPERT_TM + N_EXPERTS
    pad_start, blk0, nblk, counts, nu, fill_start, fill_len = _moe_plan(cnt)
    dest = _slots(pad_start, top_e, rank).T.reshape(-1)
    xs = _dispatch(fill_start, fill_len, nu, dest, x2p, x2s, n_blk=n_blk, tm=EXPERT_TM)
    ys = _experts(blk0, nblk, counts, nu, xs, w1_exp[0], w3_exp[0], w2_exp[0], tm=EXPERT_TM)
    sh = (w1_sh[0].astype(BF16), w3_sh[0].astype(BF16), w2_sh[0].astype(BF16),
          ln3_g[0].reshape(1, d), ln3_b[0].reshape(1, d))
    gate_t = gate.T
    yp = _combine(dest, x2p, gate_t, *sh, ys, row_off=0)
    ysm = _combine(dest, x2s, gate_t, *sh, ys, row_off=n_p)

    return (yp.reshape(bp, lp, d), ysm.reshape(bs, ls, d), sp[None], hp[None],
            mk.reshape(1, bp, N_MEM, MEM_HEADS, MEM_DH), mv.reshape(1, bp, N_MEM, MEM_HEADS, MEM_DH),
            ss[None], hs[None])
```

```python
import functools

import jax
import jax.numpy as jnp
from jax import lax
from jax.experimental import pallas as pl
from jax.experimental.pallas import tpu as pltpu

F32 = jnp.float32
BF16 = jnp.bfloat16
I32 = jnp.int32

D_MODEL = 1024
CHUNK = 64
SUB = 16
GLA_HEADS = 4
GLA_DK = 64
GLA_DV = 128
GLA_KEY = GLA_HEADS * GLA_DK
GLA_WIDTH = GLA_HEADS * GLA_DV
GATE_RANK = 16
GATE_PAD = 128
POOL_WIDTH = 512
POOL_WINDOWS = (2, 4, 8, 16)
POOL_GC = 128
POOL_HIST = 15
HIST_ROWS = 16
N_MEM = 256
MEM_HEADS = 4
MEM_DH = 256
N_EXPERTS = 256
N_GROUPS = 8
EXPERTS_PER_GROUP = 32
TOPK_GROUPS = 4
TOP_K = 8
ROUTED_SCALE = 2.5
D_EXPERT = 256
TOKEN_TILE = 256
SUBLANES, LANES = 8, 128
assert D_MODEL == SUBLANES * LANES
ALPHA = 2.0 ** 0.25
EPS = 1e-5
OFF_Q, OFF_K, OFF_V, OFF_G, OFF_U, OFF_GD = 0, 256, 512, 1024, 1536, 2048
D_IN_PAD = OFF_GD + GATE_PAD

VMEM_LIMIT = 56 * 1024 * 1024


def _ln(x, g, b):
    mu = jnp.mean(x, axis=-1, keepdims=True)
    xc = x - mu
    var = jnp.mean(xc * xc, axis=-1, keepdims=True)
    return xc * lax.rsqrt(var + EPS) * g + b


def _dot(a, b):
    return jnp.dot(a, b, preferred_element_type=F32)


def _dot_nt(a, b):
    return lax.dot_general(a, b, (((1,), (1,)), ((), ())), preferred_element_type=F32)


def _dot_tn(a, b):
    return lax.dot_general(a, b, (((0,), (0,)), ((), ())), preferred_element_type=F32)


def _split3(x):
    h = x.astype(BF16)
    r = x - h.astype(F32)
    m = r.astype(BF16)
    l = (r - m.astype(F32)).astype(BF16)
    return h, m, l


def _trunk_a_kernel(x_ref, s0_ref, h0_ref, lng_ref, lnb_ref, win_ref, wgu_ref, bg_ref, gng_ref,
                    pw_ref, ps_ref, wout_ref, l1g_ref, l1b_ref,
                    x1_ref, sn_ref, hn_ref,
                    proj_scr, ext_scr, s_scr, op_scr, *, tl, chunk, start_pos):
    t = pl.program_id(1)
    nt = pl.num_programs(1)

    @pl.when(t == 0)
    def _():
        s_scr[...] = s0_ref[0]
        ext_scr[0:1, :] = jnp.zeros((1, POOL_WIDTH), F32)
        ext_scr[1:HIST_ROWS, :] = h0_ref[0]

    xn = _ln(x_ref[0], lng_ref[...], lnb_ref[...])
    proj_scr[...] = _dot(xn.astype(BF16), win_ref[...])

    c = chunk
    shift = lambda a, n: lax.shift_right_logical(a, n.bit_length() - 1)
    gd = proj_scr[:, OFF_GD:OFF_GD + GATE_PAD]
    z = _dot(gd.astype(BF16), wgu_ref[...]) + bg_ref[...]
    lf = (jnp.minimum(z, 0.0) - jnp.log1p(jnp.exp(-jnp.abs(z)))) * (1.0 / 16.0)
    trow = lax.broadcasted_iota(I32, (tl, tl), 0)
    tcol = lax.broadcasted_iota(I32, (tl, tl), 1)
    tri = jnp.logical_and(tcol <= trow, shift(tcol, c) == shift(trow, c)).astype(BF16)
    lh, lm, ll = _split3(lf)
    cum = _dot(tri, lh) + _dot(tri, lm) + _dot(tri, ll)
    q_all = proj_scr[:, OFF_Q:OFF_Q + GLA_KEY] * (GLA_DK ** -0.5)
    k_all = proj_scr[:, OFF_K:OFF_K + GLA_KEY]
    qs_all = q_all * jnp.exp(cum)

    causal = lax.broadcasted_iota(I32, (c, c), 1) <= lax.broadcasted_iota(I32, (c, c), 0)
    eye_dk = lax.broadcasted_iota(I32, (GLA_DK, GLA_DK), 0) == lax.broadcasted_iota(I32, (GLA_DK, GLA_DK), 1)
    n_sub = c // SUB
    rblk = shift(lax.broadcasted_iota(I32, (c, n_sub * GLA_DK), 0), SUB)
    lblk = shift(lax.broadcasted_iota(I32, (c, n_sub * GLA_DK), 1), GLA_DK)
    lblk_row = shift(lax.broadcasted_iota(I32, (1, n_sub * GLA_DK), 1), GLA_DK)
    mask_q = rblk == lblk
    mask_k = rblk <= lblk

    def tile_lanes(a):
        return jnp.concatenate([a] * n_sub, axis=1)

    states = [s_scr[h] for h in range(GLA_HEADS)]
    for ci in range(tl // c):
        rs = slice(ci * c, (ci + 1) * c)
        cum_c = cum[rs]
        last = cum_c[c - 1:c, :]
        ks_c = k_all[rs] * jnp.exp(last - cum_c)
        for h in range(GLA_HEADS):
            ksl = slice(h * GLA_DK, (h + 1) * GLA_DK)
            vsl = slice(h * GLA_DV, (h + 1) * GLA_DV)
            v_h = proj_scr[rs, OFF_V + h * GLA_DV:OFF_V + (h + 1) * GLA_DV].astype(BF16)
            cum_t = tile_lanes(cum_c[:, ksl])
            q_t = tile_lanes(q_all[rs, ksl])
            k_t = tile_lanes(k_all[rs, ksl])
            ref_row = jnp.zeros((1, n_sub * GLA_DK), F32)
            for i in range(1, n_sub):
                ref_row = jnp.where(lblk_row == i, cum_t[i * SUB - 1:i * SUB, :], ref_row)
            arg = cum_t - ref_row
            lhs = jnp.where(mask_q, q_t * jnp.exp(jnp.where(mask_q, arg, 0.0)), 0.0)
            rhs = jnp.where(mask_k, k_t * jnp.exp(jnp.where(mask_k, -arg, 0.0)), 0.0)
            att = jnp.where(causal, _dot_nt(lhs.astype(BF16), rhs.astype(BF16)), 0.0)
            s_h = states[h]
            o_h = _dot(att.astype(BF16), v_h) + _dot(qs_all[rs, ksl].astype(BF16), s_h.astype(BF16))
            dcol = jnp.sum(jnp.where(eye_dk, jnp.broadcast_to(jnp.exp(last[:, ksl]), (GLA_DK, GLA_DK)), 0.0),
                           axis=1, keepdims=True)
            states[h] = dcol * s_h + _dot_tn(ks_c[:, ksl].astype(BF16), v_h)
            o_h = o_h * lax.rsqrt(jnp.mean(o_h * o_h, axis=-1, keepdims=True) + EPS) * gng_ref[...]
            g_h = proj_scr[rs, OFF_G + h * GLA_DV:OFF_G + (h + 1) * GLA_DV]
            op_scr[rs, vsl] = (o_h * (g_h * jax.nn.sigmoid(g_h))).astype(BF16)
    for h in range(GLA_HEADS):
        s_scr[h] = states[h]

    u = proj_scr[:, OFF_U:OFF_U + POOL_WIDTH]
    ext_scr[HIST_ROWS:HIST_ROWS + tl, :] = u
    n_valid = start_pos + t * tl + lax.broadcasted_iota(I32, (tl, 1), 0) + 1
    for gi, w in enumerate(POOL_WINDOWS):
        lsl = slice(gi * POOL_GC, (gi + 1) * POOL_GC)
        win = ext_scr[HIST_ROWS:HIST_ROWS + tl, lsl]
        for s in range(1, w):
            win = win + ext_scr[HIST_ROWS - s:HIST_ROWS - s + tl, lsl]
        cnt = jnp.minimum(w, n_valid).astype(F32)
        r = win / cnt - u[:, lsl]
        p = _dot(r.astype(BF16), pw_ref[gi]) * ps_ref[:, lsl]
        op_scr[:, GLA_WIDTH + gi * POOL_GC:GLA_WIDTH + (gi + 1) * POOL_GC] = p.astype(BF16)
    tail = ext_scr[tl:tl + HIST_ROWS, :]
    ext_scr[0:HIST_ROWS, :] = tail

    mix = _dot(op_scr[...], wout_ref[...])
    x1_ref[0] = _ln(xn * ALPHA + mix, l1g_ref[...], l1b_ref[...])

    @pl.when(t == nt - 1)
    def _():
        sn_ref[0] = s_scr[...]
        hn_ref[0] = ext_scr[1:HIST_ROWS, :]


def _trunk_a(x, s0, h0, wts, *, start_pos):
    b, l, d = x.shape
    tl = min(l, 256)
    chunk = min(tl, CHUNK)
    assert l % tl == 0 and tl % chunk == 0 and chunk % SUB == 0 and l >= HIST_ROWS
    nt = l // tl
    kern = functools.partial(_trunk_a_kernel, tl=tl, chunk=chunk, start_pos=start_pos)
    const = lambda shape: pl.BlockSpec(shape, lambda bi, ti: (0,) * len(shape))
    return pl.pallas_call(
        kern,
        grid=(b, nt),
        in_specs=[
            pl.BlockSpec((1, tl, d), lambda bi, ti: (bi, ti, 0)),
            pl.BlockSpec((1, GLA_HEADS, GLA_DK, GLA_DV), lambda bi, ti: (bi, 0, 0, 0)),
            pl.BlockSpec((1, POOL_HIST, POOL_WIDTH), lambda bi, ti: (bi, 0, 0)),
            const((1, d)), const((1, d)),
            const((d, D_IN_PAD)), const((GATE_PAD, GLA_KEY)), const((1, GLA_KEY)), const((1, GLA_DV)),
            const((len(POOL_WINDOWS), POOL_GC, POOL_GC)), const((1, POOL_WIDTH)),
            const((GLA_WIDTH + POOL_WIDTH, d)), const((1, d)), const((1, d)),
        ],
        out_specs=[
            pl.BlockSpec((1, tl, d), lambda bi, ti: (bi, ti, 0)),
            pl.BlockSpec((1, GLA_HEADS, GLA_DK, GLA_DV), lambda bi, ti: (bi, 0, 0, 0)),
            pl.BlockSpec((1, POOL_HIST, POOL_WIDTH), lambda bi, ti: (bi, 0, 0)),
        ],
        out_shape=[
            jax.ShapeDtypeStruct((b, l, d), F32),
            jax.ShapeDtypeStruct((b, GLA_HEADS, GLA_DK, GLA_DV), F32),
            jax.ShapeDtypeStruct((b, POOL_HIST, POOL_WIDTH), F32),
        ],
        scratch_shapes=[
            pltpu.VMEM((tl, D_IN_PAD), F32),
            pltpu.VMEM((HIST_ROWS + tl, POOL_WIDTH), F32),
            pltpu.VMEM((GLA_HEADS, GLA_DK, GLA_DV), F32),
            pltpu.VMEM((tl, GLA_WIDTH + POOL_WIDTH), BF16),
        ],
        compiler_params=pltpu.CompilerParams(
            dimension_semantics=("arbitrary", "arbitrary"), vmem_limit_bytes=VMEM_LIMIT),
        name="trunk_a",
    )(x, s0, h0, *wts)


def _prep_trunk_a_weights(ln_in_g, ln_in_b, w_in, w_gate_up, b_gate, gla_norm_g, pool_w, pool_scale, w_out,
                          ln1_g, ln1_b):
    d = D_MODEL
    p_gd = 2 * GLA_KEY + 2 * GLA_WIDTH
    w_in_r = jnp.concatenate(
        [w_in[:, :p_gd], w_in[:, p_gd + GATE_RANK:], w_in[:, p_gd:p_gd + GATE_RANK],
         jnp.zeros((d, GATE_PAD - GATE_RANK), w_in.dtype)], axis=1).astype(BF16)
    wgu = jnp.concatenate([w_gate_up, jnp.zeros((GATE_PAD - GATE_RANK, GLA_KEY), w_gate_up.dtype)],
                          axis=0).astype(BF16)
    return (ln_in_g.reshape(1, d), ln_in_b.reshape(1, d), w_in_r, wgu, b_gate.reshape(1, GLA_KEY),
            gla_norm_g.reshape(1, GLA_DV), pool_w.astype(BF16), pool_scale.reshape(1, POOL_WIDTH),
            w_out.astype(BF16), ln1_g.reshape(1, d), ln1_b.reshape(1, d))


def _mem_kv_kernel(m_ref, wk_ref, wv_ref, k_ref, v_ref, kb_ref, vb_ref):
    m = m_ref[...].astype(BF16)
    k = _dot(m, wk_ref[...])
    v = _dot(m, wv_ref[...])
    k_ref[...] = k
    v_ref[...] = v
    kb_ref[...] = k.astype(BF16)
    vb_ref[...] = v.astype(BF16)


def _mem_kv(mem, wk, wv):
    m, d = mem.shape
    tm = min(m, 512)
    assert m % tm == 0
    row = pl.BlockSpec((tm, d), lambda i: (i, 0))
    wspec = pl.BlockSpec((d, d), lambda i: (0, 0))
    return pl.pallas_call(
        _mem_kv_kernel,
        grid=(m // tm,),
        in_specs=[row, wspec, wspec],
        out_specs=[row, row, row, row],
        out_shape=[jax.ShapeDtypeStruct((m, d), F32), jax.ShapeDtypeStruct((m, d), F32),
                   jax.ShapeDtypeStruct((m, d), BF16), jax.ShapeDtypeStruct((m, d), BF16)],
        compiler_params=pltpu.CompilerParams(dimension_semantics=("arbitrary",), vmem_limit_bytes=VMEM_LIMIT),
        name="mem_kv",
    )(mem, wk, wv)


def _attn_kernel(x1_ref, k_ref, v_ref, wq_ref, wo_ref, g_ref, b_ref, x2_ref, o_scr):
    x1 = x1_ref[0]
    q = (_dot(x1.astype(BF16), wq_ref[...]) * (MEM_DH ** -0.5)).astype(BF16)
    for h in range(MEM_HEADS):
        hs = slice(h * MEM_DH, (h + 1) * MEM_DH)
        s = _dot_nt(q[:, hs], k_ref[0, :, hs])
        e = jnp.exp(s - jnp.max(s, axis=-1, keepdims=True))
        p = e / jnp.sum(e, axis=-1, keepdims=True)
        o_scr[:, hs] = _dot(p.astype(BF16), v_ref[0, :, hs]).astype(BF16)
    attn = _dot(o_scr[...], wo_ref[...])
    x2_ref[...] = _ln(x1 * ALPHA + attn, g_ref[...], b_ref[...])


def _attn(x1, mem_k, mem_v, wq, wo, g, b):
    bsz, l, d = x1.shape
    tl = min(l, 256)
    assert l % tl == 0
    nt = l // tl
    const = lambda shape: pl.BlockSpec(shape, lambda bi, ti: (0,) * len(shape))
    return pl.pallas_call(
        _attn_kernel,
        grid=(bsz, nt),
        in_specs=[
            pl.BlockSpec((1, tl, d), lambda bi, ti: (bi, ti, 0)),
            pl.BlockSpec((1, N_MEM, d), lambda bi, ti: (bi, 0, 0)),
            pl.BlockSpec((1, N_MEM, d), lambda bi, ti: (bi, 0, 0)),
            const((d, d)), const((d, d)), const((1, d)), const((1, d)),
        ],
        out_specs=pl.BlockSpec((tl, d), lambda bi, ti: (bi * nt + ti, 0)),
        out_shape=jax.ShapeDtypeStruct((bsz * l, d), F32),
        scratch_shapes=[pltpu.VMEM((tl, d), BF16)],
        compiler_params=pltpu.CompilerParams(
            dimension_semantics=("arbitrary", "arbitrary"), vmem_limit_bytes=VMEM_LIMIT),
        name="attn",
    )(x1, mem_k, mem_v, wq, wo, g, b)


def _two_part_specs(t, d, tiles_a):
    spec_a = pl.BlockSpec((t, d), lambda i, *_: (jnp.minimum(i, tiles_a - 1), 0))
    spec_b = pl.BlockSpec((t, d), lambda i, *_: (jnp.maximum(i - tiles_a, 0), 0))
    return spec_a, spec_b


def _router_kernel(xa_ref, xb_ref, wrt_ref, bias_ref, e_ref, g_ref, r_ref, cnt_ref, cnt_scr, *, tl, tiles_a):
    i = pl.program_id(0)

    @pl.when(i == 0)
    def _():
        cnt_scr[...] = jnp.zeros_like(cnt_scr)

    x = jnp.where(i < tiles_a, xa_ref[...], xb_ref[...])
    xh = x.astype(BF16)
    xm = (x - xh.astype(F32)).astype(BF16)
    wh = wrt_ref[0]
    wm = wrt_ref[1]
    logits = _dot_nt(wh, xh) + (_dot_nt(wh, xm) + _dot_nt(wm, xh))
    scores = jax.nn.sigmoid(logits)
    biased = scores + bias_ref[...]
    ninf = -jnp.inf
    eg = EXPERTS_PER_GROUP
    riota = lax.broadcasted_iota(I32, (eg, tl), 0)
    gs_rows = []
    for g in range(N_GROUPS):
        blk = biased[g * eg:(g + 1) * eg, :]
        m1 = jnp.max(blk, axis=0, keepdims=True)
        i1 = jnp.min(jnp.where(blk == m1, riota, eg), axis=0, keepdims=True)
        m2 = jnp.max(jnp.where(riota == i1, ninf, blk), axis=0, keepdims=True)
        gs_rows.append(m1 + m2)
    gs = jnp.concatenate(gs_rows, axis=0)
    giota = lax.broadcasted_iota(I32, (N_GROUPS, tl), 0)
    sel = jnp.zeros((N_GROUPS, tl), jnp.bool_)
    for _ in range(TOPK_GROUPS):
        m = jnp.max(gs, axis=0, keepdims=True)
        gi = jnp.min(jnp.where(gs == m, giota, N_GROUPS), axis=0, keepdims=True)
        hit = giota == gi
        sel = jnp.logical_or(sel, hit)
        gs = jnp.where(hit, ninf, gs)
    self = jnp.where(sel, 1.0, 0.0)
    masked = jnp.concatenate(
        [jnp.where(self[g:g + 1, :] > 0.5, biased[g * eg:(g + 1) * eg, :], ninf) for g in range(N_GROUPS)], axis=0)
    eiota = lax.broadcasted_iota(I32, (N_EXPERTS, tl), 0)
    idx_rows, sc_rows = [], []
    multi = jnp.zeros((N_EXPERTS, tl), F32)
    for _ in range(TOP_K):
        m = jnp.max(masked, axis=0, keepdims=True)
        idx = jnp.min(jnp.where(masked == m, eiota, N_EXPERTS), axis=0, keepdims=True)
        hit = eiota == idx
        sc_rows.append(jnp.sum(jnp.where(hit, scores, 0.0), axis=0, keepdims=True))
        idx_rows.append(idx)
        multi = jnp.where(hit, 1.0, multi)
        masked = jnp.where(hit, ninf, masked)
    top_e = jnp.concatenate(idx_rows, axis=0)
    sc = jnp.concatenate(sc_rows, axis=0)
    e_ref[...] = top_e
    g_ref[...] = sc / jnp.sum(sc, axis=0, keepdims=True) * ROUTED_SCALE
    mh = multi.astype(BF16)
    before = (lax.broadcasted_iota(I32, (tl, tl), 0) < lax.broadcasted_iota(I32, (tl, tl), 1)).astype(BF16)
    running = cnt_scr[...]
    rankmat = _dot(mh, before) + jnp.concatenate([running] * (tl // 128), axis=1)
    r_rows = [jnp.sum(jnp.where(eiota == idx_rows[k], rankmat, 0.0), axis=0, keepdims=True) for k in range(TOP_K)]
    r_ref[...] = jnp.concatenate(r_rows, axis=0).astype(I32)
    total = running + _dot(mh, jnp.ones((tl, 128), BF16))
    cnt_scr[...] = total
    cnt_ref[...] = total


def _router(x2a, x2b, wrt, bias_col):
    d = x2a.shape[1]
    tl = TOKEN_TILE
    assert x2a.shape[0] % tl == 0 and x2b.shape[0] % tl == 0
    tiles_a = x2a.shape[0] // tl
    n = x2a.shape[0] + x2b.shape[0]
    kspec = pl.BlockSpec((TOP_K, tl), lambda i: (0, i))
    return pl.pallas_call(
        functools.partial(_router_kernel, tl=tl, tiles_a=tiles_a),
        grid=(n // tl,),
        in_specs=[*_two_part_specs(tl, d, tiles_a),
                  pl.BlockSpec((2, N_EXPERTS, d), lambda i: (0, 0, 0)),
                  pl.BlockSpec((N_EXPERTS, 1), lambda i: (0, 0))],
        out_specs=[kspec, kspec, kspec, pl.BlockSpec((N_EXPERTS, 128), lambda i: (0, 0))],
        out_shape=[jax.ShapeDtypeStruct((TOP_K, n), I32), jax.ShapeDtypeStruct((TOP_K, n), F32),
                   jax.ShapeDtypeStruct((TOP_K, n), I32), jax.ShapeDtypeStruct((N_EXPERTS, 128), F32)],
        scratch_shapes=[pltpu.VMEM((N_EXPERTS, 128), F32)],
        compiler_params=pltpu.CompilerParams(dimension_semantics=("arbitrary",), vmem_limit_bytes=VMEM_LIMIT),
        name="router",
    )(x2a, x2b, wrt, bias_col)


def _to_token_tiles(x, dst_ref, t):
    for s in range(SUBLANES):
        dst_ref[pl.ds(s, t, stride=SUBLANES), :] = x[:, s * LANES:(s + 1) * LANES]


def _from_token_tiles(src_ref, t):
    return jnp.concatenate([src_ref[pl.ds(s, t, stride=SUBLANES), :] for s in range(SUBLANES)], axis=1)


def _tile_rows(ref, row0, n_tok=1):
    return ref.at[pl.ds(pl.multiple_of(row0, SUBLANES), n_tok * SUBLANES)]


def _dispatch_kernel(fs_ref, fl_ref, nu_ref, dest_ref, xa_ref, xb_ref, xs_ref, stage, zbuf, sem, zsem,
                     *, t, tm, tiles_a, n_blk):
    i = pl.program_id(0)
    nsteps = pl.num_programs(0)

    def zero_fill(start):
        def go(cp):
            if start:
                cp.start()
            else:
                cp.wait()

        def per_expert(e, carry):
            ln = fl_ref[e]
            off = fs_ref[e]
            bit = tm // 2
            while bit:
                @pl.when((ln & bit) != 0)
                def _(off=off, bit=bit):
                    go(pltpu.make_async_copy(_tile_rows(zbuf, 0, bit), _tile_rows(xs_ref, off * SUBLANES, bit), zsem))
                off = off + (ln & bit)
                bit //= 2
            return carry

        lax.fori_loop(0, N_EXPERTS, per_expert, 0)

        def per_block(b, carry):
            go(pltpu.make_async_copy(zbuf, _tile_rows(xs_ref, b * (tm * SUBLANES), tm), zsem))
            return carry

        lax.fori_loop(nu_ref[0], n_blk, per_block, 0)

    @pl.when(i == 0)
    def _():
        zbuf[...] = jnp.zeros_like(zbuf)
        zero_fill(True)

    def drain(slot):
        for k in range(TOP_K):
            pltpu.make_async_copy(stage.at[slot], _tile_rows(xs_ref, 0, t), sem.at[slot]).wait()

    def step(slot):
        @pl.when(i >= 2)
        def _():
            drain(slot)

        _to_token_tiles(jnp.where(i < tiles_a, xa_ref[...], xb_ref[...]), stage.at[slot], t)

        def body(n, carry):
            src = _tile_rows(stage.at[slot], n * SUBLANES)
            for k in range(TOP_K):
                dst = _tile_rows(xs_ref, dest_ref[n * TOP_K + k])
                pltpu.make_async_copy(src, dst, sem.at[slot]).start(priority=k % 2)
            return carry

        lax.fori_loop(0, t, body, 0)

    for slot in range(2):
        @pl.when(i % 2 == slot)
        def _(slot=slot):
            step(slot)

    @pl.when(i == 0)
    def _():
        zero_fill(False)

    @pl.when(i == nsteps - 1)
    def _():
        for slot in range(2):
            @pl.when(jnp.logical_or(nsteps >= 2, (nsteps - 1) % 2 == slot))
            def _(slot=slot):
                drain(slot)


def _dispatch(fill_start, fill_len, nu, dest, x2a, x2b, *, n_blk, tm):
    d = x2a.shape[1]
    t = TOKEN_TILE
    assert x2a.shape[0] % t == 0 and x2b.shape[0] % t == 0
    tiles_a = x2a.shape[0] // t
    n = x2a.shape[0] + x2b.shape[0]
    return pl.pallas_call(
        functools.partial(_dispatch_kernel, t=t, tm=tm, tiles_a=tiles_a, n_blk=n_blk),
        grid_spec=pltpu.PrefetchScalarGridSpec(
            num_scalar_prefetch=3,
            grid=(n // t,),
            in_specs=[pl.BlockSpec((t * TOP_K,), lambda i, *_: (i,), memory_space=pltpu.SMEM),
                      *_two_part_specs(t, d, tiles_a)],
            out_specs=pl.BlockSpec(memory_space=pl.ANY),
            scratch_shapes=[pltpu.VMEM((2, t * SUBLANES, LANES), F32), pltpu.VMEM((tm * SUBLANES, LANES), F32),
                            pltpu.SemaphoreType.DMA((2,)), pltpu.SemaphoreType.DMA(())],
        ),
        out_shape=jax.ShapeDtypeStruct((n_blk * tm * SUBLANES, LANES), F32),
        compiler_params=pltpu.CompilerParams(dimension_semantics=("arbitrary",), vmem_limit_bytes=VMEM_LIMIT),
        name="dispatch",
    )(fill_start, fill_len, nu, dest, x2a, x2b)


def _experts_kernel(b0_ref, nb_ref, cnt_ref, nu_ref, w1_ref, w3_ref, w2_ref, xs_ref, ys_ref,
                    w1b, w3b, w2b, xbuf, ybuf, in_sem, out_sem, *, tm, n_blk):
    e = pl.program_id(0)
    n_used = nu_ref[0]
    blk_rows = tm * SUBLANES

    def x_copy(b):
        return pltpu.make_async_copy(_tile_rows(xs_ref, b * blk_rows, tm), xbuf.at[b % 2], in_sem.at[b % 2])

    def y_copy(b):
        return pltpu.make_async_copy(ybuf.at[b % 2], _tile_rows(ys_ref, b * blk_rows, tm), out_sem.at[b % 2])

    @pl.when(e == 0)
    def _():
        x_copy(0).start()

    w1b[...] = w1_ref[0].astype(BF16)
    w3b[...] = w3_ref[0].astype(BF16)
    w2b[...] = w2_ref[0].astype(BF16)
    b0 = b0_ref[e]
    cnt = cnt_ref[e]

    def block(j, carry):
        b = b0 + j
        x_copy(b).wait()

        @pl.when(b + 1 < n_used)
        def _():
            x_copy(b + 1).start()

        @pl.when(b >= 2)
        def _():
            y_copy(b - 2).wait()

        rows = tm // EXPERT_SPLIT
        for r0 in range(0, tm, rows):
            part = pl.ds(r0 * SUBLANES, rows * SUBLANES)
            live = lax.broadcasted_iota(I32, (rows, 1), 0) < cnt - j * tm - r0
            x = jnp.where(live, _from_token_tiles(xbuf.at[b % 2, part], rows), 0.0).astype(BF16)
            a = _dot(x, w1b[...])
            h = (a * jax.nn.sigmoid(a)) * _dot(x, w3b[...])
            _to_token_tiles(_dot(h.astype(BF16), w2b[...]), ybuf.at[b % 2, part], rows)
        y_copy(b).start()
        return carry

    lax.fori_loop(0, nb_ref[e], block, 0)

    @pl.when(e == pl.num_programs(0) - 1)
    def _():
        @pl.when(n_used >= 2)
        def _():
            y_copy(n_used - 2).wait()

        y_copy(n_used - 1).wait()
        ybuf[0] = jnp.zeros(ybuf.shape[1:], F32)

        def zero(b, carry):
            pltpu.make_async_copy(ybuf.at[0], _tile_rows(ys_ref, b * blk_rows, tm), out_sem.at[0]).start()
            return carry

        lax.fori_loop(n_used, n_blk, zero, 0)

        def zero_wait(b, carry):
            pltpu.make_async_copy(ybuf.at[0], _tile_rows(ys_ref, b * blk_rows, tm), out_sem.at[0]).wait()
            return carry

        lax.fori_loop(n_used, n_blk, zero_wait, 0)


def _experts(blk0, nblk, counts, nu, xs, w1, w3, w2, *, tm):
    blk_rows = tm * SUBLANES
    n_blk = xs.shape[0] // blk_rows
    n_exp, d, de = w1.shape
    wmap = lambda e, *_: (e, 0, 0)
    return pl.pallas_call(
        functools.partial(_experts_kernel, tm=tm, n_blk=n_blk),
        grid_spec=pltpu.PrefetchScalarGridSpec(
            num_scalar_prefetch=4,
            grid=(n_exp,),
            in_specs=[pl.BlockSpec((1, d, de), wmap), pl.BlockSpec((1, d, de), wmap), pl.BlockSpec((1, de, d), wmap),
                      pl.BlockSpec(memory_space=pl.ANY)],
            out_specs=pl.BlockSpec(memory_space=pl.ANY),
            scratch_shapes=[pltpu.VMEM((d, de), BF16), pltpu.VMEM((d, de), BF16), pltpu.VMEM((de, d), BF16),
                            pltpu.VMEM((2, blk_rows, LANES), F32), pltpu.VMEM((2, blk_rows, LANES), F32),
                            pltpu.SemaphoreType.DMA((2,)), pltpu.SemaphoreType.DMA((2,))],
        ),
        out_shape=jax.ShapeDtypeStruct(xs.shape, F32),
        compiler_params=pltpu.CompilerParams(dimension_semantics=("arbitrary",), vmem_limit_bytes=VMEM_LIMIT),
        name="experts",
    )(blk0, nblk, counts, nu, w1, w3, w2, xs)


def _combine_kernel(dcur_ref, dnxt_ref, x2_ref, g_ref, w1_ref, w3_ref, w2_ref, lg_ref, lb_ref, ys_ref,
                    out_ref, buf, acc_scr, sem, *, t, nsteps):
    i = pl.program_id(0)

    def fetch(d_ref, s, n):
        for k in range(TOP_K):
            pltpu.make_async_copy(_tile_rows(ys_ref, d_ref[n * TOP_K + k]), _tile_rows(buf.at[s, k], n * SUBLANES),
                                  sem.at[s]).start(priority=k % 2)

    def drain(s):
        for k in range(TOP_K):
            pltpu.make_async_copy(_tile_rows(ys_ref, 0, t), buf.at[s, k], sem.at[s]).wait()

    @pl.when(i == 0)
    def _():
        def first(n, carry):
            fetch(dcur_ref, 0, n)
            return carry

        lax.fori_loop(0, t, first, 0)

    def step(s):
        drain(s)

        def group(gi, carry):
            r0 = pl.multiple_of(gi * SUBLANES, SUBLANES)
            for j in range(SUBLANES):
                fetch(dnxt_ref, 1 - s, r0 + j)
            gates = g_ref[pl.ds(r0, SUBLANES), :]
            gk = [jnp.broadcast_to(gates[:, k:k + 1], (SUBLANES, LANES)) for k in range(TOP_K)]
            for lb in range(SUBLANES):
                acc = None
                for k in range(TOP_K):
                    v = buf[s, k, pl.ds(r0 * SUBLANES + lb, SUBLANES, stride=SUBLANES), :]
                    acc = gk[k] * v if acc is None else acc + gk[k] * v
                acc_scr[pl.ds(r0, SUBLANES), lb * LANES:(lb + 1) * LANES] = acc
            return carry

        lax.fori_loop(0, t // SUBLANES, group, 0)

    for s in range(2):
        @pl.when(i % 2 == s)
        def _(s=s):
            step(s)

    @pl.when(i == nsteps - 1)
    def _():
        drain(nsteps % 2)

    x2 = x2_ref[...]
    acc = acc_scr[...]
    xb = x2.astype(BF16)
    a = _dot(xb, w1_ref[...])
    hs = (a * jax.nn.sigmoid(a)) * _dot(xb, w3_ref[...])
    shared = _dot(hs.astype(BF16), w2_ref[...])
    out_ref[...] = _ln(x2 * ALPHA + (acc + shared), lg_ref[...], lb_ref[...])


def _combine(dest, x2, gate_t, w1s, w3s, w2s, lg, lb, ys, *, row_off):
    n_rows, d = x2.shape
    n = dest.shape[0] // TOP_K
    t = TOKEN_TILE
    assert row_off % t == 0 and n_rows % t == 0 and n % t == 0
    off = row_off // t
    last = n // t - 1
    de = w1s.shape[-1]
    const = lambda shape: pl.BlockSpec(shape, lambda i: (0,) * len(shape))
    return pl.pallas_call(
        functools.partial(_combine_kernel, t=t, nsteps=n_rows // t),
        grid=(n_rows // t,),
        in_specs=[pl.BlockSpec((t * TOP_K,), lambda i: (off + i,), memory_space=pltpu.SMEM),
                  pl.BlockSpec((t * TOP_K,), lambda i: (jnp.minimum(off + i + 1, last),), memory_space=pltpu.SMEM),
                  pl.BlockSpec((t, d), lambda i: (i, 0)),
                  pl.BlockSpec((t, TOP_K), lambda i: (off + i, 0)),
                  const((d, de)), const((d, de)), const((de, d)), const((1, d)), const((1, d)),
                  pl.BlockSpec(memory_space=pl.ANY)],
        out_specs=pl.BlockSpec((t, d), lambda i: (i, 0)),
        out_shape=jax.ShapeDtypeStruct((n_rows, d), F32),
        scratch_shapes=[pltpu.VMEM((2, TOP_K, t * SUBLANES, LANES), F32), pltpu.VMEM((t, d), F32),
                        pltpu.SemaphoreType.DMA((2,))],
        compiler_params=pltpu.CompilerParams(dimension_semantics=("arbitrary",), vmem_limit_bytes=VMEM_LIMIT),
        name="combine",
    )(dest, dest, x2, gate_t, w1s, w3s, w2s, lg, lb, ys)


EXPERT_TM = 512
EXPERT_SPLIT = 2


def _moe_plan(cnt):
    tm = EXPERT_TM
    counts = cnt[:, 0].astype(I32)
    padded = (counts + tm - 1) // tm * tm
    pad_end = jnp.cumsum(padded)
    pad_start = pad_end - padded
    n_used = pad_end[-1] // tm
    fill_start = (pad_start + counts).astype(I32)
    fill_len = (padded - counts).astype(I32)
    return (pad_start.astype(I32), (pad_start // tm).astype(I32), (padded // tm).astype(I32), counts,
            n_used.reshape(1).astype(I32), fill_start, fill_len)


def _slots_kernel(ps_ref, e_ref, r_ref, d_ref):
    e = e_ref[...]

    def body(j, base):
        return jnp.where(e == j, ps_ref[j], base)

    base = lax.fori_loop(0, N_EXPERTS, body, jnp.zeros_like(e), unroll=8)
    d_ref[...] = (base + r_ref[...]) * SUBLANES


def _slots(pad_start, top_e, rank):
    k, n = top_e.shape
    tl = next(c for c in (2048, 1536, 1024, 512, 256, 128) if n % c == 0)
    spec = pl.BlockSpec((k, tl), lambda i, ps: (0, i))
    return pl.pallas_call(
        _slots_kernel,
        grid_spec=pltpu.PrefetchScalarGridSpec(num_scalar_prefetch=1, grid=(n // tl,), in_specs=[spec, spec],
                                               out_specs=spec),
        out_shape=jax.ShapeDtypeStruct((k, n), I32),
        compiler_params=pltpu.CompilerParams(dimension_semantics=("arbitrary",)),
        name="slots",
    )(pad_start, top_e, rank)


PAST_LEN = 1024


def kernel(x_prompt, x_sample, state_gla, cache_pool, cache_mem_k, cache_mem_v, mem_prompt, ln_in_g, ln_in_b, w_in, w_gate_up, b_gate, gla_norm_g, pool_w, pool_scale, w_out, ln1_g, ln1_b, wq_mem, wk_mem, wv_mem, wo_mem, ln2_g, ln2_b, w_router, router_bias, w1_exp, w3_exp, w2_exp, w1_sh, w3_sh, w2_sh, ln3_g, ln3_b):
    assert w_in.shape[0] == 1, "single-layer trunk"
    bp, lp, d = x_prompt.shape
    bs, ls, _ = x_sample.shape
    n_p, n_s = bp * lp, bs * ls
    n_all = n_p + n_s

    wts_a = _prep_trunk_a_weights(ln_in_g, ln_in_b, w_in[0], w_gate_up[0], b_gate[0], gla_norm_g[0], pool_w[0],
                                  pool_scale[0], w_out[0], ln1_g[0], ln1_b[0])
    s0 = jnp.zeros((bp, GLA_HEADS, GLA_DK, GLA_DV), F32)
    h0 = jnp.zeros((bp, POOL_HIST, POOL_WIDTH), F32)
    x1p, sp, hp = _trunk_a(x_prompt, s0, h0, wts_a, start_pos=0)
    x1s, ss, hs = _trunk_a(x_sample, state_gla[0], cache_pool[0], wts_a, start_pos=PAST_LEN)

    mk, mv, mkb, mvb = _mem_kv(mem_prompt.reshape(bp * N_MEM, d), wk_mem[0].astype(BF16), wv_mem[0].astype(BF16))
    wq, wo = wq_mem[0].astype(BF16), wo_mem[0].astype(BF16)
    g2, b2 = ln2_g[0].reshape(1, d), ln2_b[0].reshape(1, d)
    x2p = _attn(x1p, mkb.reshape(bp, N_MEM, d), mvb.reshape(bp, N_MEM, d), wq, wo, g2, b2)
    x2s = _attn(x1s, cache_mem_k[0].reshape(bs, N_MEM, d).astype(BF16),
                cache_mem_v[0].reshape(bs, N_MEM, d).astype(BF16), wq, wo, g2, b2)

    wrt = w_router[0].T
    wrt_h = wrt.astype(BF16)
    wrt_m = (wrt - wrt_h.astype(F32)).astype(BF16)
    top_e, gate, rank, cnt = _router(x2p, x2s, jnp.stack([wrt_h, wrt_m]), router_bias[0].reshape(N_EXPERTS, 1))

    assert (n_all * TOP_K) % EXPERT_TM == 0
    n_blk = n_all * TOP_K // EXPERT_TM + N_EXPERTS
    pad_start, blk0, nblk, counts, nu, fill_start, fill_len = _moe_plan(cnt)
    dest = _slots(pad_start, top_e, rank).T.reshape(-1)
    xs = _dispatch(fill_start, fill_len, nu, dest, x2p, x2s, n_blk=n_blk, tm=EXPERT_TM)
    ys = _experts(blk0, nblk, counts, nu, xs, w1_exp[0], w3_exp[0], w2_exp[0], tm=EXPERT_TM)
    sh = (w1_sh[0].astype(BF16), w3_sh[0].astype(BF16), w2_sh[0].astype(BF16),
          ln3_g[0].reshape(1, d), ln3_b[0].reshape(1, d))
    gate_t = gate.T
    yp = _combine(dest, x2p, gate_t, *sh, ys, row_off=0)
    ysm = _combine(dest, x2s, gate_t, *sh, ys, row_off=n_p)

    return (yp.reshape(bp, lp, d), ysm.reshape(bs, ls, d), sp[None], hp[None],
            mk.reshape(1, bp, N_MEM, MEM_HEADS, MEM_DH), mv.reshape(1, bp, N_MEM, MEM_HEADS, MEM_DH),
            ss[None], hs[None])
```

```python
import functools

import jax
import jax.numpy as jnp
from jax import lax
from jax.experimental import pallas as pl
from jax.experimental.pallas import tpu as pltpu

F32 = jnp.float32
BF16 = jnp.bfloat16
I32 = jnp.int32
U32 = jnp.uint32

D_MODEL = 1024
CHUNK = 64
SUB = 16
GLA_HEADS = 4
GLA_DK = 64
GLA_DV = 128
GLA_KEY = GLA_HEADS * GLA_DK
GLA_WIDTH = GLA_HEADS * GLA_DV
GATE_RANK = 16
GATE_PAD = 128
POOL_WIDTH = 512
POOL_WINDOWS = (2, 4, 8, 16)
POOL_GC = 128
POOL_HIST = 15
HIST_ROWS = 16
N_MEM = 256
MEM_HEADS = 4
MEM_DH = 256
N_EXPERTS = 256
N_GROUPS = 8
EXPERTS_PER_GROUP = 32
TOPK_GROUPS = 4
TOP_K = 8
ROUTED_SCALE = 2.5
D_EXPERT = 256
TOKEN_TILE = 256
SUBLANES, LANES = 8, 128
assert D_MODEL == SUBLANES * LANES
ALPHA = 2.0 ** 0.25
EPS = 1e-5
OFF_Q, OFF_K, OFF_V, OFF_G, OFF_U, OFF_GD = 0, 256, 512, 1024, 1536, 2048
D_IN_PAD = OFF_GD + GATE_PAD

VMEM_LIMIT = 56 * 1024 * 1024


def _ln(x, g, b):
    mu = jnp.mean(x, axis=-1, keepdims=True)
    xc = x - mu
    var = jnp.mean(xc * xc, axis=-1, keepdims=True)
    return xc * lax.rsqrt(var + EPS) * g + b


def _dot(a, b):
    return jnp.dot(a, b, preferred_element_type=F32)


def _dot_nt(a, b):
    return lax.dot_general(a, b, (((1,), (1,)), ((), ())), preferred_element_type=F32)


def _dot_tn(a, b):
    return lax.dot_general(a, b, (((0,), (0,)), ((), ())), preferred_element_type=F32)


def _split3(x):
    h = x.astype(BF16)
    r = x - h.astype(F32)
    m = r.astype(BF16)
    l = (r - m.astype(F32)).astype(BF16)
    return h, m, l


def _trunk_a_kernel(x_ref, s0_ref, h0_ref, lng_ref, lnb_ref, win_ref, wgu_ref, bg_ref, gng_ref,
                    pw_ref, ps_ref, wout_ref, l1g_ref, l1b_ref,
                    x1_ref, sn_ref, hn_ref,
                    proj_scr, ext_scr, s_scr, op_scr, *, tl, chunk, start_pos):
    t = pl.program_id(1)
    nt = pl.num_programs(1)

    @pl.when(t == 0)
    def _():
        s_scr[...] = s0_ref[0]
        ext_scr[0:1, :] = jnp.zeros((1, POOL_WIDTH), F32)
        ext_scr[1:HIST_ROWS, :] = h0_ref[0]

    xn = _ln(x_ref[0], lng_ref[...], lnb_ref[...])
    proj_scr[...] = _dot(xn.astype(BF16), win_ref[...])

    c = chunk
    shift = lambda a, n: lax.shift_right_logical(a, n.bit_length() - 1)
    gd = proj_scr[:, OFF_GD:OFF_GD + GATE_PAD]
    z = _dot(gd.astype(BF16), wgu_ref[...]) + bg_ref[...]
    lf = (jnp.minimum(z, 0.0) - jnp.log1p(jnp.exp(-jnp.abs(z)))) * (1.0 / 16.0)
    trow = lax.broadcasted_iota(I32, (tl, tl), 0)
    tcol = lax.broadcasted_iota(I32, (tl, tl), 1)
    tri = jnp.logical_and(tcol <= trow, shift(tcol, c) == shift(trow, c)).astype(BF16)
    lh, lm, ll = _split3(lf)
    cum = _dot(tri, lh) + _dot(tri, lm) + _dot(tri, ll)
    q_all = proj_scr[:, OFF_Q:OFF_Q + GLA_KEY] * (GLA_DK ** -0.5)
    k_all = proj_scr[:, OFF_K:OFF_K + GLA_KEY]
    qs_all = q_all * jnp.exp(cum)

    causal = lax.broadcasted_iota(I32, (c, c), 1) <= lax.broadcasted_iota(I32, (c, c), 0)
    eye_dk = lax.broadcasted_iota(I32, (GLA_DK, GLA_DK), 0) == lax.broadcasted_iota(I32, (GLA_DK, GLA_DK), 1)
    n_sub = c // SUB
    rblk = shift(lax.broadcasted_iota(I32, (c, n_sub * GLA_DK), 0), SUB)
    lblk = shift(lax.broadcasted_iota(I32, (c, n_sub * GLA_DK), 1), GLA_DK)
    lblk_row = shift(lax.broadcasted_iota(I32, (1, n_sub * GLA_DK), 1), GLA_DK)
    mask_q = rblk == lblk
    mask_k = rblk <= lblk

    def tile_lanes(a):
        return jnp.concatenate([a] * n_sub, axis=1)

    states = [s_scr[h] for h in range(GLA_HEADS)]
    for ci in range(tl // c):
        rs = slice(ci * c, (ci + 1) * c)
        cum_c = cum[rs]
        last = cum_c[c - 1:c, :]
        ks_c = k_all[rs] * jnp.exp(last - cum_c)
        for h in range(GLA_HEADS):
            ksl = slice(h * GLA_DK, (h + 1) * GLA_DK)
            vsl = slice(h * GLA_DV, (h + 1) * GLA_DV)
            v_h = proj_scr[rs, OFF_V + h * GLA_DV:OFF_V + (h + 1) * GLA_DV].astype(BF16)
            cum_t = tile_lanes(cum_c[:, ksl])
            q_t = tile_lanes(q_all[rs, ksl])
            k_t = tile_lanes(k_all[rs, ksl])
            ref_row = jnp.zeros((1, n_sub * GLA_DK), F32)
            for i in range(1, n_sub):
                ref_row = jnp.where(lblk_row == i, cum_t[i * SUB - 1:i * SUB, :], ref_row)
            arg = cum_t - ref_row
            lhs = jnp.where(mask_q, q_t * jnp.exp(jnp.where(mask_q, arg, 0.0)), 0.0)
            rhs = jnp.where(mask_k, k_t * jnp.exp(jnp.where(mask_k, -arg, 0.0)), 0.0)
            att = jnp.where(causal, _dot_nt(lhs.astype(BF16), rhs.astype(BF16)), 0.0)
            s_h = states[h]
            o_h = _dot(att.astype(BF16), v_h) + _dot(qs_all[rs, ksl].astype(BF16), s_h.astype(BF16))
            dcol = jnp.sum(jnp.where(eye_dk, jnp.broadcast_to(jnp.exp(last[:, ksl]), (GLA_DK, GLA_DK)), 0.0),
                           axis=1, keepdims=True)
            states[h] = dcol * s_h + _dot_tn(ks_c[:, ksl].astype(BF16), v_h)
            o_h = o_h * lax.rsqrt(jnp.mean(o_h * o_h, axis=-1, keepdims=True) + EPS) * gng_ref[...]
            g_h = proj_scr[rs, OFF_G + h * GLA_DV:OFF_G + (h + 1) * GLA_DV]
            op_scr[rs, vsl] = (o_h * (g_h * jax.nn.sigmoid(g_h))).astype(BF16)
    for h in range(GLA_HEADS):
        s_scr[h] = states[h]

    u = proj_scr[:, OFF_U:OFF_U + POOL_WIDTH]
    ext_scr[HIST_ROWS:HIST_ROWS + tl, :] = u
    n_valid = start_pos + t * tl + lax.broadcasted_iota(I32, (tl, 1), 0) + 1
    for gi, w in enumerate(POOL_WINDOWS):
        lsl = slice(gi * POOL_GC, (gi + 1) * POOL_GC)
        win = ext_scr[HIST_ROWS:HIST_ROWS + tl, lsl]
        for s in range(1, w):
            win = win + ext_scr[HIST_ROWS - s:HIST_ROWS - s + tl, lsl]
        cnt = jnp.minimum(w, n_valid).astype(F32)
        r = win / cnt - u[:, lsl]
        p = _dot(r.astype(BF16), pw_ref[gi]) * ps_ref[:, lsl]
        op_scr[:, GLA_WIDTH + gi * POOL_GC:GLA_WIDTH + (gi + 1) * POOL_GC] = p.astype(BF16)
    tail = ext_scr[tl:tl + HIST_ROWS, :]
    ext_scr[0:HIST_ROWS, :] = tail

    mix = _dot(op_scr[...], wout_ref[...])
    x1_ref[0] = _ln(xn * ALPHA + mix, l1g_ref[...], l1b_ref[...])

    @pl.when(t == nt - 1)
    def _():
        sn_ref[0] = s_scr[...]
        hn_ref[0] = ext_scr[1:HIST_ROWS, :]


def _trunk_a(x, s0, h0, wts, *, start_pos):
    b, l, d = x.shape
    tl = min(l, 256)
    chunk = min(tl, CHUNK)
    assert l % tl == 0 and tl % chunk == 0 and chunk % SUB == 0 and l >= HIST_ROWS
    nt = l // tl
    kern = functools.partial(_trunk_a_kernel, tl=tl, chunk=chunk, start_pos=start_pos)
    const = lambda shape: pl.BlockSpec(shape, lambda bi, ti: (0,) * len(shape))
    return pl.pallas_call(
        kern,
        grid=(b, nt),
        in_specs=[
            pl.BlockSpec((1, tl, d), lambda bi, ti: (bi, ti, 0)),
            pl.BlockSpec((1, GLA_HEADS, GLA_DK, GLA_DV), lambda bi, ti: (bi, 0, 0, 0)),
            pl.BlockSpec((1, POOL_HIST, POOL_WIDTH), lambda bi, ti: (bi, 0, 0)),
            const((1, d)), const((1, d)),
            const((d, D_IN_PAD)), const((GATE_PAD, GLA_KEY)), const((1, GLA_KEY)), const((1, GLA_DV)),
            const((len(POOL_WINDOWS), POOL_GC, POOL_GC)), const((1, POOL_WIDTH)),
            const((GLA_WIDTH + POOL_WIDTH, d)), const((1, d)), const((1, d)),
        ],
        out_specs=[
            pl.BlockSpec((1, tl, d), lambda bi, ti: (bi, ti, 0)),
            pl.BlockSpec((1, GLA_HEADS, GLA_DK, GLA_DV), lambda bi, ti: (bi, 0, 0, 0)),
            pl.BlockSpec((1, POOL_HIST, POOL_WIDTH), lambda bi, ti: (bi, 0, 0)),
        ],
        out_shape=[
            jax.ShapeDtypeStruct((b, l, d), F32),
            jax.ShapeDtypeStruct((b, GLA_HEADS, GLA_DK, GLA_DV), F32),
            jax.ShapeDtypeStruct((b, POOL_HIST, POOL_WIDTH), F32),
        ],
        scratch_shapes=[
            pltpu.VMEM((tl, D_IN_PAD), F32),
            pltpu.VMEM((HIST_ROWS + tl, POOL_WIDTH), F32),
            pltpu.VMEM((GLA_HEADS, GLA_DK, GLA_DV), F32),
            pltpu.VMEM((tl, GLA_WIDTH + POOL_WIDTH), BF16),
        ],
        compiler_params=pltpu.CompilerParams(
            dimension_semantics=("arbitrary", "arbitrary"), vmem_limit_bytes=VMEM_LIMIT),
        name="trunk_a",
    )(x, s0, h0, *wts)


def _prep_trunk_a_weights(ln_in_g, ln_in_b, w_in, w_gate_up, b_gate, gla_norm_g, pool_w, pool_scale, w_out,
                          ln1_g, ln1_b):
    d = D_MODEL
    p_gd = 2 * GLA_KEY + 2 * GLA_WIDTH
    w_in_r = jnp.concatenate(
        [w_in[:, :p_gd], w_in[:, p_gd + GATE_RANK:], w_in[:, p_gd:p_gd + GATE_RANK],
         jnp.zeros((d, GATE_PAD - GATE_RANK), w_in.dtype)], axis=1).astype(BF16)
    wgu = jnp.concatenate([w_gate_up, jnp.zeros((GATE_PAD - GATE_RANK, GLA_KEY), w_gate_up.dtype)],
                          axis=0).astype(BF16)
    return (ln_in_g.reshape(1, d), ln_in_b.reshape(1, d), w_in_r, wgu, b_gate.reshape(1, GLA_KEY),
            gla_norm_g.reshape(1, GLA_DV), pool_w.astype(BF16), pool_scale.reshape(1, POOL_WIDTH),
            w_out.astype(BF16), ln1_g.reshape(1, d), ln1_b.reshape(1, d))


def _mem_kv_kernel(m_ref, wk_ref, wv_ref, k_ref, v_ref, kb_ref, vb_ref):
    m = m_ref[...].astype(BF16)
    k = _dot(m, wk_ref[...])
    v = _dot(m, wv_ref[...])
    k_ref[...] = k
    v_ref[...] = v
    kb_ref[...] = k.astype(BF16)
    vb_ref[...] = v.astype(BF16)


def _mem_kv(mem, wk, wv):
    m, d = mem.shape
    tm = min(m, 512)
    assert m % tm == 0
    row = pl.BlockSpec((tm, d), lambda i: (i, 0))
    wspec = pl.BlockSpec((d, d), lambda i: (0, 0))
    return pl.pallas_call(
        _mem_kv_kernel,
        grid=(m // tm,),
        in_specs=[row, wspec, wspec],
        out_specs=[row, row, row, row],
        out_shape=[jax.ShapeDtypeStruct((m, d), F32), jax.ShapeDtypeStruct((m, d), F32),
                   jax.ShapeDtypeStruct((m, d), BF16), jax.ShapeDtypeStruct((m, d), BF16)],
        compiler_params=pltpu.CompilerParams(dimension_semantics=("arbitrary",), vmem_limit_bytes=VMEM_LIMIT),
        name="mem_kv",
    )(mem, wk, wv)


def _attn_kernel(x1_ref, k_ref, v_ref, wq_ref, wo_ref, g_ref, b_ref, x2_ref, o_scr):
    x1 = x1_ref[0]
    q = (_dot(x1.astype(BF16), wq_ref[...]) * (MEM_DH ** -0.5)).astype(BF16)
    for h in range(MEM_HEADS):
        hs = slice(h * MEM_DH, (h + 1) * MEM_DH)
        s = _dot_nt(q[:, hs], k_ref[0, :, hs])
        e = jnp.exp(s - jnp.max(s, axis=-1, keepdims=True))
        p = e / jnp.sum(e, axis=-1, keepdims=True)
        o_scr[:, hs] = _dot(p.astype(BF16), v_ref[0, :, hs]).astype(BF16)
    attn = _dot(o_scr[...], wo_ref[...])
    x2_ref[...] = _ln(x1 * ALPHA + attn, g_ref[...], b_ref[...])


def _attn(x1, mem_k, mem_v, wq, wo, g, b):
    bsz, l, d = x1.shape
    tl = min(l, 256)
    assert l % tl == 0
    nt = l // tl
    const = lambda shape: pl.BlockSpec(shape, lambda bi, ti: (0,) * len(shape))
    return pl.pallas_call(
        _attn_kernel,
        grid=(bsz, nt),
        in_specs=[
            pl.BlockSpec((1, tl, d), lambda bi, ti: (bi, ti, 0)),
            pl.BlockSpec((1, N_MEM, d), lambda bi, ti: (bi, 0, 0)),
            pl.BlockSpec((1, N_MEM, d), lambda bi, ti: (bi, 0, 0)),
            const((d, d)), const((d, d)), const((1, d)), const((1, d)),
        ],
        out_specs=pl.BlockSpec((tl, d), lambda bi, ti: (bi * nt + ti, 0)),
        out_shape=jax.ShapeDtypeStruct((bsz * l, d), F32),
        scratch_shapes=[pltpu.VMEM((tl, d), BF16)],
        compiler_params=pltpu.CompilerParams(
            dimension_semantics=("arbitrary", "arbitrary"), vmem_limit_bytes=VMEM_LIMIT),
        name="attn",
    )(x1, mem_k, mem_v, wq, wo, g, b)


def _two_part_specs(t, d, tiles_a):
    spec_a = pl.BlockSpec((t, d), lambda i, *_: (jnp.minimum(i, tiles_a - 1), 0))
    spec_b = pl.BlockSpec((t, d), lambda i, *_: (jnp.maximum(i - tiles_a, 0), 0))
    return spec_a, spec_b


def _router_kernel(xa_ref, xb_ref, wrt_ref, bias_ref, e_ref, g_ref, r_ref, cnt_ref, cnt_scr, *, tl, tiles_a):
    i = pl.program_id(0)

    @pl.when(i == 0)
    def _():
        cnt_scr[...] = jnp.zeros_like(cnt_scr)

    x = jnp.where(i < tiles_a, xa_ref[...], xb_ref[...])
    xh = x.astype(BF16)
    xm = (x - xh.astype(F32)).astype(BF16)
    wh = wrt_ref[0]
    wm = wrt_ref[1]
    logits = _dot_nt(wh, xh) + (_dot_nt(wh, xm) + _dot_nt(wm, xh))
    scores = jax.nn.sigmoid(logits)
    biased = scores + bias_ref[...]
    ninf = -jnp.inf
    eg = EXPERTS_PER_GROUP
    riota = lax.broadcasted_iota(I32, (eg, tl), 0)
    gs_rows = []
    for g in range(N_GROUPS):
        blk = biased[g * eg:(g + 1) * eg, :]
        m1 = jnp.max(blk, axis=0, keepdims=True)
        i1 = jnp.min(jnp.where(blk == m1, riota, eg), axis=0, keepdims=True)
        m2 = jnp.max(jnp.where(riota == i1, ninf, blk), axis=0, keepdims=True)
        gs_rows.append(m1 + m2)
    gs = jnp.concatenate(gs_rows, axis=0)
    giota = lax.broadcasted_iota(I32, (N_GROUPS, tl), 0)
    sel = jnp.zeros((N_GROUPS, tl), jnp.bool_)
    for _ in range(TOPK_GROUPS):
        m = jnp.max(gs, axis=0, keepdims=True)
        gi = jnp.min(jnp.where(gs == m, giota, N_GROUPS), axis=0, keepdims=True)
        hit = giota == gi
        sel = jnp.logical_or(sel, hit)
        gs = jnp.where(hit, ninf, gs)
    self = jnp.where(sel, 1.0, 0.0)
    masked = jnp.concatenate(
        [jnp.where(self[g:g + 1, :] > 0.5, biased[g * eg:(g + 1) * eg, :], ninf) for g in range(N_GROUPS)], axis=0)
    eiota = lax.broadcasted_iota(I32, (N_EXPERTS, tl), 0)
    idx_rows, sc_rows = [], []
    multi = jnp.zeros((N_EXPERTS, tl), F32)
    for _ in range(TOP_K):
        m = jnp.max(masked, axis=0, keepdims=True)
        idx = jnp.min(jnp.where(masked == m, eiota, N_EXPERTS), axis=0, keepdims=True)
        hit = eiota == idx
        sc_rows.append(jnp.sum(jnp.where(hit, scores, 0.0), axis=0, keepdims=True))
        idx_rows.append(idx)
        multi = jnp.where(hit, 1.0, multi)
        masked = jnp.where(hit, ninf, masked)
    top_e = jnp.concatenate(idx_rows, axis=0)
    sc = jnp.concatenate(sc_rows, axis=0)
    e_ref[...] = top_e
    g_ref[...] = sc / jnp.sum(sc, axis=0, keepdims=True) * ROUTED_SCALE
    mh = multi.astype(BF16)
    before = (lax.broadcasted_iota(I32, (tl, tl), 0) < lax.broadcasted_iota(I32, (tl, tl), 1)).astype(BF16)
    running = cnt_scr[...]
    rankmat = _dot(mh, before) + jnp.concatenate([running] * (tl // 128), axis=1)
    r_rows = [jnp.sum(jnp.where(eiota == idx_rows[k], rankmat, 0.0), axis=0, keepdims=True) for k in range(TOP_K)]
    r_ref[...] = jnp.concatenate(r_rows, axis=0).astype(I32)
    total = running + _dot(mh, jnp.ones((tl, 128), BF16))
    cnt_scr[...] = total
    cnt_ref[...] = total


def _router(x2a, x2b, wrt, bias_col):
    d = x2a.shape[1]
    tl = TOKEN_TILE
    assert x2a.shape[0] % tl == 0 and x2b.shape[0] % tl == 0
    tiles_a = x2a.shape[0] // tl
    n = x2a.shape[0] + x2b.shape[0]
    kspec = pl.BlockSpec((TOP_K, tl), lambda i: (0, i))
    return pl.pallas_call(
        functools.partial(_router_kernel, tl=tl, tiles_a=tiles_a),
        grid=(n // tl,),
        in_specs=[*_two_part_specs(tl, d, tiles_a),
                  pl.BlockSpec((2, N_EXPERTS, d), lambda i: (0, 0, 0)),
                  pl.BlockSpec((N_EXPERTS, 1), lambda i: (0, 0))],
        out_specs=[kspec, kspec, kspec, pl.BlockSpec((N_EXPERTS, 128), lambda i: (0, 0))],
        out_shape=[jax.ShapeDtypeStruct((TOP_K, n), I32), jax.ShapeDtypeStruct((TOP_K, n), F32),
                   jax.ShapeDtypeStruct((TOP_K, n), I32), jax.ShapeDtypeStruct((N_EXPERTS, 128), F32)],
        scratch_shapes=[pltpu.VMEM((N_EXPERTS, 128), F32)],
        compiler_params=pltpu.CompilerParams(dimension_semantics=("arbitrary",), vmem_limit_bytes=VMEM_LIMIT),
        name="router",
    )(x2a, x2b, wrt, bias_col)


PACK_ROWS = D_MODEL // 2 // LANES
HI_MASK = 0xFFFF0000


def _slots_of(buf_ref, start, n=None):
    if n is None:
        return buf_ref.at[start, pl.ds(0, PACK_ROWS)]
    return buf_ref.at[pl.ds(start, n), pl.ds(0, PACK_ROWS)]


def _pack_rows(x, dst_ref, row0=0):
    t = x.shape[0]
    bits = lax.bitcast_convert_type(x.astype(BF16).astype(F32), U32)
    flat = dst_ref.reshape(dst_ref.shape[0] * SUBLANES, LANES)
    for j in range(PACK_ROWS):
        lo = lax.shift_right_logical(bits[:, j * LANES:(j + 1) * LANES], jnp.uint32(16))
        hi = bits[:, D_MODEL // 2 + j * LANES:D_MODEL // 2 + (j + 1) * LANES] & jnp.uint32(HI_MASK)
        flat[pl.ds(row0 * SUBLANES + j, t, stride=SUBLANES), :] = hi | lo


def _unpack_words(w):
    lo = lax.bitcast_convert_type(lax.shift_left(w, jnp.uint32(16)), F32)
    hi = lax.bitcast_convert_type(w & jnp.uint32(HI_MASK), F32)
    return lo, hi


def _unpack_rows(src_ref, t, row0=0):
    flat = src_ref.reshape(src_ref.shape[0] * SUBLANES, LANES)
    parts = [_unpack_words(flat[pl.ds(row0 * SUBLANES + j, t, stride=SUBLANES), :]) for j in range(PACK_ROWS)]
    return jnp.concatenate([p[0] for p in parts] + [p[1] for p in parts], axis=1)


def _dispatch_kernel(fs_ref, fl_ref, nu_ref, dest_ref, xa_ref, xb_ref, xs_ref, stage, zbuf, sem, zsem,
                     *, t, tm, tiles_a, n_blk):
    i = pl.program_id(0)
    nsteps = pl.num_programs(0)

    def zero_fill(start):
        def go(cp):
            if start:
                cp.start()
            else:
                cp.wait()

        def per_expert(e, carry):
            ln = fl_ref[e]
            off = fs_ref[e]
            bit = tm // 2
            while bit:
                @pl.when((ln & bit) != 0)
                def _(off=off, bit=bit):
                    go(pltpu.make_async_copy(_slots_of(zbuf, 0, bit), xs_ref.at[pl.ds(off, bit)], zsem))
                off = off + (ln & bit)
                bit //= 2
            return carry

        lax.fori_loop(0, N_EXPERTS, per_expert, 0)

        def per_block(b, carry):
            go(pltpu.make_async_copy(_slots_of(zbuf, 0, tm), xs_ref.at[pl.ds(b * tm, tm)], zsem))
            return carry

        lax.fori_loop(nu_ref[0], n_blk, per_block, 0)

    @pl.when(i == 0)
    def _():
        zbuf[...] = jnp.zeros_like(zbuf)
        zero_fill(True)

    def drain(slot):
        for k in range(TOP_K):
            pltpu.make_async_copy(_slots_of(stage.at[slot], 0, t), xs_ref.at[pl.ds(0, t)], sem.at[slot]).wait()

    def step(slot):
        @pl.when(i >= 2)
        def _():
            drain(slot)

        _pack_rows(jnp.where(i < tiles_a, xa_ref[...], xb_ref[...]), stage.at[slot])

        def body(n, carry):
            src = _slots_of(stage.at[slot], n)
            for k in range(TOP_K):
                dst = xs_ref.at[dest_ref[n * TOP_K + k]]
                pltpu.make_async_copy(src, dst, sem.at[slot]).start(priority=k % 2)
            return carry

        lax.fori_loop(0, t, body, 0)

    for slot in range(2):
        @pl.when(i % 2 == slot)
        def _(slot=slot):
            step(slot)

    @pl.when(i == 0)
    def _():
        zero_fill(False)

    @pl.when(i == nsteps - 1)
    def _():
        for slot in range(2):
            @pl.when(jnp.logical_or(nsteps >= 2, (nsteps - 1) % 2 == slot))
            def _(slot=slot):
                drain(slot)


def _dispatch(fill_start, fill_len, nu, dest, x2a, x2b, *, n_blk, tm):
    d = x2a.shape[1]
    t = TOKEN_TILE
    assert x2a.shape[0] % t == 0 and x2b.shape[0] % t == 0
    tiles_a = x2a.shape[0] // t
    n = x2a.shape[0] + x2b.shape[0]
    return pl.pallas_call(
        functools.partial(_dispatch_kernel, t=t, tm=tm, tiles_a=tiles_a, n_blk=n_blk),
        grid_spec=pltpu.PrefetchScalarGridSpec(
            num_scalar_prefetch=3,
            grid=(n // t,),
            in_specs=[pl.BlockSpec((t * TOP_K,), lambda i, *_: (i,), memory_space=pltpu.SMEM),
                      *_two_part_specs(t, d, tiles_a)],
            out_specs=pl.BlockSpec(memory_space=pl.ANY),
            scratch_shapes=[pltpu.VMEM((2, t, SUBLANES, LANES), U32), pltpu.VMEM((tm, SUBLANES, LANES), U32),
                            pltpu.SemaphoreType.DMA((2,)), pltpu.SemaphoreType.DMA(())],
        ),
        out_shape=jax.ShapeDtypeStruct((n_blk * tm, PACK_ROWS, LANES), U32),
        compiler_params=pltpu.CompilerParams(dimension_semantics=("arbitrary",), vmem_limit_bytes=VMEM_LIMIT),
        name="dispatch",
    )(fill_start, fill_len, nu, dest, x2a, x2b)


def _experts_kernel(b0_ref, nb_ref, cnt_ref, nu_ref, w1_ref, w3_ref, w2_ref, xs_ref, ys_ref,
                    w1b, w3b, w2b, xbuf, ybuf, in_sem, out_sem, *, tm, n_blk):
    e = pl.program_id(0)
    n_used = nu_ref[0]

    def x_copy(b):
        return pltpu.make_async_copy(xs_ref.at[pl.ds(b * tm, tm)], _slots_of(xbuf.at[b % 2], 0, tm), in_sem.at[b % 2])

    def y_copy(b):
        return pltpu.make_async_copy(_slots_of(ybuf.at[b % 2], 0, tm), ys_ref.at[pl.ds(b * tm, tm)], out_sem.at[b % 2])

    @pl.when(e == 0)
    def _():
        x_copy(0).start()

    w1b[...] = w1_ref[0].astype(BF16)
    w3b[...] = w3_ref[0].astype(BF16)
    w2b[...] = w2_ref[0].astype(BF16)
    b0 = b0_ref[e]
    cnt = cnt_ref[e]

    def block(j, carry):
        b = b0 + j
        x_copy(b).wait()

        @pl.when(b + 1 < n_used)
        def _():
            x_copy(b + 1).start()

        @pl.when(b >= 2)
        def _():
            y_copy(b - 2).wait()

        rows = tm // EXPERT_SPLIT
        for r0 in range(0, tm, rows):
            live = lax.broadcasted_iota(I32, (rows, 1), 0) < cnt - j * tm - r0
            x = jnp.where(live, _unpack_rows(xbuf.at[b % 2], rows, r0), 0.0).astype(BF16)
            a = _dot(x, w1b[...])
            h = (a * jax.nn.sigmoid(a)) * _dot(x, w3b[...])
            _pack_rows(_dot(h.astype(BF16), w2b[...]), ybuf.at[b % 2], r0)
        y_copy(b).start()
        return carry

    lax.fori_loop(0, nb_ref[e], block, 0)

    @pl.when(e == pl.num_programs(0) - 1)
    def _():
        @pl.when(n_used >= 2)
        def _():
            y_copy(n_used - 2).wait()

        y_copy(n_used - 1).wait()
        ybuf[0] = jnp.zeros(ybuf.shape[1:], U32)

        def zero_copy(b):
            return pltpu.make_async_copy(_slots_of(ybuf.at[0], 0, tm), ys_ref.at[pl.ds(b * tm, tm)], out_sem.at[0])

        def zero(b, carry):
            zero_copy(b).start()
            return carry

        lax.fori_loop(n_used, n_blk, zero, 0)

        def zero_wait(b, carry):
            zero_copy(b).wait()
            return carry

        lax.fori_loop(n_used, n_blk, zero_wait, 0)


def _experts(blk0, nblk, counts, nu, xs, w1, w3, w2, *, tm):
    n_blk = xs.shape[0] // tm
    n_exp, d, de = w1.shape
    wmap = lambda e, *_: (e, 0, 0)
    return pl.pallas_call(
        functools.partial(_experts_kernel, tm=tm, n_blk=n_blk),
        grid_spec=pltpu.PrefetchScalarGridSpec(
            num_scalar_prefetch=4,
            grid=(n_exp,),
            in_specs=[pl.BlockSpec((1, d, de), wmap), pl.BlockSpec((1, d, de), wmap), pl.BlockSpec((1, de, d), wmap),
                      pl.BlockSpec(memory_space=pl.ANY)],
            out_specs=pl.BlockSpec(memory_space=pl.ANY),
            scratch_shapes=[pltpu.VMEM((d, de), BF16), pltpu.VMEM((d, de), BF16), pltpu.VMEM((de, d), BF16),
                            pltpu.VMEM((2, tm, SUBLANES, LANES), U32), pltpu.VMEM((2, tm, SUBLANES, LANES), U32),
                            pltpu.SemaphoreType.DMA((2,)), pltpu.SemaphoreType.DMA((2,))],
        ),
        out_shape=jax.ShapeDtypeStruct(xs.shape, U32),
        compiler_params=pltpu.CompilerParams(dimension_semantics=("arbitrary",), vmem_limit_bytes=VMEM_LIMIT),
        name="experts",
    )(blk0, nblk, counts, nu, w1, w3, w2, xs)


def _combine_kernel(dcur_ref, dnxt_ref, x2_ref, g_ref, w1_ref, w3_ref, w2_ref, lg_ref, lb_ref, ys_ref,
                    out_ref, buf, acc_scr, sem, *, t, nsteps):
    i = pl.program_id(0)

    def fetch(d_ref, s, n):
        for k in range(TOP_K):
            pltpu.make_async_copy(ys_ref.at[d_ref[n * TOP_K + k]], _slots_of(buf.at[s, k], n),
                                  sem.at[s]).start(priority=k % 2)

    def drain(s):
        for k in range(TOP_K):
            pltpu.make_async_copy(ys_ref.at[pl.ds(0, t)], _slots_of(buf.at[s, k], 0, t), sem.at[s]).wait()

    @pl.when(i == 0)
    def _():
        def first(n, carry):
            fetch(dcur_ref, 0, n)
            return carry

        lax.fori_loop(0, t, first, 0)

    def step(s):
        drain(s)

        def group(gi, carry):
            r0 = pl.multiple_of(gi * SUBLANES, SUBLANES)
            for j in range(SUBLANES):
                fetch(dnxt_ref, 1 - s, r0 + j)
            gates = g_ref[pl.ds(r0, SUBLANES), :]
            gk = [jnp.broadcast_to(gates[:, k:k + 1], (SUBLANES, LANES)) for k in range(TOP_K)]
            planes = [buf.at[s, k].reshape(t * SUBLANES, LANES) for k in range(TOP_K)]
            for j in range(PACK_ROWS):
                acc_lo = acc_hi = None
                for k in range(TOP_K):
                    lo, hi = _unpack_words(planes[k][pl.ds(r0 * SUBLANES + j, SUBLANES, stride=SUBLANES), :])
                    acc_lo = gk[k] * lo if acc_lo is None else acc_lo + gk[k] * lo
                    acc_hi = gk[k] * hi if acc_hi is None else acc_hi + gk[k] * hi
                acc_scr[pl.ds(r0, SUBLANES), j * LANES:(j + 1) * LANES] = acc_lo
                acc_scr[pl.ds(r0, SUBLANES), D_MODEL // 2 + j * LANES:D_MODEL // 2 + (j + 1) * LANES] = acc_hi
            return carry

        lax.fori_loop(0, t // SUBLANES, group, 0)

    for s in range(2):
        @pl.when(i % 2 == s)
        def _(s=s):
            step(s)

    @pl.when(i == nsteps - 1)
    def _():
        drain(nsteps % 2)

    x2 = x2_ref[...]
    acc = acc_scr[...]
    xb = x2.astype(BF16)
    a = _dot(xb, w1_ref[...])
    hs = (a * jax.nn.sigmoid(a)) * _dot(xb, w3_ref[...])
    shared = _dot(hs.astype(BF16), w2_ref[...])
    out_ref[...] = _ln(x2 * ALPHA + (acc + shared), lg_ref[...], lb_ref[...])


def _combine(dest, x2, gate_t, w1s, w3s, w2s, lg, lb, ys, *, row_off):
    n_rows, d = x2.shape
    n = dest.shape[0] // TOP_K
    t = TOKEN_TILE
    assert row_off % t == 0 and n_rows % t == 0 and n % t == 0
    off = row_off // t
    last = n // t - 1
    de = w1s.shape[-1]
    const = lambda shape: pl.BlockSpec(shape, lambda i: (0,) * len(shape))
    return pl.pallas_call(
        functools.partial(_combine_kernel, t=t, nsteps=n_rows // t),
        grid=(n_rows // t,),
        in_specs=[pl.BlockSpec((t * TOP_K,), lambda i: (off + i,), memory_space=pltpu.SMEM),
                  pl.BlockSpec((t * TOP_K,), lambda i: (jnp.minimum(off + i + 1, last),), memory_space=pltpu.SMEM),
                  pl.BlockSpec((t, d), lambda i: (i, 0)),
                  pl.BlockSpec((t, TOP_K), lambda i: (off + i, 0)),
                  const((d, de)), const((d, de)), const((de, d)), const((1, d)), const((1, d)),
                  pl.BlockSpec(memory_space=pl.ANY)],
        out_specs=pl.BlockSpec((t, d), lambda i: (i, 0)),
        out_shape=jax.ShapeDtypeStruct((n_rows, d), F32),
        scratch_shapes=[pltpu.VMEM((2, TOP_K, t, SUBLANES, LANES), U32), pltpu.VMEM((t, d), F32),
                        pltpu.SemaphoreType.DMA((2,))],
        compiler_params=pltpu.CompilerParams(dimension_semantics=("arbitrary",), vmem_limit_bytes=VMEM_LIMIT),
        name="combine",
    )(dest, dest, x2, gate_t, w1s, w3s, w2s, lg, lb, ys)


EXPERT_TM = 512
EXPERT_SPLIT = 2


def _moe_plan(cnt):
    tm = EXPERT_TM
    counts = cnt[:, 0].astype(I32)
    padded = (counts + tm - 1) // tm * tm
    pad_end = jnp.cumsum(padded)
    pad_start = pad_end - padded
    n_used = pad_end[-1] // tm
    fill_start = (pad_start + counts).astype(I32)
    fill_len = (padded - counts).astype(I32)
    return (pad_start.astype(I32), (pad_start // tm).astype(I32), (padded // tm).astype(I32), counts,
            n_used.reshape(1).astype(I32), fill_start, fill_len)


def _slots_kernel(ps_ref, e_ref, r_ref, d_ref):
    e = e_ref[...]

    def body(j, base):
        return jnp.where(e == j, ps_ref[j], base)

    base = lax.fori_loop(0, N_EXPERTS, body, jnp.zeros_like(e), unroll=8)
    d_ref[...] = base + r_ref[...]


def _slots(pad_start, top_e, rank):
    k, n = top_e.shape
    tl = next(c for c in (2048, 1536, 1024, 512, 256, 128) if n % c == 0)
    spec = pl.BlockSpec((k, tl), lambda i, ps: (0, i))
    return pl.pallas_call(
        _slots_kernel,
        grid_spec=pltpu.PrefetchScalarGridSpec(num_scalar_prefetch=1, grid=(n // tl,), in_specs=[spec, spec],
                                               out_specs=spec),
        out_shape=jax.ShapeDtypeStruct((k, n), I32),
        compiler_params=pltpu.CompilerParams(dimension_semantics=("arbitrary",)),
        name="slots",
    )(pad_start, top_e, rank)


PAST_LEN = 1024


def kernel(x_prompt, x_sample, state_gla, cache_pool, cache_mem_k, cache_mem_v, mem_prompt, ln_in_g, ln_in_b, w_in, w_gate_up, b_gate, gla_norm_g, pool_w, pool_scale, w_out, ln1_g, ln1_b, wq_mem, wk_mem, wv_mem, wo_mem, ln2_g, ln2_b, w_router, router_bias, w1_exp, w3_exp, w2_exp, w1_sh, w3_sh, w2_sh, ln3_g, ln3_b):
    assert w_in.shape[0] == 1, "single-layer trunk"
    bp, lp, d = x_prompt.shape
    bs, ls, _ = x_sample.shape
    n_p, n_s = bp * lp, bs * ls
    n_all = n_p + n_s

    wts_a = _prep_trunk_a_weights(ln_in_g, ln_in_b, w_in[0], w_gate_up[0], b_gate[0], gla_norm_g[0], pool_w[0],
                                  pool_scale[0], w_out[0], ln1_g[0], ln1_b[0])
    s0 = jnp.zeros((bp, GLA_HEADS, GLA_DK, GLA_DV), F32)
    h0 = jnp.zeros((bp, POOL_HIST, POOL_WIDTH), F32)
    x1p, sp, hp = _trunk_a(x_prompt, s0, h0, wts_a, start_pos=0)
    x1s, ss, hs = _trunk_a(x_sample, state_gla[0], cache_pool[0], wts_a, start_pos=PAST_LEN)

    mk, mv, mkb, mvb = _mem_kv(mem_prompt.reshape(bp * N_MEM, d), wk_mem[0].astype(BF16), wv_mem[0].astype(BF16))
    wq, wo = wq_mem[0].astype(BF16), wo_mem[0].astype(BF16)
    g2, b2 = ln2_g[0].reshape(1, d), ln2_b[0].reshape(1, d)
    x2p = _attn(x1p, mkb.reshape(bp, N_MEM, d), mvb.reshape(bp, N_MEM, d), wq, wo, g2, b2)
    x2s = _attn(x1s, cache_mem_k[0].reshape(bs, N_MEM, d).astype(BF16),
                cache_mem_v[0].reshape(bs, N_MEM, d).astype(BF16), wq, wo, g2, b2)

    wrt = w_router[0].T
    wrt_h = wrt.astype(BF16)
    wrt_m = (wrt - wrt_h.astype(F32)).astype(BF16)
    top_e, gate, rank, cnt = _router(x2p, x2s, jnp.stack([wrt_h, wrt_m]), router_bias[0].reshape(N_EXPERTS, 1))

    assert (n_all * TOP_K) % EXPERT_TM == 0
    n_blk = n_all * TOP_K // EXPERT_TM + N_EXPERTS
    pad_start, blk0, nblk, counts, nu, fill_start, fill_len = _moe_plan(cnt)
    dest = _slots(pad_start, top_e, rank).T.reshape(-1)
    xs = _dispatch(fill_start, fill_len, nu, dest, x2p, x2s, n_blk=n_blk, tm=EXPERT_TM)
    ys = _experts(blk0, nblk, counts, nu, xs, w1_exp[0], w3_exp[0], w2_exp[0], tm=EXPERT_TM)
    sh = (w1_sh[0].astype(BF16), w3_sh[0].astype(BF16), w2_sh[0].astype(BF16),
          ln3_g[0].reshape(1, d), ln3_b[0].reshape(1, d))
    gate_t = gate.T
    yp = _combine(dest, x2p, gate_t, *sh, ys, row_off=0)
    ysm = _combine(dest, x2s, gate_t, *sh, ys, row_off=n_p)

    return (yp.reshape(bp, lp, d), ysm.reshape(bs, ls, d), sp[None], hp[None],
            mk.reshape(1, bp, N_MEM, MEM_HEADS, MEM_DH), mv.reshape(1, bp, N_MEM, MEM_HEADS, MEM_DH),
            ss[None], hs[None])
```

```python
import functools

import jax
import jax.numpy as jnp
from jax import lax
from jax.experimental import pallas as pl
from jax.experimental.pallas import tpu as pltpu

F32 = jnp.float32
BF16 = jnp.bfloat16
I32 = jnp.int32
U32 = jnp.uint32

D_MODEL = 1024
CHUNK = 64
SUB = 16
GLA_HEADS = 4
GLA_DK = 64
GLA_DV = 128
GLA_KEY = GLA_HEADS * GLA_DK
GLA_WIDTH = GLA_HEADS * GLA_DV
GATE_RANK = 16
GATE_PAD = 128
POOL_WIDTH = 512
POOL_WINDOWS = (2, 4, 8, 16)
POOL_GC = 128
POOL_HIST = 15
HIST_ROWS = 16
N_MEM = 256
MEM_HEADS = 4
MEM_DH = 256
N_EXPERTS = 256
N_GROUPS = 8
EXPERTS_PER_GROUP = 32
TOPK_GROUPS = 4
TOP_K = 8
ROUTED_SCALE = 2.5
D_EXPERT = 256
TOKEN_TILE = 256
SUBLANES, LANES = 8, 128
assert D_MODEL == SUBLANES * LANES
ALPHA = 2.0 ** 0.25
EPS = 1e-5
OFF_Q, OFF_K, OFF_V, OFF_G, OFF_U, OFF_GD = 0, 256, 512, 1024, 1536, 2048
D_IN_PAD = OFF_GD + GATE_PAD

VMEM_LIMIT = 56 * 1024 * 1024


def _ln(x, g, b):
    mu = jnp.mean(x, axis=-1, keepdims=True)
    xc = x - mu
    var = jnp.mean(xc * xc, axis=-1, keepdims=True)
    return xc * lax.rsqrt(var + EPS) * g + b


def _dot(a, b):
    return jnp.dot(a, b, preferred_element_type=F32)


def _dot_nt(a, b):
    return lax.dot_general(a, b, (((1,), (1,)), ((), ())), preferred_element_type=F32)


def _dot_tn(a, b):
    return lax.dot_general(a, b, (((0,), (0,)), ((), ())), preferred_element_type=F32)


def _split3(x):
    h = x.astype(BF16)
    r = x - h.astype(F32)
    m = r.astype(BF16)
    l = (r - m.astype(F32)).astype(BF16)
    return h, m, l


def _trunk_a_kernel(x_ref, s0_ref, h0_ref, lng_ref, lnb_ref, win_ref, wgu_ref, bg_ref, gng_ref,
                    pw_ref, ps_ref, wout_ref, l1g_ref, l1b_ref,
                    x1_ref, sn_ref, hn_ref,
                    proj_scr, ext_scr, s_scr, op_scr, *, tl, chunk, start_pos):
    t = pl.program_id(1)
    nt = pl.num_programs(1)

    @pl.when(t == 0)
    def _():
        s_scr[...] = s0_ref[0]
        ext_scr[0:1, :] = jnp.zeros((1, POOL_WIDTH), F32)
        ext_scr[1:HIST_ROWS, :] = h0_ref[0]

    xn = _ln(x_ref[0], lng_ref[...], lnb_ref[...])
    proj_scr[...] = _dot(xn.astype(BF16), win_ref[...])

    c = chunk
    shift = lambda a, n: lax.shift_right_logical(a, n.bit_length() - 1)
    gd = proj_scr[:, OFF_GD:OFF_GD + GATE_PAD]
    z = _dot(gd.astype(BF16), wgu_ref[...]) + bg_ref[...]
    lf = (jnp.minimum(z, 0.0) - jnp.log1p(jnp.exp(-jnp.abs(z)))) * (1.0 / 16.0)
    trow = lax.broadcasted_iota(I32, (tl, tl), 0)
    tcol = lax.broadcasted_iota(I32, (tl, tl), 1)
    tri = jnp.logical_and(tcol <= trow, shift(tcol, c) == shift(trow, c)).astype(BF16)
    lh, lm, ll = _split3(lf)
    cum = _dot(tri, lh) + _dot(tri, lm) + _dot(tri, ll)
    q_all = proj_scr[:, OFF_Q:OFF_Q + GLA_KEY] * (GLA_DK ** -0.5)
    k_all = proj_scr[:, OFF_K:OFF_K + GLA_KEY]
    qs_all = q_all * jnp.exp(cum)

    causal = lax.broadcasted_iota(I32, (c, c), 1) <= lax.broadcasted_iota(I32, (c, c), 0)
    eye_dk = lax.broadcasted_iota(I32, (GLA_DK, GLA_DK), 0) == lax.broadcasted_iota(I32, (GLA_DK, GLA_DK), 1)
    n_sub = c // SUB
    rblk = shift(lax.broadcasted_iota(I32, (c, n_sub * GLA_DK), 0), SUB)
    lblk = shift(lax.broadcasted_iota(I32, (c, n_sub * GLA_DK), 1), GLA_DK)
    lblk_row = shift(lax.broadcasted_iota(I32, (1, n_sub * GLA_DK), 1), GLA_DK)
    mask_q = rblk == lblk
    mask_k = rblk <= lblk

    def tile_lanes(a):
        return jnp.concatenate([a] * n_sub, axis=1)

    states = [s_scr[h] for h in range(GLA_HEADS)]
    for ci in range(tl // c):
        rs = slice(ci * c, (ci + 1) * c)
        cum_c = cum[rs]
        last = cum_c[c - 1:c, :]
        ks_c = k_all[rs] * jnp.exp(last - cum_c)
        for h in range(GLA_HEADS):
            ksl = slice(h * GLA_DK, (h + 1) * GLA_DK)
            vsl = slice(h * GLA_DV, (h + 1) * GLA_DV)
            v_h = proj_scr[rs, OFF_V + h * GLA_DV:OFF_V + (h + 1) * GLA_DV].astype(BF16)
            cum_t = tile_lanes(cum_c[:, ksl])
            q_t = tile_lanes(q_all[rs, ksl])
            k_t = tile_lanes(k_all[rs, ksl])
            ref_row = jnp.zeros((1, n_sub * GLA_DK), F32)
            for i in range(1, n_sub):
                ref_row = jnp.where(lblk_row == i, cum_t[i * SUB - 1:i * SUB, :], ref_row)
            arg = cum_t - ref_row
            lhs = jnp.where(mask_q, q_t * jnp.exp(jnp.where(mask_q, arg, 0.0)), 0.0)
            rhs = jnp.where(mask_k, k_t * jnp.exp(jnp.where(mask_k, -arg, 0.0)), 0.0)
            att = jnp.where(causal, _dot_nt(lhs.astype(BF16), rhs.astype(BF16)), 0.0)
            s_h = states[h]
            o_h = _dot(att.astype(BF16), v_h) + _dot(qs_all[rs, ksl].astype(BF16), s_h.astype(BF16))
            dcol = jnp.sum(jnp.where(eye_dk, jnp.broadcast_to(jnp.exp(last[:, ksl]), (GLA_DK, GLA_DK)), 0.0),
                           axis=1, keepdims=True)
            states[h] = dcol * s_h + _dot_tn(ks_c[:, ksl].astype(BF16), v_h)
            o_h = o_h * lax.rsqrt(jnp.mean(o_h * o_h, axis=-1, keepdims=True) + EPS) * gng_ref[...]
            g_h = proj_scr[rs, OFF_G + h * GLA_DV:OFF_G + (h + 1) * GLA_DV]
            op_scr[rs, vsl] = (o_h * (g_h * jax.nn.sigmoid(g_h))).astype(BF16)
    for h in range(GLA_HEADS):
        s_scr[h] = states[h]

    u = proj_scr[:, OFF_U:OFF_U + POOL_WIDTH]
    ext_scr[HIST_ROWS:HIST_ROWS + tl, :] = u
    n_valid = start_pos + t * tl + lax.broadcasted_iota(I32, (tl, 1), 0) + 1
    for gi, w in enumerate(POOL_WINDOWS):
        lsl = slice(gi * POOL_GC, (gi + 1) * POOL_GC)
        win = ext_scr[HIST_ROWS:HIST_ROWS + tl, lsl]
        for s in range(1, w):
            win = win + ext_scr[HIST_ROWS - s:HIST_ROWS - s + tl, lsl]
        cnt = jnp.minimum(w, n_valid).astype(F32)
        r = win / cnt - u[:, lsl]
        p = _dot(r.astype(BF16), pw_ref[gi]) * ps_ref[:, lsl]
        op_scr[:, GLA_WIDTH + gi * POOL_GC:GLA_WIDTH + (gi + 1) * POOL_GC] = p.astype(BF16)
    tail = ext_scr[tl:tl + HIST_ROWS, :]
    ext_scr[0:HIST_ROWS, :] = tail

    mix = _dot(op_scr[...], wout_ref[...])
    x1_ref[0] = _ln(xn * ALPHA + mix, l1g_ref[...], l1b_ref[...])

    @pl.when(t == nt - 1)
    def _():
        sn_ref[0] = s_scr[...]
        hn_ref[0] = ext_scr[1:HIST_ROWS, :]


def _trunk_a(x, s0, h0, wts, *, start_pos):
    b, l, d = x.shape
    tl = min(l, 256)
    chunk = min(tl, CHUNK)
    assert l % tl == 0 and tl % chunk == 0 and chunk % SUB == 0 and l >= HIST_ROWS
    nt = l // tl
    kern = functools.partial(_trunk_a_kernel, tl=tl, chunk=chunk, start_pos=start_pos)
    const = lambda shape: pl.BlockSpec(shape, lambda bi, ti: (0,) * len(shape))
    return pl.pallas_call(
        kern,
        grid=(b, nt),
        in_specs=[
            pl.BlockSpec((1, tl, d), lambda bi, ti: (bi, ti, 0)),
            pl.BlockSpec((1, GLA_HEADS, GLA_DK, GLA_DV), lambda bi, ti: (bi, 0, 0, 0)),
            pl.BlockSpec((1, POOL_HIST, POOL_WIDTH), lambda bi, ti: (bi, 0, 0)),
            const((1, d)), const((1, d)),
            const((d, D_IN_PAD)), const((GATE_PAD, GLA_KEY)), const((1, GLA_KEY)), const((1, GLA_DV)),
            const((len(POOL_WINDOWS), POOL_GC, POOL_GC)), const((1, POOL_WIDTH)),
            const((GLA_WIDTH + POOL_WIDTH, d)), const((1, d)), const((1, d)),
        ],
        out_specs=[
            pl.BlockSpec((1, tl, d), lambda bi, ti: (bi, ti, 0)),
            pl.BlockSpec((1, GLA_HEADS, GLA_DK, GLA_DV), lambda bi, ti: (bi, 0, 0, 0)),
            pl.BlockSpec((1, POOL_HIST, POOL_WIDTH), lambda bi, ti: (bi, 0, 0)),
        ],
        out_shape=[
            jax.ShapeDtypeStruct((b, l, d), F32),
            jax.ShapeDtypeStruct((b, GLA_HEADS, GLA_DK, GLA_DV), F32),
            jax.ShapeDtypeStruct((b, POOL_HIST, POOL_WIDTH), F32),
        ],
        scratch_shapes=[
            pltpu.VMEM((tl, D_IN_PAD), F32),
            pltpu.VMEM((HIST_ROWS + tl, POOL_WIDTH), F32),
            pltpu.VMEM((GLA_HEADS, GLA_DK, GLA_DV), F32),
            pltpu.VMEM((tl, GLA_WIDTH + POOL_WIDTH), BF16),
        ],
        compiler_params=pltpu.CompilerParams(
            dimension_semantics=("arbitrary", "arbitrary"), vmem_limit_bytes=VMEM_LIMIT),
        name="trunk_a",
    )(x, s0, h0, *wts)


def _prep_trunk_a_weights(ln_in_g, ln_in_b, w_in, w_gate_up, b_gate, gla_norm_g, pool_w, pool_scale, w_out,
                          ln1_g, ln1_b):
    d = D_MODEL
    p_gd = 2 * GLA_KEY + 2 * GLA_WIDTH
    w_in_r = jnp.concatenate(
        [w_in[:, :p_gd], w_in[:, p_gd + GATE_RANK:], w_in[:, p_gd:p_gd + GATE_RANK],
         jnp.zeros((d, GATE_PAD - GATE_RANK), w_in.dtype)], axis=1).astype(BF16)
    wgu = jnp.concatenate([w_gate_up, jnp.zeros((GATE_PAD - GATE_RANK, GLA_KEY), w_gate_up.dtype)],
                          axis=0).astype(BF16)
    return (ln_in_g.reshape(1, d), ln_in_b.reshape(1, d), w_in_r, wgu, b_gate.reshape(1, GLA_KEY),
            gla_norm_g.reshape(1, GLA_DV), pool_w.astype(BF16), pool_scale.reshape(1, POOL_WIDTH),
            w_out.astype(BF16), ln1_g.reshape(1, d), ln1_b.reshape(1, d))


def _mem_kv_kernel(m_ref, wk_ref, wv_ref, k_ref, v_ref, kb_ref, vb_ref):
    m = m_ref[...].astype(BF16)
    k = _dot(m, wk_ref[...])
    v = _dot(m, wv_ref[...])
    k_ref[...] = k
    v_ref[...] = v
    kb_ref[...] = k.astype(BF16)
    vb_ref[...] = v.astype(BF16)


def _mem_kv(mem, wk, wv):
    m, d = mem.shape
    tm = min(m, 512)
    assert m % tm == 0
    row = pl.BlockSpec((tm, d), lambda i: (i, 0))
    wspec = pl.BlockSpec((d, d), lambda i: (0, 0))
    return pl.pallas_call(
        _mem_kv_kernel,
        grid=(m // tm,),
        in_specs=[row, wspec, wspec],
        out_specs=[row, row, row, row],
        out_shape=[jax.ShapeDtypeStruct((m, d), F32), jax.ShapeDtypeStruct((m, d), F32),
                   jax.ShapeDtypeStruct((m, d), BF16), jax.ShapeDtypeStruct((m, d), BF16)],
        compiler_params=pltpu.CompilerParams(dimension_semantics=("arbitrary",), vmem_limit_bytes=VMEM_LIMIT),
        name="mem_kv",
    )(mem, wk, wv)


def _attn_kernel(x1_ref, k_ref, v_ref, wq_ref, wo_ref, g_ref, b_ref, x2_ref, o_scr):
    x1 = x1_ref[0]
    q = (_dot(x1.astype(BF16), wq_ref[...]) * (MEM_DH ** -0.5)).astype(BF16)
    for h in range(MEM_HEADS):
        hs = slice(h * MEM_DH, (h + 1) * MEM_DH)
        s = _dot_nt(q[:, hs], k_ref[0, :, hs])
        e = jnp.exp(s - jnp.max(s, axis=-1, keepdims=True))
        p = e / jnp.sum(e, axis=-1, keepdims=True)
        o_scr[:, hs] = _dot(p.astype(BF16), v_ref[0, :, hs]).astype(BF16)
    attn = _dot(o_scr[...], wo_ref[...])
    x2_ref[...] = _ln(x1 * ALPHA + attn, g_ref[...], b_ref[...])


def _attn(x1, mem_k, mem_v, wq, wo, g, b):
    bsz, l, d = x1.shape
    tl = min(l, 256)
    assert l % tl == 0
    nt = l // tl
    const = lambda shape: pl.BlockSpec(shape, lambda bi, ti: (0,) * len(shape))
    return pl.pallas_call(
        _attn_kernel,
        grid=(bsz, nt),
        in_specs=[
            pl.BlockSpec((1, tl, d), lambda bi, ti: (bi, ti, 0)),
            pl.BlockSpec((1, N_MEM, d), lambda bi, ti: (bi, 0, 0)),
            pl.BlockSpec((1, N_MEM, d), lambda bi, ti: (bi, 0, 0)),
            const((d, d)), const((d, d)), const((1, d)), const((1, d)),
        ],
        out_specs=pl.BlockSpec((tl, d), lambda bi, ti: (bi * nt + ti, 0)),
        out_shape=jax.ShapeDtypeStruct((bsz * l, d), F32),
        scratch_shapes=[pltpu.VMEM((tl, d), BF16)],
        compiler_params=pltpu.CompilerParams(
            dimension_semantics=("arbitrary", "arbitrary"), vmem_limit_bytes=VMEM_LIMIT),
        name="attn",
    )(x1, mem_k, mem_v, wq, wo, g, b)


def _two_part_specs(t, d, tiles_a):
    spec_a = pl.BlockSpec((t, d), lambda i, *_: (jnp.minimum(i, tiles_a - 1), 0))
    spec_b = pl.BlockSpec((t, d), lambda i, *_: (jnp.maximum(i - tiles_a, 0), 0))
    return spec_a, spec_b


def _router_kernel(xa_ref, xb_ref, wrt_ref, bias_ref, e_ref, g_ref, r_ref, cnt_ref, cnt_scr, *, tl, tiles_a):
    i = pl.program_id(0)

    @pl.when(i == 0)
    def _():
        cnt_scr[...] = jnp.zeros_like(cnt_scr)

    x = jnp.where(i < tiles_a, xa_ref[...], xb_ref[...])
    xh = x.astype(BF16)
    xm = (x - xh.astype(F32)).astype(BF16)
    wh = wrt_ref[0]
    wm = wrt_ref[1]
    logits = _dot_nt(wh, xh) + (_dot_nt(wh, xm) + _dot_nt(wm, xh))
    scores = jax.nn.sigmoid(logits)
    biased = scores + bias_ref[...]
    ninf = -jnp.inf
    eg = EXPERTS_PER_GROUP
    riota = lax.broadcasted_iota(I32, (eg, tl), 0)
    gs_rows = []
    for g in range(N_GROUPS):
        blk = biased[g * eg:(g + 1) * eg, :]
        m1 = jnp.max(blk, axis=0, keepdims=True)
        i1 = jnp.min(jnp.where(blk == m1, riota, eg), axis=0, keepdims=True)
        m2 = jnp.max(jnp.where(riota == i1, ninf, blk), axis=0, keepdims=True)
        gs_rows.append(m1 + m2)
    gs = jnp.concatenate(gs_rows, axis=0)
    giota = lax.broadcasted_iota(I32, (N_GROUPS, tl), 0)
    sel = jnp.zeros((N_GROUPS, tl), jnp.bool_)
    for _ in range(TOPK_GROUPS):
        m = jnp.max(gs, axis=0, keepdims=True)
        gi = jnp.min(jnp.where(gs == m, giota, N_GROUPS), axis=0, keepdims=True)
        hit = giota == gi
        sel = jnp.logical_or(sel, hit)
        gs = jnp.where(hit, ninf, gs)
    self = jnp.where(sel, 1.0, 0.0)
    masked = jnp.concatenate(
        [jnp.where(self[g:g + 1, :] > 0.5, biased[g * eg:(g + 1) * eg, :], ninf) for g in range(N_GROUPS)], axis=0)
    eiota = lax.broadcasted_iota(I32, (N_EXPERTS, tl), 0)
    idx_rows, sc_rows = [], []
    multi = jnp.zeros((N_EXPERTS, tl), F32)
    for _ in range(TOP_K):
        m = jnp.max(masked, axis=0, keepdims=True)
        idx = jnp.min(jnp.where(masked == m, eiota, N_EXPERTS), axis=0, keepdims=True)
        hit = eiota == idx
        sc_rows.append(jnp.sum(jnp.where(hit, scores, 0.0), axis=0, keepdims=True))
        idx_rows.append(idx)
        multi = jnp.where(hit, 1.0, multi)
        masked = jnp.where(hit, ninf, masked)
    top_e = jnp.concatenate(idx_rows, axis=0)
    sc = jnp.concatenate(sc_rows, axis=0)
    e_ref[...] = top_e
    g_ref[...] = sc / jnp.sum(sc, axis=0, keepdims=True) * ROUTED_SCALE
    mh = multi.astype(BF16)
    before = (lax.broadcasted_iota(I32, (tl, tl), 0) < lax.broadcasted_iota(I32, (tl, tl), 1)).astype(BF16)
    running = cnt_scr[...]
    rankmat = _dot(mh, before) + jnp.concatenate([running] * (tl // 128), axis=1)
    r_rows = [jnp.sum(jnp.where(eiota == idx_rows[k], rankmat, 0.0), axis=0, keepdims=True) for k in range(TOP_K)]
    r_ref[...] = jnp.concatenate(r_rows, axis=0).astype(I32)
    total = running + _dot(mh, jnp.ones((tl, 128), BF16))
    cnt_scr[...] = total
    cnt_ref[...] = total


def _router(x2a, x2b, wrt, bias_col):
    d = x2a.shape[1]
    tl = TOKEN_TILE
    assert x2a.shape[0] % tl == 0 and x2b.shape[0] % tl == 0
    tiles_a = x2a.shape[0] // tl
    n = x2a.shape[0] + x2b.shape[0]
    kspec = pl.BlockSpec((TOP_K, tl), lambda i: (0, i))
    return pl.pallas_call(
        functools.partial(_router_kernel, tl=tl, tiles_a=tiles_a),
        grid=(n // tl,),
        in_specs=[*_two_part_specs(tl, d, tiles_a),
                  pl.BlockSpec((2, N_EXPERTS, d), lambda i: (0, 0, 0)),
                  pl.BlockSpec((N_EXPERTS, 1), lambda i: (0, 0))],
        out_specs=[kspec, kspec, kspec, pl.BlockSpec((N_EXPERTS, 128), lambda i: (0, 0))],
        out_shape=[jax.ShapeDtypeStruct((TOP_K, n), I32), jax.ShapeDtypeStruct((TOP_K, n), F32),
                   jax.ShapeDtypeStruct((TOP_K, n), I32), jax.ShapeDtypeStruct((N_EXPERTS, 128), F32)],
        scratch_shapes=[pltpu.VMEM((N_EXPERTS, 128), F32)],
        compiler_params=pltpu.CompilerParams(dimension_semantics=("arbitrary",), vmem_limit_bytes=VMEM_LIMIT),
        name="router",
    )(x2a, x2b, wrt, bias_col)


PACK_ROWS = D_MODEL // 2 // LANES
HI_MASK = 0xFFFF0000


def _slots_of(buf_ref, start, n=None):
    if n is None:
        return buf_ref.at[start, pl.ds(0, PACK_ROWS)]
    return buf_ref.at[pl.ds(start, n), pl.ds(0, PACK_ROWS)]


def _word_rows(buf_ref):
    if len(buf_ref.shape) == 3:
        return buf_ref.reshape(buf_ref.shape[0] * SUBLANES, LANES), SUBLANES
    return buf_ref, PACK_ROWS


def _pack_rows(x, dst_ref, row0=0):
    t = x.shape[0]
    bits = lax.bitcast_convert_type(x.astype(BF16).astype(F32), U32)
    flat, pitch = _word_rows(dst_ref)
    for j in range(PACK_ROWS):
        lo = lax.shift_right_logical(bits[:, j * LANES:(j + 1) * LANES], jnp.uint32(16))
        hi = bits[:, D_MODEL // 2 + j * LANES:D_MODEL // 2 + (j + 1) * LANES] & jnp.uint32(HI_MASK)
        flat[pl.ds(row0 * pitch + j, t, stride=pitch), :] = hi | lo


def _unpack_words(w):
    lo = lax.bitcast_convert_type(lax.shift_left(w, jnp.uint32(16)), F32)
    hi = lax.bitcast_convert_type(w & jnp.uint32(HI_MASK), F32)
    return lo, hi


def _unpack_rows(src_ref, t, row0=0):
    flat, pitch = _word_rows(src_ref)
    parts = [_unpack_words(flat[pl.ds(row0 * pitch + j, t, stride=pitch), :]) for j in range(PACK_ROWS)]
    return jnp.concatenate([p[0] for p in parts] + [p[1] for p in parts], axis=1)


def _dispatch_kernel(fs_ref, fl_ref, nu_ref, dest_ref, xa_ref, xb_ref, xs_ref, stage, zbuf, sem, zsem,
                     *, t, tm, tiles_a, n_blk):
    i = pl.program_id(0)
    nsteps = pl.num_programs(0)

    def zero_fill(start):
        def go(cp):
            if start:
                cp.start()
            else:
                cp.wait()

        def per_expert(e, carry):
            ln = fl_ref[e]
            off = fs_ref[e]
            bit = tm // 2
            while bit:
                @pl.when((ln & bit) != 0)
                def _(off=off, bit=bit):
                    go(pltpu.make_async_copy(_slots_of(zbuf, 0, bit), xs_ref.at[pl.ds(off, bit)], zsem))
                off = off + (ln & bit)
                bit //= 2
            return carry

        lax.fori_loop(0, N_EXPERTS, per_expert, 0)

        def per_block(b, carry):
            go(pltpu.make_async_copy(_slots_of(zbuf, 0, tm), xs_ref.at[pl.ds(b * tm, tm)], zsem))
            return carry

        lax.fori_loop(nu_ref[0], n_blk, per_block, 0)

    @pl.when(i == 0)
    def _():
        zbuf[...] = jnp.zeros_like(zbuf)
        zero_fill(True)

    def drain(slot):
        for k in range(TOP_K):
            pltpu.make_async_copy(_slots_of(stage.at[slot], 0, t), xs_ref.at[pl.ds(0, t)], sem.at[slot]).wait()

    def step(slot):
        @pl.when(i >= 2)
        def _():
            drain(slot)

        _pack_rows(jnp.where(i < tiles_a, xa_ref[...], xb_ref[...]), stage.at[slot])

        def body(n, carry):
            src = _slots_of(stage.at[slot], n)
            for k in range(TOP_K):
                dst = xs_ref.at[dest_ref[n * TOP_K + k]]
                pltpu.make_async_copy(src, dst, sem.at[slot]).start(priority=k % 2)
            return carry

        lax.fori_loop(0, t, body, 0)

    for slot in range(2):
        @pl.when(i % 2 == slot)
        def _(slot=slot):
            step(slot)

    @pl.when(i == 0)
    def _():
        zero_fill(False)

    @pl.when(i == nsteps - 1)
    def _():
        for slot in range(2):
            @pl.when(jnp.logical_or(nsteps >= 2, (nsteps - 1) % 2 == slot))
            def _(slot=slot):
                drain(slot)


def _dispatch(fill_start, fill_len, nu, dest, x2a, x2b, *, n_blk, tm):
    d = x2a.shape[1]
    t = TOKEN_TILE
    assert x2a.shape[0] % t == 0 and x2b.shape[0] % t == 0
    tiles_a = x2a.shape[0] // t
    n = x2a.shape[0] + x2b.shape[0]
    return pl.pallas_call(
        functools.partial(_dispatch_kernel, t=t, tm=tm, tiles_a=tiles_a, n_blk=n_blk),
        grid_spec=pltpu.PrefetchScalarGridSpec(
            num_scalar_prefetch=3,
            grid=(n // t,),
            in_specs=[pl.BlockSpec((t * TOP_K,), lambda i, *_: (i,), memory_space=pltpu.SMEM),
                      *_two_part_specs(t, d, tiles_a)],
            out_specs=pl.BlockSpec(memory_space=pl.ANY),
            scratch_shapes=[pltpu.VMEM((2, t, SUBLANES, LANES), U32), pltpu.VMEM((tm, SUBLANES, LANES), U32),
                            pltpu.SemaphoreType.DMA((2,)), pltpu.SemaphoreType.DMA(())],
        ),
        out_shape=jax.ShapeDtypeStruct((n_blk * tm, PACK_ROWS, LANES), U32),
        compiler_params=pltpu.CompilerParams(dimension_semantics=("arbitrary",), vmem_limit_bytes=VMEM_LIMIT),
        name="dispatch",
    )(fill_start, fill_len, nu, dest, x2a, x2b)


def _experts_kernel(b0_ref, nb_ref, cnt_ref, nu_ref, w1_ref, w3_ref, w2_ref, xs_ref, ys_ref,
                    w1b, w3b, w2b, xbuf, ybuf, in_sem, out_sem, *, tm, n_blk):
    e = pl.program_id(0)
    n_used = nu_ref[0]

    blk_rows = tm * PACK_ROWS

    def hbm_block(ref, b):
        return ref.at[pl.ds(pl.multiple_of(b * blk_rows, blk_rows), blk_rows)]

    def x_copy(b):
        return pltpu.make_async_copy(hbm_block(xs_ref, b), xbuf.at[b % 2], in_sem.at[b % 2])

    def y_copy(b):
        return pltpu.make_async_copy(ybuf.at[b % 2], hbm_block(ys_ref, b), out_sem.at[b % 2])

    @pl.when(e == 0)
    def _():
        x_copy(0).start()

    w1b[...] = w1_ref[0].astype(BF16)
    w3b[...] = w3_ref[0].astype(BF16)
    w2b[...] = w2_ref[0].astype(BF16)
    b0 = b0_ref[e]
    cnt = cnt_ref[e]

    def block(j, carry):
        b = b0 + j
        x_copy(b).wait()

        @pl.when(b + 1 < n_used)
        def _():
            x_copy(b + 1).start()

        @pl.when(b >= 2)
        def _():
            y_copy(b - 2).wait()

        rows = tm // EXPERT_SPLIT
        for r0 in range(0, tm, rows):
            live = lax.broadcasted_iota(I32, (rows, 1), 0) < cnt - j * tm - r0
            x = jnp.where(live, _unpack_rows(xbuf.at[b % 2], rows, r0), 0.0).astype(BF16)
            a = _dot(x, w1b[...])
            h = (a * jax.nn.sigmoid(a)) * _dot(x, w3b[...])
            _pack_rows(_dot(h.astype(BF16), w2b[...]), ybuf.at[b % 2], r0)
        y_copy(b).start()
        return carry

    lax.fori_loop(0, nb_ref[e], block, 0)

    @pl.when(e == pl.num_programs(0) - 1)
    def _():
        @pl.when(n_used >= 2)
        def _():
            y_copy(n_used - 2).wait()

        y_copy(n_used - 1).wait()
        ybuf[0] = jnp.zeros(ybuf.shape[1:], U32)

        def zero_copy(b):
            return pltpu.make_async_copy(ybuf.at[0], hbm_block(ys_ref, b), out_sem.at[0])

        def zero(b, carry):
            zero_copy(b).start()
            return carry

        lax.fori_loop(n_used, n_blk, zero, 0)

        def zero_wait(b, carry):
            zero_copy(b).wait()
            return carry

        lax.fori_loop(n_used, n_blk, zero_wait, 0)


def _experts(blk0, nblk, counts, nu, xs, w1, w3, w2, *, tm):
    n_slot = xs.shape[0]
    n_blk = n_slot // tm
    n_exp, d, de = w1.shape
    wmap = lambda e, *_: (e, 0, 0)
    ys = pl.pallas_call(
        functools.partial(_experts_kernel, tm=tm, n_blk=n_blk),
        grid_spec=pltpu.PrefetchScalarGridSpec(
            num_scalar_prefetch=4,
            grid=(n_exp,),
            in_specs=[pl.BlockSpec((1, d, de), wmap), pl.BlockSpec((1, d, de), wmap), pl.BlockSpec((1, de, d), wmap),
                      pl.BlockSpec(memory_space=pl.ANY)],
            out_specs=pl.BlockSpec(memory_space=pl.ANY),
            scratch_shapes=[pltpu.VMEM((d, de), BF16), pltpu.VMEM((d, de), BF16), pltpu.VMEM((de, d), BF16),
                            pltpu.VMEM((2, tm * PACK_ROWS, LANES), U32), pltpu.VMEM((2, tm * PACK_ROWS, LANES), U32),
                            pltpu.SemaphoreType.DMA((2,)), pltpu.SemaphoreType.DMA((2,))],
        ),
        out_shape=jax.ShapeDtypeStruct((n_slot * PACK_ROWS, LANES), U32),
        compiler_params=pltpu.CompilerParams(dimension_semantics=("arbitrary",), vmem_limit_bytes=VMEM_LIMIT),
        name="experts",
    )(blk0, nblk, counts, nu, w1, w3, w2, xs.reshape(n_slot * PACK_ROWS, LANES))
    return ys.reshape(n_slot, PACK_ROWS, LANES)


def _combine_kernel(dcur_ref, dnxt_ref, x2_ref, g_ref, w1_ref, w3_ref, w2_ref, lg_ref, lb_ref, ys_ref,
                    out_ref, buf, acc_scr, sem, *, t, nsteps):
    i = pl.program_id(0)

    def fetch(d_ref, s, n):
        for k in range(TOP_K):
            pltpu.make_async_copy(ys_ref.at[d_ref[n * TOP_K + k]], _slots_of(buf.at[s, k], n),
                                  sem.at[s]).start(priority=k % 2)

    def drain(s):
        for k in range(TOP_K):
            pltpu.make_async_copy(ys_ref.at[pl.ds(0, t)], _slots_of(buf.at[s, k], 0, t), sem.at[s]).wait()

    @pl.when(i == 0)
    def _():
        def first(n, carry):
            fetch(dcur_ref, 0, n)
            return carry

        lax.fori_loop(0, t, first, 0)

    def step(s):
        drain(s)

        def group(gi, carry):
            r0 = pl.multiple_of(gi * SUBLANES, SUBLANES)
            for j in range(SUBLANES):
                fetch(dnxt_ref, 1 - s, r0 + j)
            gates = g_ref[pl.ds(r0, SUBLANES), :]
            gk = [jnp.broadcast_to(gates[:, k:k + 1], (SUBLANES, LANES)) for k in range(TOP_K)]
            planes = [buf.at[s, k].reshape(t * SUBLANES, LANES) for k in range(TOP_K)]
            for j in range(PACK_ROWS):
                acc_lo = acc_hi = None
                for k in range(TOP_K):
                    lo, hi = _unpack_words(planes[k][pl.ds(r0 * SUBLANES + j, SUBLANES, stride=SUBLANES), :])
                    acc_lo = gk[k] * lo if acc_lo is None else acc_lo + gk[k] * lo
                    acc_hi = gk[k] * hi if acc_hi is None else acc_hi + gk[k] * hi
                acc_scr[pl.ds(r0, SUBLANES), j * LANES:(j + 1) * LANES] = acc_lo
                acc_scr[pl.ds(r0, SUBLANES), D_MODEL // 2 + j * LANES:D_MODEL // 2 + (j + 1) * LANES] = acc_hi
            return carry

        lax.fori_loop(0, t // SUBLANES, group, 0)

    for s in range(2):
        @pl.when(i % 2 == s)
        def _(s=s):
            step(s)

    @pl.when(i == nsteps - 1)
    def _():
        drain(nsteps % 2)

    x2 = x2_ref[...]
    acc = acc_scr[...]
    xb = x2.astype(BF16)
    a = _dot(xb, w1_ref[...])
    hs = (a * jax.nn.sigmoid(a)) * _dot(xb, w3_ref[...])
    shared = _dot(hs.astype(BF16), w2_ref[...])
    out_ref[...] = _ln(x2 * ALPHA + (acc + shared), lg_ref[...], lb_ref[...])


def _combine(dest, x2, gate_t, w1s, w3s, w2s, lg, lb, ys, *, row_off):
    n_rows, d = x2.shape
    n = dest.shape[0] // TOP_K
    t = TOKEN_TILE
    assert row_off % t == 0 and n_rows % t == 0 and n % t == 0
    off = row_off // t
    last = n // t - 1
    de = w1s.shape[-1]
    const = lambda shape: pl.BlockSpec(shape, lambda i: (0,) * len(shape))
    return pl.pallas_call(
        functools.partial(_combine_kernel, t=t, nsteps=n_rows // t),
        grid=(n_rows // t,),
        in_specs=[pl.BlockSpec((t * TOP_K,), lambda i: (off + i,), memory_space=pltpu.SMEM),
                  pl.BlockSpec((t * TOP_K,), lambda i: (jnp.minimum(off + i + 1, last),), memory_space=pltpu.SMEM),
                  pl.BlockSpec((t, d), lambda i: (i, 0)),
                  pl.BlockSpec((t, TOP_K), lambda i: (off + i, 0)),
                  const((d, de)), const((d, de)), const((de, d)), const((1, d)), const((1, d)),
                  pl.BlockSpec(memory_space=pl.ANY)],
        out_specs=pl.BlockSpec((t, d), lambda i: (i, 0)),
        out_shape=jax.ShapeDtypeStruct((n_rows, d), F32),
        scratch_shapes=[pltpu.VMEM((2, TOP_K, t, SUBLANES, LANES), U32), pltpu.VMEM((t, d), F32),
                        pltpu.SemaphoreType.DMA((2,))],
        compiler_params=pltpu.CompilerParams(dimension_semantics=("arbitrary",), vmem_limit_bytes=VMEM_LIMIT),
        name="combine",
    )(dest, dest, x2, gate_t, w1s, w3s, w2s, lg, lb, ys)


EXPERT_TM = 512
EXPERT_SPLIT = 2


def _moe_plan(cnt):
    tm = EXPERT_TM
    counts = cnt[:, 0].astype(I32)
    padded = (counts + tm - 1) // tm * tm
    pad_end = jnp.cumsum(padded)
    pad_start = pad_end - padded
    n_used = pad_end[-1] // tm
    fill_start = (pad_start + counts).astype(I32)
    fill_len = (padded - counts).astype(I32)
    return (pad_start.astype(I32), (pad_start // tm).astype(I32), (padded // tm).astype(I32), counts,
            n_used.reshape(1).astype(I32), fill_start, fill_len)


def _slots_kernel(ps_ref, e_ref, r_ref, d_ref):
    e = e_ref[...]

    def body(j, base):
        return jnp.where(e == j, ps_ref[j], base)

    base = lax.fori_loop(0, N_EXPERTS, body, jnp.zeros_like(e), unroll=8)
    d_ref[...] = base + r_ref[...]


def _slots(pad_start, top_e, rank):
    k, n = top_e.shape
    tl = next(c for c in (2048, 1536, 1024, 512, 256, 128) if n % c == 0)
    spec = pl.BlockSpec((k, tl), lambda i, ps: (0, i))
    return pl.pallas_call(
        _slots_kernel,
        grid_spec=pltpu.PrefetchScalarGridSpec(num_scalar_prefetch=1, grid=(n // tl,), in_specs=[spec, spec],
                                               out_specs=spec),
        out_shape=jax.ShapeDtypeStruct((k, n), I32),
        compiler_params=pltpu.CompilerParams(dimension_semantics=("arbitrary",)),
        name="slots",
    )(pad_start, top_e, rank)


PAST_LEN = 1024


def kernel(x_prompt, x_sample, state_gla, cache_pool, cache_mem_k, cache_mem_v, mem_prompt, ln_in_g, ln_in_b, w_in, w_gate_up, b_gate, gla_norm_g, pool_w, pool_scale, w_out, ln1_g, ln1_b, wq_mem, wk_mem, wv_mem, wo_mem, ln2_g, ln2_b, w_router, router_bias, w1_exp, w3_exp, w2_exp, w1_sh, w3_sh, w2_sh, ln3_g, ln3_b):
    assert w_in.shape[0] == 1, "single-layer trunk"
    bp, lp, d = x_prompt.shape
    bs, ls, _ = x_sample.shape
    n_p, n_s = bp * lp, bs * ls
    n_all = n_p + n_s

    wts_a = _prep_trunk_a_weights(ln_in_g, ln_in_b, w_in[0], w_gate_up[0], b_gate[0], gla_norm_g[0], pool_w[0],
                                  pool_scale[0], w_out[0], ln1_g[0], ln1_b[0])
    s0 = jnp.zeros((bp, GLA_HEADS, GLA_DK, GLA_DV), F32)
    h0 = jnp.zeros((bp, POOL_HIST, POOL_WIDTH), F32)
    x1p, sp, hp = _trunk_a(x_prompt, s0, h0, wts_a, start_pos=0)
    x1s, ss, hs = _trunk_a(x_sample, state_gla[0], cache_pool[0], wts_a, start_pos=PAST_LEN)

    mk, mv, mkb, mvb = _mem_kv(mem_prompt.reshape(bp * N_MEM, d), wk_mem[0].astype(BF16), wv_mem[0].astype(BF16))
    wq, wo = wq_mem[0].astype(BF16), wo_mem[0].astype(BF16)
    g2, b2 = ln2_g[0].reshape(1, d), ln2_b[0].reshape(1, d)
    x2p = _attn(x1p, mkb.reshape(bp, N_MEM, d), mvb.reshape(bp, N_MEM, d), wq, wo, g2, b2)
    x2s = _attn(x1s, cache_mem_k[0].reshape(bs, N_MEM, d).astype(BF16),
                cache_mem_v[0].reshape(bs, N_MEM, d).astype(BF16), wq, wo, g2, b2)

    wrt = w_router[0].T
    wrt_h = wrt.astype(BF16)
    wrt_m = (wrt - wrt_h.astype(F32)).astype(BF16)
    top_e, gate, rank, cnt = _router(x2p, x2s, jnp.stack([wrt_h, wrt_m]), router_bias[0].reshape(N_EXPERTS, 1))

    assert (n_all * TOP_K) % EXPERT_TM == 0
    n_blk = n_all * TOP_K // EXPERT_TM + N_EXPERTS
    pad_start, blk0, nblk, counts, nu, fill_start, fill_len = _moe_plan(cnt)
    dest = _slots(pad_start, top_e, rank).T.reshape(-1)
    xs = _dispatch(fill_start, fill_len, nu, dest, x2p, x2s, n_blk=n_blk, tm=EXPERT_TM)
    ys = _experts(blk0, nblk, counts, nu, xs, w1_exp[0], w3_exp[0], w2_exp[0], tm=EXPERT_TM)
    sh = (w1_sh[0].astype(BF16), w3_sh[0].astype(BF16), w2_sh[0].astype(BF16),
          ln3_g[0].reshape(1, d), ln3_b[0].reshape(1, d))
    gate_t = gate.T
    yp = _combine(dest, x2p, gate_t, *sh, ys, row_off=0)
    ysm = _combine(dest, x2s, gate_t, *sh, ys, row_off=n_p)

    return (yp.reshape(bp, lp, d), ysm.reshape(bs, ls, d), sp[None], hp[None],
            mk.reshape(1, bp, N_MEM, MEM_HEADS, MEM_DH), mv.reshape(1, bp, N_MEM, MEM_HEADS, MEM_DH),
            ss[None], hs[None])
```

```python
import functools

import jax
import jax.numpy as jnp
from jax import lax
from jax.experimental import pallas as pl
from jax.experimental.pallas import tpu as pltpu

F32 = jnp.float32
BF16 = jnp.bfloat16
I32 = jnp.int32
U32 = jnp.uint32

D_MODEL = 1024
CHUNK = 64
SUB = 16
GLA_HEADS = 4
GLA_DK = 64
GLA_DV = 128
GLA_KEY = GLA_HEADS * GLA_DK
GLA_WIDTH = GLA_HEADS * GLA_DV
GATE_RANK = 16
GATE_PAD = 128
POOL_WIDTH = 512
POOL_WINDOWS = (2, 4, 8, 16)
POOL_GC = 128
POOL_HIST = 15
HIST_ROWS = 16
N_MEM = 256
MEM_HEADS = 4
MEM_DH = 256
N_EXPERTS = 256
N_GROUPS = 8
EXPERTS_PER_GROUP = 32
TOPK_GROUPS = 4
TOP_K = 8
ROUTED_SCALE = 2.5
D_EXPERT = 256
TOKEN_TILE = 256
SUBLANES, LANES = 8, 128
assert D_MODEL == SUBLANES * LANES
ALPHA = 2.0 ** 0.25
EPS = 1e-5
OFF_Q, OFF_K, OFF_V, OFF_G, OFF_U, OFF_GD = 0, 256, 512, 1024, 1536, 2048
D_IN_PAD = OFF_GD + GATE_PAD

VMEM_LIMIT = 56 * 1024 * 1024


def _ln(x, g, b):
    mu = jnp.mean(x, axis=-1, keepdims=True)
    xc = x - mu
    var = jnp.mean(xc * xc, axis=-1, keepdims=True)
    return xc * lax.rsqrt(var + EPS) * g + b


def _dot(a, b):
    return jnp.dot(a, b, preferred_element_type=F32)


def _dot_nt(a, b):
    return lax.dot_general(a, b, (((1,), (1,)), ((), ())), preferred_element_type=F32)


def _dot_tn(a, b):
    return lax.dot_general(a, b, (((0,), (0,)), ((), ())), preferred_element_type=F32)


def _split3(x):
    h = x.astype(BF16)
    r = x - h.astype(F32)
    m = r.astype(BF16)
    l = (r - m.astype(F32)).astype(BF16)
    return h, m, l


def _trunk_a_kernel(x_ref, s0_ref, h0_ref, lng_ref, lnb_ref, win_ref, wgu_ref, bg_ref, gng_ref,
                    pw_ref, ps_ref, wout_ref, l1g_ref, l1b_ref,
                    x1_ref, sn_ref, hn_ref,
                    proj_scr, ext_scr, s_scr, op_scr, *, tl, chunk, start_pos):
    t = pl.program_id(1)
    nt = pl.num_programs(1)

    @pl.when(t == 0)
    def _():
        s_scr[...] = s0_ref[0]
        ext_scr[0:1, :] = jnp.zeros((1, POOL_WIDTH), F32)
        ext_scr[1:HIST_ROWS, :] = h0_ref[0]

    xn = _ln(x_ref[0], lng_ref[...], lnb_ref[...])
    proj_scr[...] = _dot(xn.astype(BF16), win_ref[...])

    c = chunk
    shift = lambda a, n: lax.shift_right_logical(a, n.bit_length() - 1)
    gd = proj_scr[:, OFF_GD:OFF_GD + GATE_PAD]
    z = _dot(gd.astype(BF16), wgu_ref[...]) + bg_ref[...]
    lf = (jnp.minimum(z, 0.0) - jnp.log1p(jnp.exp(-jnp.abs(z)))) * (1.0 / 16.0)
    trow = lax.broadcasted_iota(I32, (tl, tl), 0)
    tcol = lax.broadcasted_iota(I32, (tl, tl), 1)
    tri = jnp.logical_and(tcol <= trow, shift(tcol, c) == shift(trow, c)).astype(BF16)
    lh, lm, ll = _split3(lf)
    cum = _dot(tri, lh) + _dot(tri, lm) + _dot(tri, ll)
    q_all = proj_scr[:, OFF_Q:OFF_Q + GLA_KEY] * (GLA_DK ** -0.5)
    k_all = proj_scr[:, OFF_K:OFF_K + GLA_KEY]
    qs_all = q_all * jnp.exp(cum)

    causal = lax.broadcasted_iota(I32, (c, c), 1) <= lax.broadcasted_iota(I32, (c, c), 0)
    eye_dk = lax.broadcasted_iota(I32, (GLA_DK, GLA_DK), 0) == lax.broadcasted_iota(I32, (GLA_DK, GLA_DK), 1)
    n_sub = c // SUB
    rblk = shift(lax.broadcasted_iota(I32, (c, n_sub * GLA_DK), 0), SUB)
    lblk = shift(lax.broadcasted_iota(I32, (c, n_sub * GLA_DK), 1), GLA_DK)
    lblk_row = shift(lax.broadcasted_iota(I32, (1, n_sub * GLA_DK), 1), GLA_DK)
    mask_q = rblk == lblk
    mask_k = rblk <= lblk

    def tile_lanes(a):
        return jnp.concatenate([a] * n_sub, axis=1)

    states = [s_scr[h] for h in range(GLA_HEADS)]
    for ci in range(tl // c):
        rs = slice(ci * c, (ci + 1) * c)
        cum_c = cum[rs]
        last = cum_c[c - 1:c, :]
        ks_c = k_all[rs] * jnp.exp(last - cum_c)
        for h in range(GLA_HEADS):
            ksl = slice(h * GLA_DK, (h + 1) * GLA_DK)
            vsl = slice(h * GLA_DV, (h + 1) * GLA_DV)
            v_h = proj_scr[rs, OFF_V + h * GLA_DV:OFF_V + (h + 1) * GLA_DV].astype(BF16)
            cum_t = tile_lanes(cum_c[:, ksl])
            q_t = tile_lanes(q_all[rs, ksl])
            k_t = tile_lanes(k_all[rs, ksl])
            ref_row = jnp.zeros((1, n_sub * GLA_DK), F32)
            for i in range(1, n_sub):
                ref_row = jnp.where(lblk_row == i, cum_t[i * SUB - 1:i * SUB, :], ref_row)
            arg = cum_t - ref_row
            lhs = jnp.where(mask_q, q_t * jnp.exp(jnp.where(mask_q, arg, 0.0)), 0.0)
            rhs = jnp.where(mask_k, k_t * jnp.exp(jnp.where(mask_k, -arg, 0.0)), 0.0)
            att = jnp.where(causal, _dot_nt(lhs.astype(BF16), rhs.astype(BF16)), 0.0)
            s_h = states[h]
            o_h = _dot(att.astype(BF16), v_h) + _dot(qs_all[rs, ksl].astype(BF16), s_h.astype(BF16))
            dcol = jnp.sum(jnp.where(eye_dk, jnp.broadcast_to(jnp.exp(last[:, ksl]), (GLA_DK, GLA_DK)), 0.0),
                           axis=1, keepdims=True)
            states[h] = dcol * s_h + _dot_tn(ks_c[:, ksl].astype(BF16), v_h)
            o_h = o_h * lax.rsqrt(jnp.mean(o_h * o_h, axis=-1, keepdims=True) + EPS) * gng_ref[...]
            g_h = proj_scr[rs, OFF_G + h * GLA_DV:OFF_G + (h + 1) * GLA_DV]
            op_scr[rs, vsl] = (o_h * (g_h * jax.nn.sigmoid(g_h))).astype(BF16)
    for h in range(GLA_HEADS):
        s_scr[h] = states[h]

    u = proj_scr[:, OFF_U:OFF_U + POOL_WIDTH]
    ext_scr[HIST_ROWS:HIST_ROWS + tl, :] = u
    n_valid = start_pos + t * tl + lax.broadcasted_iota(I32, (tl, 1), 0) + 1
    for gi, w in enumerate(POOL_WINDOWS):
        lsl = slice(gi * POOL_GC, (gi + 1) * POOL_GC)
        win = ext_scr[HIST_ROWS:HIST_ROWS + tl, lsl]
        for s in range(1, w):
            win = win + ext_scr[HIST_ROWS - s:HIST_ROWS - s + tl, lsl]
        cnt = jnp.minimum(w, n_valid).astype(F32)
        r = win / cnt - u[:, lsl]
        p = _dot(r.astype(BF16), pw_ref[gi]) * ps_ref[:, lsl]
        op_scr[:, GLA_WIDTH + gi * POOL_GC:GLA_WIDTH + (gi + 1) * POOL_GC] = p.astype(BF16)
    tail = ext_scr[tl:tl + HIST_ROWS, :]
    ext_scr[0:HIST_ROWS, :] = tail

    mix = _dot(op_scr[...], wout_ref[...])
    x1_ref[0] = _ln(xn * ALPHA + mix, l1g_ref[...], l1b_ref[...])

    @pl.when(t == nt - 1)
    def _():
        sn_ref[0] = s_scr[...]
        hn_ref[0] = ext_scr[1:HIST_ROWS, :]


def _trunk_a(x, s0, h0, wts, *, start_pos):
    b, l, d = x.shape
    tl = min(l, 256)
    chunk = min(tl, CHUNK)
    assert l % tl == 0 and tl % chunk == 0 and chunk % SUB == 0 and l >= HIST_ROWS
    nt = l // tl
    kern = functools.partial(_trunk_a_kernel, tl=tl, chunk=chunk, start_pos=start_pos)
    const = lambda shape: pl.BlockSpec(shape, lambda bi, ti: (0,) * len(shape))
    return pl.pallas_call(
        kern,
        grid=(b, nt),
        in_specs=[
            pl.BlockSpec((1, tl, d), lambda bi, ti: (bi, ti, 0)),
            pl.BlockSpec((1, GLA_HEADS, GLA_DK, GLA_DV), lambda bi, ti: (bi, 0, 0, 0)),
            pl.BlockSpec((1, POOL_HIST, POOL_WIDTH), lambda bi, ti: (bi, 0, 0)),
            const((1, d)), const((1, d)),
            const((d, D_IN_PAD)), const((GATE_PAD, GLA_KEY)), const((1, GLA_KEY)), const((1, GLA_DV)),
            const((len(POOL_WINDOWS), POOL_GC, POOL_GC)), const((1, POOL_WIDTH)),
            const((GLA_WIDTH + POOL_WIDTH, d)), const((1, d)), const((1, d)),
        ],
        out_specs=[
            pl.BlockSpec((1, tl, d), lambda bi, ti: (bi, ti, 0)),
            pl.BlockSpec((1, GLA_HEADS, GLA_DK, GLA_DV), lambda bi, ti: (bi, 0, 0, 0)),
            pl.BlockSpec((1, POOL_HIST, POOL_WIDTH), lambda bi, ti: (bi, 0, 0)),
        ],
        out_shape=[
            jax.ShapeDtypeStruct((b, l, d), F32),
            jax.ShapeDtypeStruct((b, GLA_HEADS, GLA_DK, GLA_DV), F32),
            jax.ShapeDtypeStruct((b, POOL_HIST, POOL_WIDTH), F32),
        ],
        scratch_shapes=[
            pltpu.VMEM((tl, D_IN_PAD), F32),
            pltpu.VMEM((HIST_ROWS + tl, POOL_WIDTH), F32),
            pltpu.VMEM((GLA_HEADS, GLA_DK, GLA_DV), F32),
            pltpu.VMEM((tl, GLA_WIDTH + POOL_WIDTH), BF16),
        ],
        compiler_params=pltpu.CompilerParams(
            dimension_semantics=("arbitrary", "arbitrary"), vmem_limit_bytes=VMEM_LIMIT),
        name="trunk_a",
    )(x, s0, h0, *wts)


def _prep_trunk_a_weights(ln_in_g, ln_in_b, w_in, w_gate_up, b_gate, gla_norm_g, pool_w, pool_scale, w_out,
                          ln1_g, ln1_b):
    d = D_MODEL
    p_gd = 2 * GLA_KEY + 2 * GLA_WIDTH
    w_in_r = jnp.concatenate(
        [w_in[:, :p_gd], w_in[:, p_gd + GATE_RANK:], w_in[:, p_gd:p_gd + GATE_RANK],
         jnp.zeros((d, GATE_PAD - GATE_RANK), w_in.dtype)], axis=1).astype(BF16)
    wgu = jnp.concatenate([w_gate_up, jnp.zeros((GATE_PAD - GATE_RANK, GLA_KEY), w_gate_up.dtype)],
                          axis=0).astype(BF16)
    return (ln_in_g.reshape(1, d), ln_in_b.reshape(1, d), w_in_r, wgu, b_gate.reshape(1, GLA_KEY),
            gla_norm_g.reshape(1, GLA_DV), pool_w.astype(BF16), pool_scale.reshape(1, POOL_WIDTH),
            w_out.astype(BF16), ln1_g.reshape(1, d), ln1_b.reshape(1, d))


def _mem_kv_kernel(m_ref, wk_ref, wv_ref, k_ref, v_ref, kb_ref, vb_ref):
    m = m_ref[...].astype(BF16)
    k = _dot(m, wk_ref[...])
    v = _dot(m, wv_ref[...])
    k_ref[...] = k
    v_ref[...] = v
    kb_ref[...] = k.astype(BF16)
    vb_ref[...] = v.astype(BF16)


def _mem_kv(mem, wk, wv):
    m, d = mem.shape
    tm = min(m, 512)
    assert m % tm == 0
    row = pl.BlockSpec((tm, d), lambda i: (i, 0))
    wspec = pl.BlockSpec((d, d), lambda i: (0, 0))
    return pl.pallas_call(
        _mem_kv_kernel,
        grid=(m // tm,),
        in_specs=[row, wspec, wspec],
        out_specs=[row, row, row, row],
        out_shape=[jax.ShapeDtypeStruct((m, d), F32), jax.ShapeDtypeStruct((m, d), F32),
                   jax.ShapeDtypeStruct((m, d), BF16), jax.ShapeDtypeStruct((m, d), BF16)],
        compiler_params=pltpu.CompilerParams(dimension_semantics=("arbitrary",), vmem_limit_bytes=VMEM_LIMIT),
        name="mem_kv",
    )(mem, wk, wv)


def _attn_kernel(x1_ref, k_ref, v_ref, wq_ref, wo_ref, g_ref, b_ref, x2_ref, o_scr):
    x1 = x1_ref[0]
    q = (_dot(x1.astype(BF16), wq_ref[...]) * (MEM_DH ** -0.5)).astype(BF16)
    for h in range(MEM_HEADS):
        hs = slice(h * MEM_DH, (h + 1) * MEM_DH)
        s = _dot_nt(q[:, hs], k_ref[0, :, hs])
        e = jnp.exp(s - jnp.max(s, axis=-1, keepdims=True))
        p = e / jnp.sum(e, axis=-1, keepdims=True)
        o_scr[:, hs] = _dot(p.astype(BF16), v_ref[0, :, hs]).astype(BF16)
    attn = _dot(o_scr[...], wo_ref[...])
    x2_ref[...] = _ln(x1 * ALPHA + attn, g_ref[...], b_ref[...])


def _attn(x1, mem_k, mem_v, wq, wo, g, b):
    bsz, l, d = x1.shape
    tl = min(l, 256)
    assert l % tl == 0
    nt = l // tl
    const = lambda shape: pl.BlockSpec(shape, lambda bi, ti: (0,) * len(shape))
    return pl.pallas_call(
        _attn_kernel,
        grid=(bsz, nt),
        in_specs=[
            pl.BlockSpec((1, tl, d), lambda bi, ti: (bi, ti, 0)),
            pl.BlockSpec((1, N_MEM, d), lambda bi, ti: (bi, 0, 0)),
            pl.BlockSpec((1, N_MEM, d), lambda bi, ti: (bi, 0, 0)),
            const((d, d)), const((d, d)), const((1, d)), const((1, d)),
        ],
        out_specs=pl.BlockSpec((tl, d), lambda bi, ti: (bi * nt + ti, 0)),
        out_shape=jax.ShapeDtypeStruct((bsz * l, d), F32),
        scratch_shapes=[pltpu.VMEM((tl, d), BF16)],
        compiler_params=pltpu.CompilerParams(
            dimension_semantics=("arbitrary", "arbitrary"), vmem_limit_bytes=VMEM_LIMIT),
        name="attn",
    )(x1, mem_k, mem_v, wq, wo, g, b)


def _two_part_specs(t, d, tiles_a):
    spec_a = pl.BlockSpec((t, d), lambda i, *_: (jnp.minimum(i, tiles_a - 1), 0))
    spec_b = pl.BlockSpec((t, d), lambda i, *_: (jnp.maximum(i - tiles_a, 0), 0))
    return spec_a, spec_b


def _router_kernel(xa_ref, xb_ref, wrt_ref, bias_ref, e_ref, g_ref, r_ref, cnt_ref, cnt_scr, *, tl, tiles_a):
    i = pl.program_id(0)

    @pl.when(i == 0)
    def _():
        cnt_scr[...] = jnp.zeros_like(cnt_scr)

    x = jnp.where(i < tiles_a, xa_ref[...], xb_ref[...])
    xh = x.astype(BF16)
    xm = (x - xh.astype(F32)).astype(BF16)
    wh = wrt_ref[0]
    wm = wrt_ref[1]
    logits = _dot_nt(wh, xh) + (_dot_nt(wh, xm) + _dot_nt(wm, xh))
    scores = jax.nn.sigmoid(logits)
    biased = scores + bias_ref[...]
    ninf = -jnp.inf
    eg = EXPERTS_PER_GROUP
    riota = lax.broadcasted_iota(I32, (eg, tl), 0)
    gs_rows = []
    for g in range(N_GROUPS):
        blk = biased[g * eg:(g + 1) * eg, :]
        m1 = jnp.max(blk, axis=0, keepdims=True)
        i1 = jnp.min(jnp.where(blk == m1, riota, eg), axis=0, keepdims=True)
        m2 = jnp.max(jnp.where(riota == i1, ninf, blk), axis=0, keepdims=True)
        gs_rows.append(m1 + m2)
    gs = jnp.concatenate(gs_rows, axis=0)
    giota = lax.broadcasted_iota(I32, (N_GROUPS, tl), 0)
    sel = jnp.zeros((N_GROUPS, tl), jnp.bool_)
    for _ in range(TOPK_GROUPS):
        m = jnp.max(gs, axis=0, keepdims=True)
        gi = jnp.min(jnp.where(gs == m, giota, N_GROUPS), axis=0, keepdims=True)
        hit = giota == gi
        sel = jnp.logical_or(sel, hit)
        gs = jnp.where(hit, ninf, gs)
    self = jnp.where(sel, 1.0, 0.0)
    masked = jnp.concatenate(
        [jnp.where(self[g:g + 1, :] > 0.5, biased[g * eg:(g + 1) * eg, :], ninf) for g in range(N_GROUPS)], axis=0)
    eiota = lax.broadcasted_iota(I32, (N_EXPERTS, tl), 0)
    idx_rows, sc_rows = [], []
    multi = jnp.zeros((N_EXPERTS, tl), F32)
    for _ in range(TOP_K):
        m = jnp.max(masked, axis=0, keepdims=True)
        idx = jnp.min(jnp.where(masked == m, eiota, N_EXPERTS), axis=0, keepdims=True)
        hit = eiota == idx
        sc_rows.append(jnp.sum(jnp.where(hit, scores, 0.0), axis=0, keepdims=True))
        idx_rows.append(idx)
        multi = jnp.where(hit, 1.0, multi)
        masked = jnp.where(hit, ninf, masked)
    top_e = jnp.concatenate(idx_rows, axis=0)
    sc = jnp.concatenate(sc_rows, axis=0)
    e_ref[...] = top_e
    g_ref[...] = sc / jnp.sum(sc, axis=0, keepdims=True) * ROUTED_SCALE
    mh = multi.astype(BF16)
    before = (lax.broadcasted_iota(I32, (tl, tl), 0) < lax.broadcasted_iota(I32, (tl, tl), 1)).astype(BF16)
    running = cnt_scr[...]
    rankmat = _dot(mh, before) + jnp.concatenate([running] * (tl // 128), axis=1)
    r_rows = [jnp.sum(jnp.where(eiota == idx_rows[k], rankmat, 0.0), axis=0, keepdims=True) for k in range(TOP_K)]
    r_ref[...] = jnp.concatenate(r_rows, axis=0).astype(I32)
    total = running + _dot(mh, jnp.ones((tl, 128), BF16))
    cnt_scr[...] = total
    cnt_ref[...] = total


def _router(x2a, x2b, wrt, bias_col):
    d = x2a.shape[1]
    tl = TOKEN_TILE
    assert x2a.shape[0] % tl == 0 and x2b.shape[0] % tl == 0
    tiles_a = x2a.shape[0] // tl
    n = x2a.shape[0] + x2b.shape[0]
    kspec = pl.BlockSpec((TOP_K, tl), lambda i: (0, i))
    return pl.pallas_call(
        functools.partial(_router_kernel, tl=tl, tiles_a=tiles_a),
        grid=(n // tl,),
        in_specs=[*_two_part_specs(tl, d, tiles_a),
                  pl.BlockSpec((2, N_EXPERTS, d), lambda i: (0, 0, 0)),
                  pl.BlockSpec((N_EXPERTS, 1), lambda i: (0, 0))],
        out_specs=[kspec, kspec, kspec, pl.BlockSpec((N_EXPERTS, 128), lambda i: (0, 0))],
        out_shape=[jax.ShapeDtypeStruct((TOP_K, n), I32), jax.ShapeDtypeStruct((TOP_K, n), F32),
                   jax.ShapeDtypeStruct((TOP_K, n), I32), jax.ShapeDtypeStruct((N_EXPERTS, 128), F32)],
        scratch_shapes=[pltpu.VMEM((N_EXPERTS, 128), F32)],
        compiler_params=pltpu.CompilerParams(dimension_semantics=("arbitrary",), vmem_limit_bytes=VMEM_LIMIT),
        name="router",
    )(x2a, x2b, wrt, bias_col)


PACK_ROWS = D_MODEL // 2 // LANES
HI_MASK = 0xFFFF0000


def _slots_of(buf_ref, start, n=None):
    if n is None:
        return buf_ref.at[start, pl.ds(0, PACK_ROWS)]
    return buf_ref.at[pl.ds(start, n), pl.ds(0, PACK_ROWS)]


def _word_rows(buf_ref):
    if len(buf_ref.shape) == 3:
        return buf_ref.reshape(buf_ref.shape[0] * SUBLANES, LANES), SUBLANES
    return buf_ref, PACK_ROWS


def _pack_rows(x, dst_ref, row0=0):
    t = x.shape[0]
    bits = lax.bitcast_convert_type(x.astype(BF16).astype(F32), U32)
    flat, pitch = _word_rows(dst_ref)
    for j in range(PACK_ROWS):
        lo = lax.shift_right_logical(bits[:, j * LANES:(j + 1) * LANES], jnp.uint32(16))
        hi = bits[:, D_MODEL // 2 + j * LANES:D_MODEL // 2 + (j + 1) * LANES] & jnp.uint32(HI_MASK)
        flat[pl.ds(row0 * pitch + j, t, stride=pitch), :] = hi | lo


def _unpack_words(w):
    lo = lax.bitcast_convert_type(lax.shift_left(w, jnp.uint32(16)), F32)
    hi = lax.bitcast_convert_type(w & jnp.uint32(HI_MASK), F32)
    return lo, hi


def _unpack_rows(src_ref, t, row0=0):
    flat, pitch = _word_rows(src_ref)
    parts = [_unpack_words(flat[pl.ds(row0 * pitch + j, t, stride=pitch), :]) for j in range(PACK_ROWS)]
    return jnp.concatenate([p[0] for p in parts] + [p[1] for p in parts], axis=1)


def _dispatch_kernel(fs_ref, fl_ref, nu_ref, dest_ref, xa_ref, xb_ref, xs_ref, stage, zbuf, sem, zsem,
                     *, t, tm, tiles_a, n_blk):
    i = pl.program_id(0)
    nsteps = pl.num_programs(0)

    def zero_fill(start):
        def go(cp):
            if start:
                cp.start()
            else:
                cp.wait()

        def per_expert(e, carry):
            ln = fl_ref[e]
            off = fs_ref[e]
            bit = tm // 2
            while bit:
                @pl.when((ln & bit) != 0)
                def _(off=off, bit=bit):
                    go(pltpu.make_async_copy(_slots_of(zbuf, 0, bit), xs_ref.at[pl.ds(off, bit)], zsem))
                off = off + (ln & bit)
                bit //= 2
            return carry

        lax.fori_loop(0, N_EXPERTS, per_expert, 0)

        def per_block(b, carry):
            go(pltpu.make_async_copy(_slots_of(zbuf, 0, tm), xs_ref.at[pl.ds(b * tm, tm)], zsem))
            return carry

        lax.fori_loop(nu_ref[0], n_blk, per_block, 0)

    @pl.when(i == 0)
    def _():
        zbuf[...] = jnp.zeros_like(zbuf)
        zero_fill(True)

    def drain(slot):
        for k in range(TOP_K):
            pltpu.make_async_copy(_slots_of(stage.at[slot], 0, t), xs_ref.at[pl.ds(0, t)], sem.at[slot]).wait()

    def step(slot):
        @pl.when(i >= 2)
        def _():
            drain(slot)

        _pack_rows(jnp.where(i < tiles_a, xa_ref[...], xb_ref[...]), stage.at[slot])

        def body(n, carry):
            src = _slots_of(stage.at[slot], n)
            for k in range(TOP_K):
                dst = xs_ref.at[dest_ref[n * TOP_K + k]]
                pltpu.make_async_copy(src, dst, sem.at[slot]).start(priority=k % 2)
            return carry

        lax.fori_loop(0, t, body, 0)

    for slot in range(2):
        @pl.when(i % 2 == slot)
        def _(slot=slot):
            step(slot)

    @pl.when(i == 0)
    def _():
        zero_fill(False)

    @pl.when(i == nsteps - 1)
    def _():
        for slot in range(2):
            @pl.when(jnp.logical_or(nsteps >= 2, (nsteps - 1) % 2 == slot))
            def _(slot=slot):
                drain(slot)


def _dispatch(fill_start, fill_len, nu, dest, x2a, x2b, *, n_blk, tm):
    d = x2a.shape[1]
    t = TOKEN_TILE
    assert x2a.shape[0] % t == 0 and x2b.shape[0] % t == 0
    tiles_a = x2a.shape[0] // t
    n = x2a.shape[0] + x2b.shape[0]
    return pl.pallas_call(
        functools.partial(_dispatch_kernel, t=t, tm=tm, tiles_a=tiles_a, n_blk=n_blk),
        grid_spec=pltpu.PrefetchScalarGridSpec(
            num_scalar_prefetch=3,
            grid=(n // t,),
            in_specs=[pl.BlockSpec((t * TOP_K,), lambda i, *_: (i,), memory_space=pltpu.SMEM),
                      *_two_part_specs(t, d, tiles_a)],
            out_specs=pl.BlockSpec(memory_space=pl.ANY),
            scratch_shapes=[pltpu.VMEM((2, t, SUBLANES, LANES), U32), pltpu.VMEM((tm, SUBLANES, LANES), U32),
                            pltpu.SemaphoreType.DMA((2,)), pltpu.SemaphoreType.DMA(())],
        ),
        out_shape=jax.ShapeDtypeStruct((n_blk * tm, PACK_ROWS, LANES), U32),
        compiler_params=pltpu.CompilerParams(dimension_semantics=("arbitrary",), vmem_limit_bytes=VMEM_LIMIT),
        name="dispatch",
    )(fill_start, fill_len, nu, dest, x2a, x2b)


def _experts_kernel(b0_ref, nb_ref, cnt_ref, nu_ref, w1_ref, w3_ref, w2_ref, xs_ref, ys_ref,
                    w1b, w3b, w2b, xbuf, ybuf, in_sem, out_sem, *, tm, n_blk):
    e = pl.program_id(0)
    n_used = nu_ref[0]

    blk_rows = tm * PACK_ROWS

    def hbm_block(ref, b):
        return ref.at[pl.ds(pl.multiple_of(b * blk_rows, blk_rows), blk_rows)]

    def x_copy(b):
        s = lax.rem(b, X_DEPTH)
        return pltpu.make_async_copy(hbm_block(xs_ref, b), xbuf.at[s], in_sem.at[s])

    def y_copy(b):
        return pltpu.make_async_copy(ybuf.at[b % 2], hbm_block(ys_ref, b), out_sem.at[b % 2])

    @pl.when(e == 0)
    def _():
        for b in range(X_DEPTH - 1):
            @pl.when(b < n_used)
            def _(b=b):
                x_copy(b).start()

    w1b[...] = w1_ref[0].astype(BF16)
    w3b[...] = w3_ref[0].astype(BF16)
    w2b[...] = w2_ref[0].astype(BF16)
    b0 = b0_ref[e]
    cnt = cnt_ref[e]

    def block(j, carry):
        b = b0 + j
        x_copy(b).wait()

        @pl.when(b + X_DEPTH - 1 < n_used)
        def _():
            x_copy(b + X_DEPTH - 1).start()

        @pl.when(b >= 2)
        def _():
            y_copy(b - 2).wait()

        rows = tm // EXPERT_SPLIT
        for r0 in range(0, tm, rows):
            live = lax.broadcasted_iota(I32, (rows, 1), 0) < cnt - j * tm - r0
            x = jnp.where(live, _unpack_rows(xbuf.at[lax.rem(b, X_DEPTH)], rows, r0), 0.0).astype(BF16)
            a = _dot(x, w1b[...])
            h = (a * jax.nn.sigmoid(a)) * _dot(x, w3b[...])
            _pack_rows(_dot(h.astype(BF16), w2b[...]), ybuf.at[b % 2], r0)
        y_copy(b).start()
        return carry

    lax.fori_loop(0, nb_ref[e], block, 0)

    @pl.when(e == pl.num_programs(0) - 1)
    def _():
        @pl.when(n_used >= 2)
        def _():
            y_copy(n_used - 2).wait()

        y_copy(n_used - 1).wait()
        ybuf[0] = jnp.zeros(ybuf.shape[1:], U32)

        def zero_copy(b):
            return pltpu.make_async_copy(ybuf.at[0], hbm_block(ys_ref, b), out_sem.at[0])

        def zero(b, carry):
            zero_copy(b).start()
            return carry

        lax.fori_loop(n_used, n_blk, zero, 0)

        def zero_wait(b, carry):
            zero_copy(b).wait()
            return carry

        lax.fori_loop(n_used, n_blk, zero_wait, 0)


def _experts(blk0, nblk, counts, nu, xs, w1, w3, w2, *, tm):
    n_slot = xs.shape[0]
    n_blk = n_slot // tm
    n_exp, d, de = w1.shape
    wmap = lambda e, *_: (e, 0, 0)
    ys = pl.pallas_call(
        functools.partial(_experts_kernel, tm=tm, n_blk=n_blk),
        grid_spec=pltpu.PrefetchScalarGridSpec(
            num_scalar_prefetch=4,
            grid=(n_exp,),
            in_specs=[pl.BlockSpec((1, d, de), wmap), pl.BlockSpec((1, d, de), wmap), pl.BlockSpec((1, de, d), wmap),
                      pl.BlockSpec(memory_space=pl.ANY)],
            out_specs=pl.BlockSpec(memory_space=pl.ANY),
            scratch_shapes=[pltpu.VMEM((d, de), BF16), pltpu.VMEM((d, de), BF16), pltpu.VMEM((de, d), BF16),
                            pltpu.VMEM((X_DEPTH, tm * PACK_ROWS, LANES), U32), pltpu.VMEM((2, tm * PACK_ROWS, LANES), U32),
                            pltpu.SemaphoreType.DMA((X_DEPTH,)), pltpu.SemaphoreType.DMA((2,))],
        ),
        out_shape=jax.ShapeDtypeStruct((n_slot * PACK_ROWS, LANES), U32),
        compiler_params=pltpu.CompilerParams(dimension_semantics=("arbitrary",), vmem_limit_bytes=VMEM_LIMIT),
        name="experts",
    )(blk0, nblk, counts, nu, w1, w3, w2, xs.reshape(n_slot * PACK_ROWS, LANES))
    return ys.reshape(n_slot, PACK_ROWS, LANES)


def _combine_kernel(dcur_ref, dnxt_ref, x2_ref, g_ref, w1_ref, w3_ref, w2_ref, lg_ref, lb_ref, ys_ref,
                    out_ref, buf, acc_scr, sem, *, t, nsteps):
    i = pl.program_id(0)

    def fetch(d_ref, s, n):
        for k in range(TOP_K):
            pltpu.make_async_copy(ys_ref.at[d_ref[n * TOP_K + k]], _slots_of(buf.at[s, k], n),
                                  sem.at[s]).start(priority=k % 2)

    def drain(s):
        for k in range(TOP_K):
            pltpu.make_async_copy(ys_ref.at[pl.ds(0, t)], _slots_of(buf.at[s, k], 0, t), sem.at[s]).wait()

    @pl.when(i == 0)
    def _():
        def first(n, carry):
            fetch(dcur_ref, 0, n)
            return carry

        lax.fori_loop(0, t, first, 0)

    def step(s):
        drain(s)

        def group(gi, carry):
            r0 = pl.multiple_of(gi * SUBLANES, SUBLANES)
            for j in range(SUBLANES):
                fetch(dnxt_ref, 1 - s, r0 + j)
            gates = g_ref[pl.ds(r0, SUBLANES), :]
            gk = [jnp.broadcast_to(gates[:, k:k + 1], (SUBLANES, LANES)) for k in range(TOP_K)]
            planes = [buf.at[s, k].reshape(t * SUBLANES, LANES) for k in range(TOP_K)]
            for j in range(PACK_ROWS):
                acc_lo = acc_hi = None
                for k in range(TOP_K):
                    lo, hi = _unpack_words(planes[k][pl.ds(r0 * SUBLANES + j, SUBLANES, stride=SUBLANES), :])
                    acc_lo = gk[k] * lo if acc_lo is None else acc_lo + gk[k] * lo
                    acc_hi = gk[k] * hi if acc_hi is None else acc_hi + gk[k] * hi
                acc_scr[pl.ds(r0, SUBLANES), j * LANES:(j + 1) * LANES] = acc_lo
                acc_scr[pl.ds(r0, SUBLANES), D_MODEL // 2 + j * LANES:D_MODEL // 2 + (j + 1) * LANES] = acc_hi
            return carry

        lax.fori_loop(0, t // SUBLANES, group, 0)

    for s in range(2):
        @pl.when(i % 2 == s)
        def _(s=s):
            step(s)

    @pl.when(i == nsteps - 1)
    def _():
        drain(nsteps % 2)

    x2 = x2_ref[...]
    acc = acc_scr[...]
    xb = x2.astype(BF16)
    a = _dot(xb, w1_ref[...])
    hs = (a * jax.nn.sigmoid(a)) * _dot(xb, w3_ref[...])
    shared = _dot(hs.astype(BF16), w2_ref[...])
    out_ref[...] = _ln(x2 * ALPHA + (acc + shared), lg_ref[...], lb_ref[...])


def _combine(dest, x2, gate_t, w1s, w3s, w2s, lg, lb, ys, *, row_off):
    n_rows, d = x2.shape
    n = dest.shape[0] // TOP_K
    t = TOKEN_TILE
    assert row_off % t == 0 and n_rows % t == 0 and n % t == 0
    off = row_off // t
    last = n // t - 1
    de = w1s.shape[-1]
    const = lambda shape: pl.BlockSpec(shape, lambda i: (0,) * len(shape))
    return pl.pallas_call(
        functools.partial(_combine_kernel, t=t, nsteps=n_rows // t),
        grid=(n_rows // t,),
        in_specs=[pl.BlockSpec((t * TOP_K,), lambda i: (off + i,), memory_space=pltpu.SMEM),
                  pl.BlockSpec((t * TOP_K,), lambda i: (jnp.minimum(off + i + 1, last),), memory_space=pltpu.SMEM),
                  pl.BlockSpec((t, d), lambda i: (i, 0)),
                  pl.BlockSpec((t, TOP_K), lambda i: (off + i, 0)),
                  const((d, de)), const((d, de)), const((de, d)), const((1, d)), const((1, d)),
                  pl.BlockSpec(memory_space=pl.ANY)],
        out_specs=pl.BlockSpec((t, d), lambda i: (i, 0)),
        out_shape=jax.ShapeDtypeStruct((n_rows, d), F32),
        scratch_shapes=[pltpu.VMEM((2, TOP_K, t, SUBLANES, LANES), U32), pltpu.VMEM((t, d), F32),
                        pltpu.SemaphoreType.DMA((2,))],
        compiler_params=pltpu.CompilerParams(dimension_semantics=("arbitrary",), vmem_limit_bytes=VMEM_LIMIT),
        name="combine",
    )(dest, dest, x2, gate_t, w1s, w3s, w2s, lg, lb, ys)


EXPERT_TM = 512
EXPERT_SPLIT = 2
X_DEPTH = 3


def _moe_plan(cnt):
    tm = EXPERT_TM
    counts = cnt[:, 0].astype(I32)
    padded = (counts + tm - 1) // tm * tm
    pad_end = jnp.cumsum(padded)
    pad_start = pad_end - padded
    n_used = pad_end[-1] // tm
    fill_start = (pad_start + counts).astype(I32)
    fill_len = (padded - counts).astype(I32)
    return (pad_start.astype(I32), (pad_start // tm).astype(I32), (padded // tm).astype(I32), counts,
            n_used.reshape(1).astype(I32), fill_start, fill_len)


def _slots_kernel(ps_ref, e_ref, r_ref, d_ref):
    e = e_ref[...]

    def body(j, base):
        return jnp.where(e == j, ps_ref[j], base)

    base = lax.fori_loop(0, N_EXPERTS, body, jnp.zeros_like(e), unroll=8)
    d_ref[...] = base + r_ref[...]


def _slots(pad_start, top_e, rank):
    k, n = top_e.shape
    tl = next(c for c in (2048, 1536, 1024, 512, 256, 128) if n % c == 0)
    spec = pl.BlockSpec((k, tl), lambda i, ps: (0, i))
    return pl.pallas_call(
        _slots_kernel,
        grid_spec=pltpu.PrefetchScalarGridSpec(num_scalar_prefetch=1, grid=(n // tl,), in_specs=[spec, spec],
                                               out_specs=spec),
        out_shape=jax.ShapeDtypeStruct((k, n), I32),
        compiler_params=pltpu.CompilerParams(dimension_semantics=("arbitrary",)),
        name="slots",
    )(pad_start, top_e, rank)


PAST_LEN = 1024


def kernel(x_prompt, x_sample, state_gla, cache_pool, cache_mem_k, cache_mem_v, mem_prompt, ln_in_g, ln_in_b, w_in, w_gate_up, b_gate, gla_norm_g, pool_w, pool_scale, w_out, ln1_g, ln1_b, wq_mem, wk_mem, wv_mem, wo_mem, ln2_g, ln2_b, w_router, router_bias, w1_exp, w3_exp, w2_exp, w1_sh, w3_sh, w2_sh, ln3_g, ln3_b):
    assert w_in.shape[0] == 1, "single-layer trunk"
    bp, lp, d = x_prompt.shape
    bs, ls, _ = x_sample.shape
    n_p, n_s = bp * lp, bs * ls
    n_all = n_p + n_s

    wts_a = _prep_trunk_a_weights(ln_in_g, ln_in_b, w_in[0], w_gate_up[0], b_gate[0], gla_norm_g[0], pool_w[0],
                                  pool_scale[0], w_out[0], ln1_g[0], ln1_b[0])
    s0 = jnp.zeros((bp, GLA_HEADS, GLA_DK, GLA_DV), F32)
    h0 = jnp.zeros((bp, POOL_HIST, POOL_WIDTH), F32)
    x1p, sp, hp = _trunk_a(x_prompt, s0, h0, wts_a, start_pos=0)
    x1s, ss, hs = _trunk_a(x_sample, state_gla[0], cache_pool[0], wts_a, start_pos=PAST_LEN)

    mk, mv, mkb, mvb = _mem_kv(mem_prompt.reshape(bp * N_MEM, d), wk_mem[0].astype(BF16), wv_mem[0].astype(BF16))
    wq, wo = wq_mem[0].astype(BF16), wo_mem[0].astype(BF16)
    g2, b2 = ln2_g[0].reshape(1, d), ln2_b[0].reshape(1, d)
    x2p = _attn(x1p, mkb.reshape(bp, N_MEM, d), mvb.reshape(bp, N_MEM, d), wq, wo, g2, b2)
    x2s = _attn(x1s, cache_mem_k[0].reshape(bs, N_MEM, d).astype(BF16),
                cache_mem_v[0].reshape(bs, N_MEM, d).astype(BF16), wq, wo, g2, b2)

    wrt = w_router[0].T
    wrt_h = wrt.astype(BF16)
    wrt_m = (wrt - wrt_h.astype(F32)).astype(BF16)
    top_e, gate, rank, cnt = _router(x2p, x2s, jnp.stack([wrt_h, wrt_m]), router_bias[0].reshape(N_EXPERTS, 1))

    assert (n_all * TOP_K) % EXPERT_TM == 0
    n_blk = n_all * TOP_K // EXPERT_TM + N_EXPERTS
    pad_start, blk0, nblk, counts, nu, fill_start, fill_len = _moe_plan(cnt)
    dest = _slots(pad_start, top_e, rank).T.reshape(-1)
    xs = _dispatch(fill_start, fill_len, nu, dest, x2p, x2s, n_blk=n_blk, tm=EXPERT_TM)
    ys = _experts(blk0, nblk, counts, nu, xs, w1_exp[0], w3_exp[0], w2_exp[0], tm=EXPERT_TM)
    sh = (w1_sh[0].astype(BF16), w3_sh[0].astype(BF16), w2_sh[0].astype(BF16),
          ln3_g[0].reshape(1, d), ln3_b[0].reshape(1, d))
    gate_t = gate.T
    yp = _combine(dest, x2p, gate_t, *sh, ys, row_off=0)
    ysm = _combine(dest, x2s, gate_t, *sh, ys, row_off=n_p)

    return (yp.reshape(bp, lp, d), ysm.reshape(bs, ls, d), sp[None], hp[None],
            mk.reshape(1, bp, N_MEM, MEM_HEADS, MEM_DH), mv.reshape(1, bp, N_MEM, MEM_HEADS, MEM_DH),
            ss[None], hs[None])
```

```python
import functools

import jax
import jax.numpy as jnp
from jax import lax
from jax.experimental import pallas as pl
from jax.experimental.pallas import tpu as pltpu

F32 = jnp.float32
BF16 = jnp.bfloat16
I32 = jnp.int32
U32 = jnp.uint32

D_MODEL = 1024
CHUNK = 64
SUB = 16
GLA_HEADS = 4
GLA_DK = 64
GLA_DV = 128
GLA_KEY = GLA_HEADS * GLA_DK
GLA_WIDTH = GLA_HEADS * GLA_DV
GATE_RANK = 16
GATE_PAD = 128
POOL_WIDTH = 512
POOL_WINDOWS = (2, 4, 8, 16)
POOL_GC = 128
POOL_HIST = 15
HIST_ROWS = 16
N_MEM = 256
MEM_HEADS = 4
MEM_DH = 256
N_EXPERTS = 256
N_GROUPS = 8
EXPERTS_PER_GROUP = 32
TOPK_GROUPS = 4
TOP_K = 8
ROUTED_SCALE = 2.5
D_EXPERT = 256
TOKEN_TILE = 256
SUBLANES, LANES = 8, 128
assert D_MODEL == SUBLANES * LANES
ALPHA = 2.0 ** 0.25
EPS = 1e-5
OFF_Q, OFF_K, OFF_V, OFF_G, OFF_U, OFF_GD = 0, 256, 512, 1024, 1536, 2048
D_IN_PAD = OFF_GD + GATE_PAD

VMEM_LIMIT = 56 * 1024 * 1024


def _ln(x, g, b):
    mu = jnp.mean(x, axis=-1, keepdims=True)
    xc = x - mu
    var = jnp.mean(xc * xc, axis=-1, keepdims=True)
    return xc * lax.rsqrt(var + EPS) * g + b


def _dot(a, b):
    return jnp.dot(a, b, preferred_element_type=F32)


def _dot_nt(a, b):
    return lax.dot_general(a, b, (((1,), (1,)), ((), ())), preferred_element_type=F32)


def _dot_tn(a, b):
    return lax.dot_general(a, b, (((0,), (0,)), ((), ())), preferred_element_type=F32)


def _split3(x):
    h = x.astype(BF16)
    r = x - h.astype(F32)
    m = r.astype(BF16)
    l = (r - m.astype(F32)).astype(BF16)
    return h, m, l


def _trunk_a_kernel(x_ref, s0_ref, h0_ref, lng_ref, lnb_ref, win_ref, wgu_ref, bg_ref, gng_ref,
                    pw_ref, ps_ref, wout_ref, l1g_ref, l1b_ref, tri_ref,
                    x1_ref, sn_ref, hn_ref,
                    proj_scr, ext_scr, s_scr, op_scr, *, tl, chunk, start_pos):
    t = pl.program_id(1)
    nt = pl.num_programs(1)

    @pl.when(t == 0)
    def _():
        s_scr[...] = s0_ref[0]
        ext_scr[0:1, :] = jnp.zeros((1, POOL_WIDTH), F32)
        ext_scr[1:HIST_ROWS, :] = h0_ref[0]

    xn = _ln(x_ref[0], lng_ref[...], lnb_ref[...])
    proj_scr[...] = _dot(xn.astype(BF16), win_ref[...])

    c = chunk
    shift = lambda a, n: lax.shift_right_logical(a, n.bit_length() - 1)
    gd = proj_scr[:, OFF_GD:OFF_GD + GATE_PAD]
    z = _dot(gd.astype(BF16), wgu_ref[...]) + bg_ref[...]
    lf = (jnp.minimum(z, 0.0) - jnp.log1p(jnp.exp(-jnp.abs(z)))) * (1.0 / 16.0)
    tri = tri_ref[...]
    lh, lm, ll = _split3(lf)
    cum = _dot(tri, lh) + _dot(tri, lm) + _dot(tri, ll)
    q_all = proj_scr[:, OFF_Q:OFF_Q + GLA_KEY] * (GLA_DK ** -0.5)
    k_all = proj_scr[:, OFF_K:OFF_K + GLA_KEY]
    qs_all = q_all * jnp.exp(cum)

    causal = lax.broadcasted_iota(I32, (c, c), 1) <= lax.broadcasted_iota(I32, (c, c), 0)
    eye_dk = lax.broadcasted_iota(I32, (GLA_DK, GLA_DK), 0) == lax.broadcasted_iota(I32, (GLA_DK, GLA_DK), 1)
    n_sub = c // SUB
    rblk = shift(lax.broadcasted_iota(I32, (c, n_sub * GLA_DK), 0), SUB)
    lblk = shift(lax.broadcasted_iota(I32, (c, n_sub * GLA_DK), 1), GLA_DK)
    lblk_row = shift(lax.broadcasted_iota(I32, (1, n_sub * GLA_DK), 1), GLA_DK)
    mask_q = rblk == lblk
    mask_k = rblk <= lblk

    def tile_lanes(a):
        return jnp.concatenate([a] * n_sub, axis=1)

    states = [s_scr[h] for h in range(GLA_HEADS)]
    for ci in range(tl // c):
        rs = slice(ci * c, (ci + 1) * c)
        cum_c = cum[rs]
        last = cum_c[c - 1:c, :]
        ks_c = k_all[rs] * jnp.exp(last - cum_c)
        for h in range(GLA_HEADS):
            ksl = slice(h * GLA_DK, (h + 1) * GLA_DK)
            vsl = slice(h * GLA_DV, (h + 1) * GLA_DV)
            v_h = proj_scr[rs, OFF_V + h * GLA_DV:OFF_V + (h + 1) * GLA_DV].astype(BF16)
            cum_t = tile_lanes(cum_c[:, ksl])
            q_t = tile_lanes(q_all[rs, ksl])
            k_t = tile_lanes(k_all[rs, ksl])
            ref_row = jnp.zeros((1, n_sub * GLA_DK), F32)
            for i in range(1, n_sub):
                ref_row = jnp.where(lblk_row == i, cum_t[i * SUB - 1:i * SUB, :], ref_row)
            arg = cum_t - ref_row
            lhs = jnp.where(mask_q, q_t * jnp.exp(jnp.where(mask_q, arg, 0.0)), 0.0)
            rhs = jnp.where(mask_k, k_t * jnp.exp(jnp.where(mask_k, -arg, 0.0)), 0.0)
            att = jnp.where(causal, _dot_nt(lhs.astype(BF16), rhs.astype(BF16)), 0.0)
            s_h = states[h]
            o_h = _dot(att.astype(BF16), v_h) + _dot(qs_all[rs, ksl].astype(BF16), s_h.astype(BF16))
            dcol = jnp.sum(jnp.where(eye_dk, jnp.broadcast_to(jnp.exp(last[:, ksl]), (GLA_DK, GLA_DK)), 0.0),
                           axis=1, keepdims=True)
            states[h] = dcol * s_h + _dot_tn(ks_c[:, ksl].astype(BF16), v_h)
            o_h = o_h * lax.rsqrt(jnp.mean(o_h * o_h, axis=-1, keepdims=True) + EPS) * gng_ref[...]
            g_h = proj_scr[rs, OFF_G + h * GLA_DV:OFF_G + (h + 1) * GLA_DV]
            op_scr[rs, vsl] = (o_h * (g_h * jax.nn.sigmoid(g_h))).astype(BF16)
    for h in range(GLA_HEADS):
        s_scr[h] = states[h]

    u = proj_scr[:, OFF_U:OFF_U + POOL_WIDTH]
    ext_scr[HIST_ROWS:HIST_ROWS + tl, :] = u
    n_valid = start_pos + t * tl + lax.broadcasted_iota(I32, (tl, 1), 0) + 1
    for gi, w in enumerate(POOL_WINDOWS):
        lsl = slice(gi * POOL_GC, (gi + 1) * POOL_GC)
        win = ext_scr[HIST_ROWS:HIST_ROWS + tl, lsl]
        for s in range(1, w):
            win = win + ext_scr[HIST_ROWS - s:HIST_ROWS - s + tl, lsl]
        cnt = jnp.minimum(w, n_valid).astype(F32)
        r = win / cnt - u[:, lsl]
        p = _dot(r.astype(BF16), pw_ref[gi]) * ps_ref[:, lsl]
        op_scr[:, GLA_WIDTH + gi * POOL_GC:GLA_WIDTH + (gi + 1) * POOL_GC] = p.astype(BF16)
    tail = ext_scr[tl:tl + HIST_ROWS, :]
    ext_scr[0:HIST_ROWS, :] = tail

    mix = _dot(op_scr[...], wout_ref[...])
    x1_ref[0] = _ln(xn * ALPHA + mix, l1g_ref[...], l1b_ref[...])

    @pl.when(t == nt - 1)
    def _():
        sn_ref[0] = s_scr[...]
        hn_ref[0] = ext_scr[1:HIST_ROWS, :]


def _trunk_a(x, s0, h0, wts, *, start_pos):
    b, l, d = x.shape
    tl = min(l, 256)
    chunk = min(tl, CHUNK)
    assert l % tl == 0 and tl % chunk == 0 and chunk % SUB == 0 and l >= HIST_ROWS
    nt = l // tl
    kern = functools.partial(_trunk_a_kernel, tl=tl, chunk=chunk, start_pos=start_pos)
    const = lambda shape: pl.BlockSpec(shape, lambda bi, ti: (0,) * len(shape))
    pos = jnp.arange(tl)
    tri = ((pos[None, :] <= pos[:, None]) & (pos[None, :] // chunk == pos[:, None] // chunk)).astype(BF16)
    return pl.pallas_call(
        kern,
        grid=(b, nt),
        in_specs=[
            pl.BlockSpec((1, tl, d), lambda bi, ti: (bi, ti, 0)),
            pl.BlockSpec((1, GLA_HEADS, GLA_DK, GLA_DV), lambda bi, ti: (bi, 0, 0, 0)),
            pl.BlockSpec((1, POOL_HIST, POOL_WIDTH), lambda bi, ti: (bi, 0, 0)),
            const((1, d)), const((1, d)),
            const((d, D_IN_PAD)), const((GATE_PAD, GLA_KEY)), const((1, GLA_KEY)), const((1, GLA_DV)),
            const((len(POOL_WINDOWS), POOL_GC, POOL_GC)), const((1, POOL_WIDTH)),
            const((GLA_WIDTH + POOL_WIDTH, d)), const((1, d)), const((1, d)), const((tl, tl)),
        ],
        out_specs=[
            pl.BlockSpec((1, tl, d), lambda bi, ti: (bi, ti, 0)),
            pl.BlockSpec((1, GLA_HEADS, GLA_DK, GLA_DV), lambda bi, ti: (bi, 0, 0, 0)),
            pl.BlockSpec((1, POOL_HIST, POOL_WIDTH), lambda bi, ti: (bi, 0, 0)),
        ],
        out_shape=[
            jax.ShapeDtypeStruct((b, l, d), F32),
            jax.ShapeDtypeStruct((b, GLA_HEADS, GLA_DK, GLA_DV), F32),
            jax.ShapeDtypeStruct((b, POOL_HIST, POOL_WIDTH), F32),
        ],
        scratch_shapes=[
            pltpu.VMEM((tl, D_IN_PAD), F32),
            pltpu.VMEM((HIST_ROWS + tl, POOL_WIDTH), F32),
            pltpu.VMEM((GLA_HEADS, GLA_DK, GLA_DV), F32),
            pltpu.VMEM((tl, GLA_WIDTH + POOL_WIDTH), BF16),
        ],
        compiler_params=pltpu.CompilerParams(
            dimension_semantics=("arbitrary", "arbitrary"), vmem_limit_bytes=VMEM_LIMIT),
        name="trunk_a",
    )(x, s0, h0, *wts, tri)


def _prep_trunk_a_weights(ln_in_g, ln_in_b, w_in, w_gate_up, b_gate, gla_norm_g, pool_w, pool_scale, w_out,
                          ln1_g, ln1_b):
    d = D_MODEL
    p_gd = 2 * GLA_KEY + 2 * GLA_WIDTH
    w_in_r = jnp.concatenate(
        [w_in[:, :p_gd], w_in[:, p_gd + GATE_RANK:], w_in[:, p_gd:p_gd + GATE_RANK],
         jnp.zeros((d, GATE_PAD - GATE_RANK), w_in.dtype)], axis=1).astype(BF16)
    wgu = jnp.concatenate([w_gate_up, jnp.zeros((GATE_PAD - GATE_RANK, GLA_KEY), w_gate_up.dtype)],
                          axis=0).astype(BF16)
    return (ln_in_g.reshape(1, d), ln_in_b.reshape(1, d), w_in_r, wgu, b_gate.reshape(1, GLA_KEY),
            gla_norm_g.reshape(1, GLA_DV), pool_w.astype(BF16), pool_scale.reshape(1, POOL_WIDTH),
            w_out.astype(BF16), ln1_g.reshape(1, d), ln1_b.reshape(1, d))


def _mem_kv_kernel(m_ref, wk_ref, wv_ref, k_ref, v_ref, kb_ref, vb_ref):
    m = m_ref[...].astype(BF16)
    k = _dot(m, wk_ref[...])
    v = _dot(m, wv_ref[...])
    k_ref[...] = k
    v_ref[...] = v
    kb_ref[...] = k.astype(BF16)
    vb_ref[...] = v.astype(BF16)


def _mem_kv(mem, wk, wv):
    m, d = mem.shape
    tm = min(m, 512)
    assert m % tm == 0
    row = pl.BlockSpec((tm, d), lambda i: (i, 0))
    wspec = pl.BlockSpec((d, d), lambda i: (0, 0))
    return pl.pallas_call(
        _mem_kv_kernel,
        grid=(m // tm,),
        in_specs=[row, wspec, wspec],
        out_specs=[row, row, row, row],
        out_shape=[jax.ShapeDtypeStruct((m, d), F32), jax.ShapeDtypeStruct((m, d), F32),
                   jax.ShapeDtypeStruct((m, d), BF16), jax.ShapeDtypeStruct((m, d), BF16)],
        compiler_params=pltpu.CompilerParams(dimension_semantics=("arbitrary",), vmem_limit_bytes=VMEM_LIMIT),
        name="mem_kv",
    )(mem, wk, wv)


def _attn_kernel(x1_ref, k_ref, v_ref, wq_ref, wo_ref, g_ref, b_ref, x2_ref, o_scr):
    x1 = x1_ref[0]
    q = (_dot(x1.astype(BF16), wq_ref[...]) * (MEM_DH ** -0.5)).astype(BF16)
    for h in range(MEM_HEADS):
        hs = slice(h * MEM_DH, (h + 1) * MEM_DH)
        s = _dot_nt(q[:, hs], k_ref[0, :, hs])
        e = jnp.exp(s - jnp.max(s, axis=-1, keepdims=True))
        p = e / jnp.sum(e, axis=-1, keepdims=True)
        o_scr[:, hs] = _dot(p.astype(BF16), v_ref[0, :, hs]).astype(BF16)
    attn = _dot(o_scr[...], wo_ref[...])
    x2_ref[...] = _ln(x1 * ALPHA + attn, g_ref[...], b_ref[...])


def _attn(x1, mem_k, mem_v, wq, wo, g, b):
    bsz, l, d = x1.shape
    tl = min(l, 256)
    assert l % tl == 0
    nt = l // tl
    const = lambda shape: pl.BlockSpec(shape, lambda bi, ti: (0,) * len(shape))
    return pl.pallas_call(
        _attn_kernel,
        grid=(bsz, nt),
        in_specs=[
            pl.BlockSpec((1, tl, d), lambda bi, ti: (bi, ti, 0)),
            pl.BlockSpec((1, N_MEM, d), lambda bi, ti: (bi, 0, 0)),
            pl.BlockSpec((1, N_MEM, d), lambda bi, ti: (bi, 0, 0)),
            const((d, d)), const((d, d)), const((1, d)), const((1, d)),
        ],
        out_specs=pl.BlockSpec((tl, d), lambda bi, ti: (bi * nt + ti, 0)),
        out_shape=jax.ShapeDtypeStruct((bsz * l, d), F32),
        scratch_shapes=[pltpu.VMEM((tl, d), BF16)],
        compiler_params=pltpu.CompilerParams(
            dimension_semantics=("arbitrary", "arbitrary"), vmem_limit_bytes=VMEM_LIMIT),
        name="attn",
    )(x1, mem_k, mem_v, wq, wo, g, b)


def _two_part_specs(t, d, tiles_a):
    spec_a = pl.BlockSpec((t, d), lambda i, *_: (jnp.minimum(i, tiles_a - 1), 0))
    spec_b = pl.BlockSpec((t, d), lambda i, *_: (jnp.maximum(i - tiles_a, 0), 0))
    return spec_a, spec_b


def _router_kernel(xa_ref, xb_ref, wrt_ref, bias_ref, e_ref, g_ref, r_ref, cnt_ref, cnt_scr, *, tl, tiles_a):
    i = pl.program_id(0)

    @pl.when(i == 0)
    def _():
        cnt_scr[...] = jnp.zeros_like(cnt_scr)

    x = jnp.where(i < tiles_a, xa_ref[...], xb_ref[...])
    xh = x.astype(BF16)
    xm = (x - xh.astype(F32)).astype(BF16)
    wh = wrt_ref[0]
    wm = wrt_ref[1]
    logits = _dot_nt(wh, xh) + (_dot_nt(wh, xm) + _dot_nt(wm, xh))
    scores = jax.nn.sigmoid(logits)
    biased = scores + bias_ref[...]
    ninf = -jnp.inf
    eg = EXPERTS_PER_GROUP
    riota = lax.broadcasted_iota(I32, (eg, tl), 0)
    gs_rows = []
    for g in range(N_GROUPS):
        blk = biased[g * eg:(g + 1) * eg, :]
        m1 = jnp.max(blk, axis=0, keepdims=True)
        i1 = jnp.min(jnp.where(blk == m1, riota, eg), axis=0, keepdims=True)
        m2 = jnp.max(jnp.where(riota == i1, ninf, blk), axis=0, keepdims=True)
        gs_rows.append(m1 + m2)
    gs = jnp.concatenate(gs_rows, axis=0)
    giota = lax.broadcasted_iota(I32, (N_GROUPS, tl), 0)
    sel = jnp.zeros((N_GROUPS, tl), jnp.bool_)
    for _ in range(TOPK_GROUPS):
        m = jnp.max(gs, axis=0, keepdims=True)
        gi = jnp.min(jnp.where(gs == m, giota, N_GROUPS), axis=0, keepdims=True)
        hit = giota == gi
        sel = jnp.logical_or(sel, hit)
        gs = jnp.where(hit, ninf, gs)
    self = jnp.where(sel, 1.0, 0.0)
    masked = jnp.concatenate(
        [jnp.where(self[g:g + 1, :] > 0.5, biased[g * eg:(g + 1) * eg, :], ninf) for g in range(N_GROUPS)], axis=0)
    eiota = lax.broadcasted_iota(I32, (N_EXPERTS, tl), 0)
    idx_rows, sc_rows = [], []
    multi = jnp.zeros((N_EXPERTS, tl), F32)
    for _ in range(TOP_K):
        m = jnp.max(masked, axis=0, keepdims=True)
        idx = jnp.min(jnp.where(masked == m, eiota, N_EXPERTS), axis=0, keepdims=True)
        hit = eiota == idx
        sc_rows.append(jnp.sum(jnp.where(hit, scores, 0.0), axis=0, keepdims=True))
        idx_rows.append(idx)
        multi = jnp.where(hit, 1.0, multi)
        masked = jnp.where(hit, ninf, masked)
    top_e = jnp.concatenate(idx_rows, axis=0)
    sc = jnp.concatenate(sc_rows, axis=0)
    e_ref[...] = top_e
    g_ref[...] = sc / jnp.sum(sc, axis=0, keepdims=True) * ROUTED_SCALE
    mh = multi.astype(BF16)
    before = (lax.broadcasted_iota(I32, (tl, tl), 0) < lax.broadcasted_iota(I32, (tl, tl), 1)).astype(BF16)
    running = cnt_scr[...]
    rankmat = _dot(mh, before) + jnp.concatenate([running] * (tl // 128), axis=1)
    r_rows = [jnp.sum(jnp.where(eiota == idx_rows[k], rankmat, 0.0), axis=0, keepdims=True) for k in range(TOP_K)]
    r_ref[...] = jnp.concatenate(r_rows, axis=0).astype(I32)
    total = running + _dot(mh, jnp.ones((tl, 128), BF16))
    cnt_scr[...] = total
    cnt_ref[...] = total


def _router(x2a, x2b, wrt, bias_col):
    d = x2a.shape[1]
    tl = TOKEN_TILE
    assert x2a.shape[0] % tl == 0 and x2b.shape[0] % tl == 0
    tiles_a = x2a.shape[0] // tl
    n = x2a.shape[0] + x2b.shape[0]
    kspec = pl.BlockSpec((TOP_K, tl), lambda i: (0, i))
    return pl.pallas_call(
        functools.partial(_router_kernel, tl=tl, tiles_a=tiles_a),
        grid=(n // tl,),
        in_specs=[*_two_part_specs(tl, d, tiles_a),
                  pl.BlockSpec((2, N_EXPERTS, d), lambda i: (0, 0, 0)),
                  pl.BlockSpec((N_EXPERTS, 1), lambda i: (0, 0))],
        out_specs=[kspec, kspec, kspec, pl.BlockSpec((N_EXPERTS, 128), lambda i: (0, 0))],
        out_shape=[jax.ShapeDtypeStruct((TOP_K, n), I32), jax.ShapeDtypeStruct((TOP_K, n), F32),
                   jax.ShapeDtypeStruct((TOP_K, n), I32), jax.ShapeDtypeStruct((N_EXPERTS, 128), F32)],
        scratch_shapes=[pltpu.VMEM((N_EXPERTS, 128), F32)],
        compiler_params=pltpu.CompilerParams(dimension_semantics=("arbitrary",), vmem_limit_bytes=VMEM_LIMIT),
        name="router",
    )(x2a, x2b, wrt, bias_col)


PACK_ROWS = D_MODEL // 2 // LANES
HI_MASK = 0xFFFF0000


def _slots_of(buf_ref, start, n=None):
    if n is None:
        return buf_ref.at[start, pl.ds(0, PACK_ROWS)]
    return buf_ref.at[pl.ds(start, n), pl.ds(0, PACK_ROWS)]


def _word_rows(buf_ref):
    if len(buf_ref.shape) == 3:
        return buf_ref.reshape(buf_ref.shape[0] * SUBLANES, LANES), SUBLANES
    return buf_ref, PACK_ROWS


def _pack_rows(x, dst_ref, row0=0):
    t = x.shape[0]
    bits = lax.bitcast_convert_type(x.astype(BF16).astype(F32), U32)
    flat, pitch = _word_rows(dst_ref)
    for j in range(PACK_ROWS):
        lo = lax.shift_right_logical(bits[:, j * LANES:(j + 1) * LANES], jnp.uint32(16))
        hi = bits[:, D_MODEL // 2 + j * LANES:D_MODEL // 2 + (j + 1) * LANES] & jnp.uint32(HI_MASK)
        flat[pl.ds(row0 * pitch + j, t, stride=pitch), :] = hi | lo


def _unpack_words(w):
    lo = lax.bitcast_convert_type(lax.shift_left(w, jnp.uint32(16)), F32)
    hi = lax.bitcast_convert_type(w & jnp.uint32(HI_MASK), F32)
    return lo, hi


def _unpack_rows(src_ref, t, row0=0):
    flat, pitch = _word_rows(src_ref)
    parts = [_unpack_words(flat[pl.ds(row0 * pitch + j, t, stride=pitch), :]) for j in range(PACK_ROWS)]
    return jnp.concatenate([p[0] for p in parts] + [p[1] for p in parts], axis=1)


def _dispatch_kernel(fs_ref, fl_ref, nu_ref, dest_ref, xa_ref, xb_ref, xs_ref, stage, zbuf, sem, zsem,
                     *, t, tm, tiles_a, n_blk):
    i = pl.program_id(0)
    nsteps = pl.num_programs(0)

    def zero_fill(start):
        def go(cp):
            if start:
                cp.start()
            else:
                cp.wait()

        def per_expert(e, carry):
            ln = fl_ref[e]
            off = fs_ref[e]
            bit = tm // 2
            while bit:
                @pl.when((ln & bit) != 0)
                def _(off=off, bit=bit):
                    go(pltpu.make_async_copy(_slots_of(zbuf, 0, bit), xs_ref.at[pl.ds(off, bit)], zsem))
                off = off + (ln & bit)
                bit //= 2
            return carry

        lax.fori_loop(0, N_EXPERTS, per_expert, 0)

        def per_block(b, carry):
            go(pltpu.make_async_copy(_slots_of(zbuf, 0, tm), xs_ref.at[pl.ds(b * tm, tm)], zsem))
            return carry

        lax.fori_loop(nu_ref[0], n_blk, per_block, 0)

    @pl.when(i == 0)
    def _():
        zbuf[...] = jnp.zeros_like(zbuf)
        zero_fill(True)

    def drain(slot):
        for k in range(TOP_K):
            pltpu.make_async_copy(_slots_of(stage.at[slot], 0, t), xs_ref.at[pl.ds(0, t)], sem.at[slot]).wait()

    def step(slot):
        @pl.when(i >= 2)
        def _():
            drain(slot)

        _pack_rows(jnp.where(i < tiles_a, xa_ref[...], xb_ref[...]), stage.at[slot])

        def body(n, carry):
            src = _slots_of(stage.at[slot], n)
            for k in range(TOP_K):
                dst = xs_ref.at[dest_ref[n * TOP_K + k]]
                pltpu.make_async_copy(src, dst, sem.at[slot]).start(priority=k % 2)
            return carry

        lax.fori_loop(0, t, body, 0)

    for slot in range(2):
        @pl.when(i % 2 == slot)
        def _(slot=slot):
            step(slot)

    @pl.when(i == 0)
    def _():
        zero_fill(False)

    @pl.when(i == nsteps - 1)
    def _():
        for slot in range(2):
            @pl.when(jnp.logical_or(nsteps >= 2, (nsteps - 1) % 2 == slot))
            def _(slot=slot):
                drain(slot)


def _dispatch(fill_start, fill_len, nu, dest, x2a, x2b, *, n_blk, tm):
    d = x2a.shape[1]
    t = TOKEN_TILE
    assert x2a.shape[0] % t == 0 and x2b.shape[0] % t == 0
    tiles_a = x2a.shape[0] // t
    n = x2a.shape[0] + x2b.shape[0]
    return pl.pallas_call(
        functools.partial(_dispatch_kernel, t=t, tm=tm, tiles_a=tiles_a, n_blk=n_blk),
        grid_spec=pltpu.PrefetchScalarGridSpec(
            num_scalar_prefetch=3,
            grid=(n // t,),
            in_specs=[pl.BlockSpec((t * TOP_K,), lambda i, *_: (i,), memory_space=pltpu.SMEM),
                      *_two_part_specs(t, d, tiles_a)],
            out_specs=pl.BlockSpec(memory_space=pl.ANY),
            scratch_shapes=[pltpu.VMEM((2, t, SUBLANES, LANES), U32), pltpu.VMEM((tm, SUBLANES, LANES), U32),
                            pltpu.SemaphoreType.DMA((2,)), pltpu.SemaphoreType.DMA(())],
        ),
        out_shape=jax.ShapeDtypeStruct((n_blk * tm, PACK_ROWS, LANES), U32),
        compiler_params=pltpu.CompilerParams(dimension_semantics=("arbitrary",), vmem_limit_bytes=VMEM_LIMIT),
        name="dispatch",
    )(fill_start, fill_len, nu, dest, x2a, x2b)


def _experts_kernel(b0_ref, nb_ref, cnt_ref, nu_ref, w1_ref, w3_ref, w2_ref, xs_ref, ys_ref,
                    w1b, w3b, w2b, xbuf, ybuf, in_sem, out_sem, *, tm, n_blk):
    e = pl.program_id(0)
    n_used = nu_ref[0]

    blk_rows = tm * PACK_ROWS

    def hbm_block(ref, b):
        return ref.at[pl.ds(pl.multiple_of(b * blk_rows, blk_rows), blk_rows)]

    def x_copy(b):
        s = lax.rem(b, X_DEPTH)
        return pltpu.make_async_copy(hbm_block(xs_ref, b), xbuf.at[s], in_sem.at[s])

    def y_copy(b):
        return pltpu.make_async_copy(ybuf.at[b % 2], hbm_block(ys_ref, b), out_sem.at[b % 2])

    @pl.when(e == 0)
    def _():
        for b in range(X_DEPTH - 1):
            @pl.when(b < n_used)
            def _(b=b):
                x_copy(b).start()

    w1b[...] = w1_ref[0].astype(BF16)
    w3b[...] = w3_ref[0].astype(BF16)
    w2b[...] = w2_ref[0].astype(BF16)
    b0 = b0_ref[e]
    cnt = cnt_ref[e]

    def block(j, carry):
        b = b0 + j
        x_copy(b).wait()

        @pl.when(b + X_DEPTH - 1 < n_used)
        def _():
            x_copy(b + X_DEPTH - 1).start()

        @pl.when(b >= 2)
        def _():
            y_copy(b - 2).wait()

        rows = tm // EXPERT_SPLIT
        for r0 in range(0, tm, rows):
            live = lax.broadcasted_iota(I32, (rows, 1), 0) < cnt - j * tm - r0
            x = jnp.where(live, _unpack_rows(xbuf.at[lax.rem(b, X_DEPTH)], rows, r0), 0.0).astype(BF16)
            a = _dot(x, w1b[...])
            h = (a * jax.nn.sigmoid(a)) * _dot(x, w3b[...])
            _pack_rows(_dot(h.astype(BF16), w2b[...]), ybuf.at[b % 2], r0)
        y_copy(b).start()
        return carry

    lax.fori_loop(0, nb_ref[e], block, 0)

    @pl.when(e == pl.num_programs(0) - 1)
    def _():
        @pl.when(n_used >= 2)
        def _():
            y_copy(n_used - 2).wait()

        y_copy(n_used - 1).wait()
        ybuf[0] = jnp.zeros(ybuf.shape[1:], U32)

        def zero_copy(b):
            return pltpu.make_async_copy(ybuf.at[0], hbm_block(ys_ref, b), out_sem.at[0])

        def zero(b, carry):
            zero_copy(b).start()
            return carry

        lax.fori_loop(n_used, n_blk, zero, 0)

        def zero_wait(b, carry):
            zero_copy(b).wait()
            return carry

        lax.fori_loop(n_used, n_blk, zero_wait, 0)


def _experts(blk0, nblk, counts, nu, xs, w1, w3, w2, *, tm):
    n_slot = xs.shape[0]
    n_blk = n_slot // tm
    n_exp, d, de = w1.shape
    wmap = lambda e, *_: (e, 0, 0)
    ys = pl.pallas_call(
        functools.partial(_experts_kernel, tm=tm, n_blk=n_blk),
        grid_spec=pltpu.PrefetchScalarGridSpec(
            num_scalar_prefetch=4,
            grid=(n_exp,),
            in_specs=[pl.BlockSpec((1, d, de), wmap), pl.BlockSpec((1, d, de), wmap), pl.BlockSpec((1, de, d), wmap),
                      pl.BlockSpec(memory_space=pl.ANY)],
            out_specs=pl.BlockSpec(memory_space=pl.ANY),
            scratch_shapes=[pltpu.VMEM((d, de), BF16), pltpu.VMEM((d, de), BF16), pltpu.VMEM((de, d), BF16),
                            pltpu.VMEM((X_DEPTH, tm * PACK_ROWS, LANES), U32), pltpu.VMEM((2, tm * PACK_ROWS, LANES), U32),
                            pltpu.SemaphoreType.DMA((X_DEPTH,)), pltpu.SemaphoreType.DMA((2,))],
        ),
        out_shape=jax.ShapeDtypeStruct((n_slot * PACK_ROWS, LANES), U32),
        compiler_params=pltpu.CompilerParams(dimension_semantics=("arbitrary",), vmem_limit_bytes=VMEM_LIMIT),
        name="experts",
    )(blk0, nblk, counts, nu, w1, w3, w2, xs.reshape(n_slot * PACK_ROWS, LANES))
    return ys.reshape(n_slot, PACK_ROWS, LANES)


def _combine_kernel(dcur_ref, dnxt_ref, x2_ref, g_ref, w1_ref, w3_ref, w2_ref, lg_ref, lb_ref, ys_ref,
                    out_ref, buf, acc_scr, sem, *, t, nsteps):
    i = pl.program_id(0)

    def fetch(d_ref, s, n):
        for k in range(TOP_K):
            pltpu.make_async_copy(ys_ref.at[d_ref[n * TOP_K + k]], _slots_of(buf.at[s, k], n),
                                  sem.at[s]).start(priority=k % 2)

    def drain(s):
        for k in range(TOP_K):
            pltpu.make_async_copy(ys_ref.at[pl.ds(0, t)], _slots_of(buf.at[s, k], 0, t), sem.at[s]).wait()

    @pl.when(i == 0)
    def _():
        def first(n, carry):
            fetch(dcur_ref, 0, n)
            return carry

        lax.fori_loop(0, t, first, 0)

    def step(s):
        drain(s)

        def group(gi, carry):
            r0 = pl.multiple_of(gi * SUBLANES, SUBLANES)
            for j in range(SUBLANES):
                fetch(dnxt_ref, 1 - s, r0 + j)
            gates = g_ref[pl.ds(r0, SUBLANES), :]
            gk = [jnp.broadcast_to(gates[:, k:k + 1], (SUBLANES, LANES)) for k in range(TOP_K)]
            planes = [buf.at[s, k].reshape(t * SUBLANES, LANES) for k in range(TOP_K)]
            for j in range(PACK_ROWS):
                acc_lo = acc_hi = None
                for k in range(TOP_K):
                    lo, hi = _unpack_words(planes[k][pl.ds(r0 * SUBLANES + j, SUBLANES, stride=SUBLANES), :])
                    acc_lo = gk[k] * lo if acc_lo is None else acc_lo + gk[k] * lo
                    acc_hi = gk[k] * hi if acc_hi is None else acc_hi + gk[k] * hi
                acc_scr[pl.ds(r0, SUBLANES), j * LANES:(j + 1) * LANES] = acc_lo
                acc_scr[pl.ds(r0, SUBLANES), D_MODEL // 2 + j * LANES:D_MODEL // 2 + (j + 1) * LANES] = acc_hi
            return carry

        lax.fori_loop(0, t // SUBLANES, group, 0)

    for s in range(2):
        @pl.when(i % 2 == s)
        def _(s=s):
            step(s)

    @pl.when(i == nsteps - 1)
    def _():
        drain(nsteps % 2)

    x2 = x2_ref[...]
    acc = acc_scr[...]
    xb = x2.astype(BF16)
    a = _dot(xb, w1_ref[...])
    hs = (a * jax.nn.sigmoid(a)) * _dot(xb, w3_ref[...])
    shared = _dot(hs.astype(BF16), w2_ref[...])
    out_ref[...] = _ln(x2 * ALPHA + (acc + shared), lg_ref[...], lb_ref[...])


def _combine(dest, x2, gate_t, w1s, w3s, w2s, lg, lb, ys, *, row_off):
    n_rows, d = x2.shape
    n = dest.shape[0] // TOP_K
    t = TOKEN_TILE
    assert row_off % t == 0 and n_rows % t == 0 and n % t == 0
    off = row_off // t
    last = n // t - 1
    de = w1s.shape[-1]
    const = lambda shape: pl.BlockSpec(shape, lambda i: (0,) * len(shape))
    return pl.pallas_call(
        functools.partial(_combine_kernel, t=t, nsteps=n_rows // t),
        grid=(n_rows // t,),
        in_specs=[pl.BlockSpec((t * TOP_K,), lambda i: (off + i,), memory_space=pltpu.SMEM),
                  pl.BlockSpec((t * TOP_K,), lambda i: (jnp.minimum(off + i + 1, last),), memory_space=pltpu.SMEM),
                  pl.BlockSpec((t, d), lambda i: (i, 0)),
                  pl.BlockSpec((t, TOP_K), lambda i: (off + i, 0)),
                  const((d, de)), const((d, de)), const((de, d)), const((1, d)), const((1, d)),
                  pl.BlockSpec(memory_space=pl.ANY)],
        out_specs=pl.BlockSpec((t, d), lambda i: (i, 0)),
        out_shape=jax.ShapeDtypeStruct((n_rows, d), F32),
        scratch_shapes=[pltpu.VMEM((2, TOP_K, t, SUBLANES, LANES), U32), pltpu.VMEM((t, d), F32),
                        pltpu.SemaphoreType.DMA((2,))],
        compiler_params=pltpu.CompilerParams(dimension_semantics=("arbitrary",), vmem_limit_bytes=VMEM_LIMIT),
        name="combine",
    )(dest, dest, x2, gate_t, w1s, w3s, w2s, lg, lb, ys)


EXPERT_TM = 512
EXPERT_SPLIT = 2
X_DEPTH = 4


def _moe_plan(cnt):
    tm = EXPERT_TM
    counts = cnt[:, 0].astype(I32)
    padded = (counts + tm - 1) // tm * tm
    pad_end = jnp.cumsum(padded)
    pad_start = pad_end - padded
    n_used = pad_end[-1] // tm
    fill_start = (pad_start + counts).astype(I32)
    fill_len = (padded - counts).astype(I32)
    return (pad_start.astype(I32), (pad_start // tm).astype(I32), (padded // tm).astype(I32), counts,
            n_used.reshape(1).astype(I32), fill_start, fill_len)


def _slots_kernel(ps_ref, e_ref, r_ref, d_ref):
    e = e_ref[...]

    def body(j, base):
        return jnp.where(e == j, ps_ref[j], base)

    base = lax.fori_loop(0, N_EXPERTS, body, jnp.zeros_like(e), unroll=8)
    d_ref[...] = base + r_ref[...]


def _slots(pad_start, top_e, rank):
    k, n = top_e.shape
    tl = next(c for c in (2048, 1536, 1024, 512, 256, 128) if n % c == 0)
    spec = pl.BlockSpec((k, tl), lambda i, ps: (0, i))
    return pl.pallas_call(
        _slots_kernel,
        grid_spec=pltpu.PrefetchScalarGridSpec(num_scalar_prefetch=1, grid=(n // tl,), in_specs=[spec, spec],
                                               out_specs=spec),
        out_shape=jax.ShapeDtypeStruct((k, n), I32),
        compiler_params=pltpu.CompilerParams(dimension_semantics=("arbitrary",)),
        name="slots",
    )(pad_start, top_e, rank)


PAST_LEN = 1024


def kernel(x_prompt, x_sample, state_gla, cache_pool, cache_mem_k, cache_mem_v, mem_prompt, ln_in_g, ln_in_b, w_in, w_gate_up, b_gate, gla_norm_g, pool_w, pool_scale, w_out, ln1_g, ln1_b, wq_mem, wk_mem, wv_mem, wo_mem, ln2_g, ln2_b, w_router, router_bias, w1_exp, w3_exp, w2_exp, w1_sh, w3_sh, w2_sh, ln3_g, ln3_b):
    assert w_in.shape[0] == 1, "single-layer trunk"
    bp, lp, d = x_prompt.shape
    bs, ls, _ = x_sample.shape
    n_p, n_s = bp * lp, bs * ls
    n_all = n_p + n_s

    wts_a = _prep_trunk_a_weights(ln_in_g, ln_in_b, w_in[0], w_gate_up[0], b_gate[0], gla_norm_g[0], pool_w[0],
                                  pool_scale[0], w_out[0], ln1_g[0], ln1_b[0])
    s0 = jnp.zeros((bp, GLA_HEADS, GLA_DK, GLA_DV), F32)
    h0 = jnp.zeros((bp, POOL_HIST, POOL_WIDTH), F32)
    x1p, sp, hp = _trunk_a(x_prompt, s0, h0, wts_a, start_pos=0)
    x1s, ss, hs = _trunk_a(x_sample, state_gla[0], cache_pool[0], wts_a, start_pos=PAST_LEN)

    mk, mv, mkb, mvb = _mem_kv(mem_prompt.reshape(bp * N_MEM, d), wk_mem[0].astype(BF16), wv_mem[0].astype(BF16))
    wq, wo = wq_mem[0].astype(BF16), wo_mem[0].astype(BF16)
    g2, b2 = ln2_g[0].reshape(1, d), ln2_b[0].reshape(1, d)
    x2p = _attn(x1p, mkb.reshape(bp, N_MEM, d), mvb.reshape(bp, N_MEM, d), wq, wo, g2, b2)
    x2s = _attn(x1s, cache_mem_k[0].reshape(bs, N_MEM, d).astype(BF16),
                cache_mem_v[0].reshape(bs, N_MEM, d).astype(BF16), wq, wo, g2, b2)

    wrt = w_router[0].T
    wrt_h = wrt.astype(BF16)
    wrt_m = (wrt - wrt_h.astype(F32)).astype(BF16)
    top_e, gate, rank, cnt = _router(x2p, x2s, jnp.stack([wrt_h, wrt_m]), router_bias[0].reshape(N_EXPERTS, 1))

    assert (n_all * TOP_K) % EXPERT_TM == 0
    n_blk = n_all * TOP_K // EXPERT_TM + N_EXPERTS
    pad_start, blk0, nblk, counts, nu, fill_start, fill_len = _moe_plan(cnt)
    dest = _slots(pad_start, top_e, rank).T.reshape(-1)
    xs = _dispatch(fill_start, fill_len, nu, dest, x2p, x2s, n_blk=n_blk, tm=EXPERT_TM)
    ys = _experts(blk0, nblk, counts, nu, xs, w1_exp[0], w3_exp[0], w2_exp[0], tm=EXPERT_TM)
    sh = (w1_sh[0].astype(BF16), w3_sh[0].astype(BF16), w2_sh[0].astype(BF16),
          ln3_g[0].reshape(1, d), ln3_b[0].reshape(1, d))
    gate_t = gate.T
    yp = _combine(dest, x2p, gate_t, *sh, ys, row_off=0)
    ysm = _combine(dest, x2s, gate_t, *sh, ys, row_off=n_p)

    return (yp.reshape(bp, lp, d), ysm.reshape(bs, ls, d), sp[None], hp[None],
            mk.reshape(1, bp, N_MEM, MEM_HEADS, MEM_DH), mv.reshape(1, bp, N_MEM, MEM_HEADS, MEM_DH),
            ss[None], hs[None])
```

```python
import functools

import jax
import jax.numpy as jnp
from jax import lax
from jax.experimental import pallas as pl
from jax.experimental.pallas import tpu as pltpu

F32 = jnp.float32
BF16 = jnp.bfloat16
I32 = jnp.int32
U32 = jnp.uint32

D_MODEL = 1024
CHUNK = 64
SUB = 16
GLA_HEADS = 4
GLA_DK = 64
GLA_DV = 128
GLA_KEY = GLA_HEADS * GLA_DK
GLA_WIDTH = GLA_HEADS * GLA_DV
GATE_RANK = 16
GATE_PAD = 128
POOL_WIDTH = 512
POOL_WINDOWS = (2, 4, 8, 16)
POOL_GC = 128
POOL_HIST = 15
HIST_ROWS = 16
N_MEM = 256
MEM_HEADS = 4
MEM_DH = 256
N_EXPERTS = 256
N_GROUPS = 8
EXPERTS_PER_GROUP = 32
TOPK_GROUPS = 4
TOP_K = 8
ROUTED_SCALE = 2.5
D_EXPERT = 256
TOKEN_TILE = 256
SUBLANES, LANES = 8, 128
assert D_MODEL == SUBLANES * LANES
ALPHA = 2.0 ** 0.25
EPS = 1e-5
OFF_Q, OFF_K, OFF_V, OFF_G, OFF_U, OFF_GD = 0, 256, 512, 1024, 1536, 2048
D_IN_PAD = OFF_GD + GATE_PAD

VMEM_LIMIT = 56 * 1024 * 1024


def _ln(x, g, b):
    mu = jnp.mean(x, axis=-1, keepdims=True)
    xc = x - mu
    var = jnp.mean(xc * xc, axis=-1, keepdims=True)
    return xc * lax.rsqrt(var + EPS) * g + b


def _dot(a, b):
    return jnp.dot(a, b, preferred_element_type=F32)


def _dot_nt(a, b):
    return lax.dot_general(a, b, (((1,), (1,)), ((), ())), preferred_element_type=F32)


def _dot_tn(a, b):
    return lax.dot_general(a, b, (((0,), (0,)), ((), ())), preferred_element_type=F32)


def _split3(x):
    h = x.astype(BF16)
    r = x - h.astype(F32)
    m = r.astype(BF16)
    l = (r - m.astype(F32)).astype(BF16)
    return h, m, l


def _trunk_a_kernel(x_ref, s0_ref, h0_ref, lng_ref, lnb_ref, win_ref, wgu_ref, bg_ref, gng_ref,
                    pw_ref, ps_ref, wout_ref, l1g_ref, l1b_ref, tri_ref,
                    x1_ref, sn_ref, hn_ref,
                    proj_scr, ext_scr, s_scr, op_scr, *, tl, chunk, start_pos):
    t = pl.program_id(1)
    nt = pl.num_programs(1)

    @pl.when(t == 0)
    def _():
        s_scr[...] = s0_ref[0]
        ext_scr[0:1, :] = jnp.zeros((1, POOL_WIDTH), F32)
        ext_scr[1:HIST_ROWS, :] = h0_ref[0]

    xn = _ln(x_ref[0], lng_ref[...], lnb_ref[...])
    proj_scr[...] = _dot(xn.astype(BF16), win_ref[...])

    c = chunk
    shift = lambda a, n: lax.shift_right_logical(a, n.bit_length() - 1)
    gd = proj_scr[:, OFF_GD:OFF_GD + GATE_PAD]
    z = _dot(gd.astype(BF16), wgu_ref[...]) + bg_ref[...]
    lf = (jnp.minimum(z, 0.0) - jnp.log1p(jnp.exp(-jnp.abs(z)))) * (1.0 / 16.0)
    tri = tri_ref[...]
    lh, lm, ll = _split3(lf)
    cum = _dot(tri, lh) + _dot(tri, lm) + _dot(tri, ll)
    q_all = proj_scr[:, OFF_Q:OFF_Q + GLA_KEY] * (GLA_DK ** -0.5)
    k_all = proj_scr[:, OFF_K:OFF_K + GLA_KEY]
    qs_all = q_all * jnp.exp(cum)

    causal = lax.broadcasted_iota(I32, (c, c), 1) <= lax.broadcasted_iota(I32, (c, c), 0)
    eye_dk = lax.broadcasted_iota(I32, (GLA_DK, GLA_DK), 0) == lax.broadcasted_iota(I32, (GLA_DK, GLA_DK), 1)
    n_sub = c // SUB
    rblk = shift(lax.broadcasted_iota(I32, (c, n_sub * GLA_DK), 0), SUB)
    lblk = shift(lax.broadcasted_iota(I32, (c, n_sub * GLA_DK), 1), GLA_DK)
    lblk_row = shift(lax.broadcasted_iota(I32, (1, n_sub * GLA_DK), 1), GLA_DK)
    mask_q = rblk == lblk
    mask_k = rblk <= lblk

    def tile_lanes(a):
        return jnp.concatenate([a] * n_sub, axis=1)

    states = [s_scr[h] for h in range(GLA_HEADS)]
    for ci in range(tl // c):
        rs = slice(ci * c, (ci + 1) * c)
        cum_c = cum[rs]
        last = cum_c[c - 1:c, :]
        ks_c = k_all[rs] * jnp.exp(last - cum_c)
        for h in range(GLA_HEADS):
            ksl = slice(h * GLA_DK, (h + 1) * GLA_DK)
            vsl = slice(h * GLA_DV, (h + 1) * GLA_DV)
            v_h = proj_scr[rs, OFF_V + h * GLA_DV:OFF_V + (h + 1) * GLA_DV].astype(BF16)
            cum_t = tile_lanes(cum_c[:, ksl])
            q_t = tile_lanes(q_all[rs, ksl])
            k_t = tile_lanes(k_all[rs, ksl])
            ref_row = jnp.zeros((1, n_sub * GLA_DK), F32)
            for i in range(1, n_sub):
                ref_row = jnp.where(lblk_row == i, cum_t[i * SUB - 1:i * SUB, :], ref_row)
            arg = cum_t - ref_row
            lhs = jnp.where(mask_q, q_t * jnp.exp(jnp.where(mask_q, arg, 0.0)), 0.0)
            rhs = jnp.where(mask_k, k_t * jnp.exp(jnp.where(mask_k, -arg, 0.0)), 0.0)
            att = jnp.where(causal, _dot_nt(lhs.astype(BF16), rhs.astype(BF16)), 0.0)
            s_h = states[h]
            o_h = _dot(att.astype(BF16), v_h) + _dot(qs_all[rs, ksl].astype(BF16), s_h.astype(BF16))
            dcol = jnp.sum(jnp.where(eye_dk, jnp.broadcast_to(jnp.exp(last[:, ksl]), (GLA_DK, GLA_DK)), 0.0),
                           axis=1, keepdims=True)
            states[h] = dcol * s_h + _dot_tn(ks_c[:, ksl].astype(BF16), v_h)
            o_h = o_h * lax.rsqrt(jnp.mean(o_h * o_h, axis=-1, keepdims=True) + EPS) * gng_ref[...]
            g_h = proj_scr[rs, OFF_G + h * GLA_DV:OFF_G + (h + 1) * GLA_DV]
            op_scr[rs, vsl] = (o_h * (g_h * jax.nn.sigmoid(g_h))).astype(BF16)
    for h in range(GLA_HEADS):
        s_scr[h] = states[h]

    u = proj_scr[:, OFF_U:OFF_U + POOL_WIDTH]
    ext_scr[HIST_ROWS:HIST_ROWS + tl, :] = u
    n_valid = start_pos + t * tl + lax.broadcasted_iota(I32, (tl, 1), 0) + 1
    for gi, w in enumerate(POOL_WINDOWS):
        lsl = slice(gi * POOL_GC, (gi + 1) * POOL_GC)
        win = ext_scr[HIST_ROWS:HIST_ROWS + tl, lsl]
        for s in range(1, w):
            win = win + ext_scr[HIST_ROWS - s:HIST_ROWS - s + tl, lsl]
        cnt = jnp.minimum(w, n_valid).astype(F32)
        r = win / cnt - u[:, lsl]
        p = _dot(r.astype(BF16), pw_ref[gi]) * ps_ref[:, lsl]
        op_scr[:, GLA_WIDTH + gi * POOL_GC:GLA_WIDTH + (gi + 1) * POOL_GC] = p.astype(BF16)
    tail = ext_scr[tl:tl + HIST_ROWS, :]
    ext_scr[0:HIST_ROWS, :] = tail

    mix = _dot(op_scr[...], wout_ref[...])
    x1_ref[0] = _ln(xn * ALPHA + mix, l1g_ref[...], l1b_ref[...])

    @pl.when(t == nt - 1)
    def _():
        sn_ref[0] = s_scr[...]
        hn_ref[0] = ext_scr[1:HIST_ROWS, :]


def _trunk_a(x, s0, h0, wts, *, start_pos):
    b, l, d = x.shape
    tl = min(l, 256)
    chunk = min(tl, CHUNK)
    assert l % tl == 0 and tl % chunk == 0 and chunk % SUB == 0 and l >= HIST_ROWS
    nt = l // tl
    kern = functools.partial(_trunk_a_kernel, tl=tl, chunk=chunk, start_pos=start_pos)
    const = lambda shape: pl.BlockSpec(shape, lambda bi, ti: (0,) * len(shape))
    pos = jnp.arange(tl)
    tri = ((pos[None, :] <= pos[:, None]) & (pos[None, :] // chunk == pos[:, None] // chunk)).astype(BF16)
    return pl.pallas_call(
        kern,
        grid=(b, nt),
        in_specs=[
            pl.BlockSpec((1, tl, d), lambda bi, ti: (bi, ti, 0)),
            pl.BlockSpec((1, GLA_HEADS, GLA_DK, GLA_DV), lambda bi, ti: (bi, 0, 0, 0)),
            pl.BlockSpec((1, POOL_HIST, POOL_WIDTH), lambda bi, ti: (bi, 0, 0)),
            const((1, d)), const((1, d)),
            const((d, D_IN_PAD)), const((GATE_PAD, GLA_KEY)), const((1, GLA_KEY)), const((1, GLA_DV)),
            const((len(POOL_WINDOWS), POOL_GC, POOL_GC)), const((1, POOL_WIDTH)),
            const((GLA_WIDTH + POOL_WIDTH, d)), const((1, d)), const((1, d)), const((tl, tl)),
        ],
        out_specs=[
            pl.BlockSpec((1, tl, d), lambda bi, ti: (bi, ti, 0)),
            pl.BlockSpec((1, GLA_HEADS, GLA_DK, GLA_DV), lambda bi, ti: (bi, 0, 0, 0)),
            pl.BlockSpec((1, POOL_HIST, POOL_WIDTH), lambda bi, ti: (bi, 0, 0)),
        ],
        out_shape=[
            jax.ShapeDtypeStruct((b, l, d), F32),
            jax.ShapeDtypeStruct((b, GLA_HEADS, GLA_DK, GLA_DV), F32),
            jax.ShapeDtypeStruct((b, POOL_HIST, POOL_WIDTH), F32),
        ],
        scratch_shapes=[
            pltpu.VMEM((tl, D_IN_PAD), F32),
            pltpu.VMEM((HIST_ROWS + tl, POOL_WIDTH), F32),
            pltpu.VMEM((GLA_HEADS, GLA_DK, GLA_DV), F32),
            pltpu.VMEM((tl, GLA_WIDTH + POOL_WIDTH), BF16),
        ],
        compiler_params=pltpu.CompilerParams(
            dimension_semantics=("arbitrary", "arbitrary"), vmem_limit_bytes=VMEM_LIMIT),
        name="trunk_a",
    )(x, s0, h0, *wts, tri)


def _prep_trunk_a_weights(ln_in_g, ln_in_b, w_in, w_gate_up, b_gate, gla_norm_g, pool_w, pool_scale, w_out,
                          ln1_g, ln1_b):
    d = D_MODEL
    p_gd = 2 * GLA_KEY + 2 * GLA_WIDTH
    w_in_r = jnp.concatenate(
        [w_in[:, :p_gd], w_in[:, p_gd + GATE_RANK:], w_in[:, p_gd:p_gd + GATE_RANK],
         jnp.zeros((d, GATE_PAD - GATE_RANK), w_in.dtype)], axis=1).astype(BF16)
    wgu = jnp.concatenate([w_gate_up, jnp.zeros((GATE_PAD - GATE_RANK, GLA_KEY), w_gate_up.dtype)],
                          axis=0).astype(BF16)
    return (ln_in_g.reshape(1, d), ln_in_b.reshape(1, d), w_in_r, wgu, b_gate.reshape(1, GLA_KEY),
            gla_norm_g.reshape(1, GLA_DV), pool_w.astype(BF16), pool_scale.reshape(1, POOL_WIDTH),
            w_out.astype(BF16), ln1_g.reshape(1, d), ln1_b.reshape(1, d))


def _mem_kv_kernel(m_ref, wk_ref, wv_ref, k_ref, v_ref, kb_ref, vb_ref):
    m = m_ref[...].astype(BF16)
    k = _dot(m, wk_ref[...])
    v = _dot(m, wv_ref[...])
    k_ref[...] = k
    v_ref[...] = v
    kb_ref[...] = k.astype(BF16)
    vb_ref[...] = v.astype(BF16)


def _mem_kv(mem, wk, wv):
    m, d = mem.shape
    tm = min(m, 512)
    assert m % tm == 0
    row = pl.BlockSpec((tm, d), lambda i: (i, 0))
    wspec = pl.BlockSpec((d, d), lambda i: (0, 0))
    return pl.pallas_call(
        _mem_kv_kernel,
        grid=(m // tm,),
        in_specs=[row, wspec, wspec],
        out_specs=[row, row, row, row],
        out_shape=[jax.ShapeDtypeStruct((m, d), F32), jax.ShapeDtypeStruct((m, d), F32),
                   jax.ShapeDtypeStruct((m, d), BF16), jax.ShapeDtypeStruct((m, d), BF16)],
        compiler_params=pltpu.CompilerParams(dimension_semantics=("arbitrary",), vmem_limit_bytes=VMEM_LIMIT),
        name="mem_kv",
    )(mem, wk, wv)


def _attn_kernel(x1_ref, k_ref, v_ref, wq_ref, wo_ref, g_ref, b_ref, x2_ref, o_scr):
    x1 = x1_ref[0]
    q = (_dot(x1.astype(BF16), wq_ref[...]) * (MEM_DH ** -0.5)).astype(BF16)
    for h in range(MEM_HEADS):
        hs = slice(h * MEM_DH, (h + 1) * MEM_DH)
        s = _dot_nt(q[:, hs], k_ref[0, :, hs])
        e = jnp.exp(s - jnp.max(s, axis=-1, keepdims=True))
        p = e / jnp.sum(e, axis=-1, keepdims=True)
        o_scr[:, hs] = _dot(p.astype(BF16), v_ref[0, :, hs]).astype(BF16)
    attn = _dot(o_scr[...], wo_ref[...])
    x2_ref[...] = _ln(x1 * ALPHA + attn, g_ref[...], b_ref[...])


def _attn(x1, mem_k, mem_v, wq, wo, g, b):
    bsz, l, d = x1.shape
    tl = min(l, 2 * TOKEN_TILE)
    assert l % tl == 0
    nt = l // tl
    const = lambda shape: pl.BlockSpec(shape, lambda bi, ti: (0,) * len(shape))
    return pl.pallas_call(
        _attn_kernel,
        grid=(bsz, nt),
        in_specs=[
            pl.BlockSpec((1, tl, d), lambda bi, ti: (bi, ti, 0)),
            pl.BlockSpec((1, N_MEM, d), lambda bi, ti: (bi, 0, 0)),
            pl.BlockSpec((1, N_MEM, d), lambda bi, ti: (bi, 0, 0)),
            const((d, d)), const((d, d)), const((1, d)), const((1, d)),
        ],
        out_specs=pl.BlockSpec((tl, d), lambda bi, ti: (bi * nt + ti, 0)),
        out_shape=jax.ShapeDtypeStruct((bsz * l, d), F32),
        scratch_shapes=[pltpu.VMEM((tl, d), BF16)],
        compiler_params=pltpu.CompilerParams(
            dimension_semantics=("arbitrary", "arbitrary"), vmem_limit_bytes=VMEM_LIMIT),
        name="attn",
    )(x1, mem_k, mem_v, wq, wo, g, b)


def _two_part_specs(t, d, tiles_a):
    spec_a = pl.BlockSpec((t, d), lambda i, *_: (jnp.minimum(i, tiles_a - 1), 0))
    spec_b = pl.BlockSpec((t, d), lambda i, *_: (jnp.maximum(i - tiles_a, 0), 0))
    return spec_a, spec_b


def _router_kernel(xa_ref, xb_ref, wrt_ref, bias_ref, e_ref, g_ref, r_ref, cnt_ref, cnt_scr, *, tl, tiles_a):
    i = pl.program_id(0)

    @pl.when(i == 0)
    def _():
        cnt_scr[...] = jnp.zeros_like(cnt_scr)

    x = jnp.where(i < tiles_a, xa_ref[...], xb_ref[...])
    xh = x.astype(BF16)
    xm = (x - xh.astype(F32)).astype(BF16)
    wh = wrt_ref[0]
    wm = wrt_ref[1]
    logits = _dot_nt(wh, xh) + (_dot_nt(wh, xm) + _dot_nt(wm, xh))
    scores = jax.nn.sigmoid(logits)
    biased = scores + bias_ref[...]
    ninf = -jnp.inf
    eg = EXPERTS_PER_GROUP
    riota = lax.broadcasted_iota(I32, (eg, tl), 0)
    gs_rows = []
    for g in range(N_GROUPS):
        blk = biased[g * eg:(g + 1) * eg, :]
        m1 = jnp.max(blk, axis=0, keepdims=True)
        i1 = jnp.min(jnp.where(blk == m1, riota, eg), axis=0, keepdims=True)
        m2 = jnp.max(jnp.where(riota == i1, ninf, blk), axis=0, keepdims=True)
        gs_rows.append(m1 + m2)
    gs = jnp.concatenate(gs_rows, axis=0)
    giota = lax.broadcasted_iota(I32, (N_GROUPS, tl), 0)
    sel = jnp.zeros((N_GROUPS, tl), jnp.bool_)
    for _ in range(TOPK_GROUPS):
        m = jnp.max(gs, axis=0, keepdims=True)
        gi = jnp.min(jnp.where(gs == m, giota, N_GROUPS), axis=0, keepdims=True)
        hit = giota == gi
        sel = jnp.logical_or(sel, hit)
        gs = jnp.where(hit, ninf, gs)
    self = jnp.where(sel, 1.0, 0.0)
    masked = jnp.concatenate(
        [jnp.where(self[g:g + 1, :] > 0.5, biased[g * eg:(g + 1) * eg, :], ninf) for g in range(N_GROUPS)], axis=0)
    eiota = lax.broadcasted_iota(I32, (N_EXPERTS, tl), 0)
    idx_rows, sc_rows = [], []
    multi = jnp.zeros((N_EXPERTS, tl), F32)
    for _ in range(TOP_K):
        m = jnp.max(masked, axis=0, keepdims=True)
        idx = jnp.min(jnp.where(masked == m, eiota, N_EXPERTS), axis=0, keepdims=True)
        hit = eiota == idx
        sc_rows.append(jnp.sum(jnp.where(hit, scores, 0.0), axis=0, keepdims=True))
        idx_rows.append(idx)
        multi = jnp.where(hit, 1.0, multi)
        masked = jnp.where(hit, ninf, masked)
    top_e = jnp.concatenate(idx_rows, axis=0)
    sc = jnp.concatenate(sc_rows, axis=0)
    e_ref[...] = top_e
    g_ref[...] = sc / jnp.sum(sc, axis=0, keepdims=True) * ROUTED_SCALE
    mh = multi.astype(BF16)
    before = (lax.broadcasted_iota(I32, (tl, tl), 0) < lax.broadcasted_iota(I32, (tl, tl), 1)).astype(BF16)
    running = cnt_scr[...]
    rankmat = _dot(mh, before) + jnp.concatenate([running] * (tl // 128), axis=1)
    r_rows = [jnp.sum(jnp.where(eiota == idx_rows[k], rankmat, 0.0), axis=0, keepdims=True) for k in range(TOP_K)]
    r_ref[...] = jnp.concatenate(r_rows, axis=0).astype(I32)
    total = running + _dot(mh, jnp.ones((tl, 128), BF16))
    cnt_scr[...] = total
    cnt_ref[...] = total


def _router(x2a, x2b, wrt, bias_col):
    d = x2a.shape[1]
    tl = TOKEN_TILE
    assert x2a.shape[0] % tl == 0 and x2b.shape[0] % tl == 0
    tiles_a = x2a.shape[0] // tl
    n = x2a.shape[0] + x2b.shape[0]
    kspec = pl.BlockSpec((TOP_K, tl), lambda i: (0, i))
    return pl.pallas_call(
        functools.partial(_router_kernel, tl=tl, tiles_a=tiles_a),
        grid=(n // tl,),
        in_specs=[*_two_part_specs(tl, d, tiles_a),
                  pl.BlockSpec((2, N_EXPERTS, d), lambda i: (0, 0, 0)),
                  pl.BlockSpec((N_EXPERTS, 1), lambda i: (0, 0))],
        out_specs=[kspec, kspec, kspec, pl.BlockSpec((N_EXPERTS, 128), lambda i: (0, 0))],
        out_shape=[jax.ShapeDtypeStruct((TOP_K, n), I32), jax.ShapeDtypeStruct((TOP_K, n), F32),
                   jax.ShapeDtypeStruct((TOP_K, n), I32), jax.ShapeDtypeStruct((N_EXPERTS, 128), F32)],
        scratch_shapes=[pltpu.VMEM((N_EXPERTS, 128), F32)],
        compiler_params=pltpu.CompilerParams(dimension_semantics=("arbitrary",), vmem_limit_bytes=VMEM_LIMIT),
        name="router",
    )(x2a, x2b, wrt, bias_col)


PACK_ROWS = D_MODEL // 2 // LANES
HI_MASK = 0xFFFF0000


def _slots_of(buf_ref, start, n=None):
    if n is None:
        return buf_ref.at[start, pl.ds(0, PACK_ROWS)]
    return buf_ref.at[pl.ds(start, n), pl.ds(0, PACK_ROWS)]


def _word_rows(buf_ref):
    if len(buf_ref.shape) == 3:
        return buf_ref.reshape(buf_ref.shape[0] * SUBLANES, LANES), SUBLANES
    return buf_ref, PACK_ROWS


def _pack_rows(x, dst_ref, row0=0):
    t = x.shape[0]
    bits = lax.bitcast_convert_type(x.astype(BF16).astype(F32), U32)
    flat, pitch = _word_rows(dst_ref)
    for j in range(PACK_ROWS):
        lo = lax.shift_right_logical(bits[:, j * LANES:(j + 1) * LANES], jnp.uint32(16))
        hi = bits[:, D_MODEL // 2 + j * LANES:D_MODEL // 2 + (j + 1) * LANES] & jnp.uint32(HI_MASK)
        flat[pl.ds(row0 * pitch + j, t, stride=pitch), :] = hi | lo


def _unpack_words(w):
    lo = lax.bitcast_convert_type(lax.shift_left(w, jnp.uint32(16)), F32)
    hi = lax.bitcast_convert_type(w & jnp.uint32(HI_MASK), F32)
    return lo, hi


def _unpack_rows(src_ref, t, row0=0):
    flat, pitch = _word_rows(src_ref)
    parts = [_unpack_words(flat[pl.ds(row0 * pitch + j, t, stride=pitch), :]) for j in range(PACK_ROWS)]
    return jnp.concatenate([p[0] for p in parts] + [p[1] for p in parts], axis=1)


def _dispatch_kernel(fs_ref, fl_ref, nu_ref, dest_ref, xa_ref, xb_ref, xs_ref, stage, zbuf, sem, zsem,
                     *, t, tm, tiles_a, n_blk):
    i = pl.program_id(0)
    nsteps = pl.num_programs(0)

    def zero_fill(start):
        def go(cp):
            if start:
                cp.start()
            else:
                cp.wait()

        def per_expert(e, carry):
            ln = fl_ref[e]
            off = fs_ref[e]
            bit = tm // 2
            while bit:
                @pl.when((ln & bit) != 0)
                def _(off=off, bit=bit):
                    go(pltpu.make_async_copy(_slots_of(zbuf, 0, bit), xs_ref.at[pl.ds(off, bit)], zsem))
                off = off + (ln & bit)
                bit //= 2
            return carry

        lax.fori_loop(0, N_EXPERTS, per_expert, 0)

        def per_block(b, carry):
            go(pltpu.make_async_copy(_slots_of(zbuf, 0, tm), xs_ref.at[pl.ds(b * tm, tm)], zsem))
            return carry

        lax.fori_loop(nu_ref[0], n_blk, per_block, 0)

    @pl.when(i == 0)
    def _():
        zbuf[...] = jnp.zeros_like(zbuf)
        zero_fill(True)

    def drain(slot):
        for k in range(TOP_K):
            pltpu.make_async_copy(_slots_of(stage.at[slot], 0, t), xs_ref.at[pl.ds(0, t)], sem.at[slot]).wait()

    def step(slot):
        @pl.when(i >= 2)
        def _():
            drain(slot)

        _pack_rows(jnp.where(i < tiles_a, xa_ref[...], xb_ref[...]), stage.at[slot])

        def body(n, carry):
            src = _slots_of(stage.at[slot], n)
            for k in range(TOP_K):
                dst = xs_ref.at[dest_ref[n * TOP_K + k]]
                pltpu.make_async_copy(src, dst, sem.at[slot]).start(priority=k % 2)
            return carry

        lax.fori_loop(0, t, body, 0)

    for slot in range(2):
        @pl.when(i % 2 == slot)
        def _(slot=slot):
            step(slot)

    @pl.when(i == 0)
    def _():
        zero_fill(False)

    @pl.when(i == nsteps - 1)
    def _():
        for slot in range(2):
            @pl.when(jnp.logical_or(nsteps >= 2, (nsteps - 1) % 2 == slot))
            def _(slot=slot):
                drain(slot)


def _dispatch(fill_start, fill_len, nu, dest, x2a, x2b, *, n_blk, tm):
    d = x2a.shape[1]
    t = TOKEN_TILE
    assert x2a.shape[0] % t == 0 and x2b.shape[0] % t == 0
    tiles_a = x2a.shape[0] // t
    n = x2a.shape[0] + x2b.shape[0]
    return pl.pallas_call(
        functools.partial(_dispatch_kernel, t=t, tm=tm, tiles_a=tiles_a, n_blk=n_blk),
        grid_spec=pltpu.PrefetchScalarGridSpec(
            num_scalar_prefetch=3,
            grid=(n // t,),
            in_specs=[pl.BlockSpec((t * TOP_K,), lambda i, *_: (i,), memory_space=pltpu.SMEM),
                      *_two_part_specs(t, d, tiles_a)],
            out_specs=pl.BlockSpec(memory_space=pl.ANY),
            scratch_shapes=[pltpu.VMEM((2, t, SUBLANES, LANES), U32), pltpu.VMEM((tm, SUBLANES, LANES), U32),
                            pltpu.SemaphoreType.DMA((2,)), pltpu.SemaphoreType.DMA(())],
        ),
        out_shape=jax.ShapeDtypeStruct((n_blk * tm, PACK_ROWS, LANES), U32),
        compiler_params=pltpu.CompilerParams(dimension_semantics=("arbitrary",), vmem_limit_bytes=VMEM_LIMIT),
        name="dispatch",
    )(fill_start, fill_len, nu, dest, x2a, x2b)


def _experts_kernel(b0_ref, nb_ref, cnt_ref, nu_ref, w1_ref, w3_ref, w2_ref, xs_ref, ys_ref,
                    w1b, w3b, w2b, xbuf, ybuf, in_sem, out_sem, *, tm, n_blk):
    e = pl.program_id(0)
    n_used = nu_ref[0]

    blk_rows = tm * PACK_ROWS

    def hbm_block(ref, b):
        return ref.at[pl.ds(pl.multiple_of(b * blk_rows, blk_rows), blk_rows)]

    def x_copy(b):
        s = lax.rem(b, X_DEPTH)
        return pltpu.make_async_copy(hbm_block(xs_ref, b), xbuf.at[s], in_sem.at[s])

    def y_copy(b):
        return pltpu.make_async_copy(ybuf.at[b % 2], hbm_block(ys_ref, b), out_sem.at[b % 2])

    @pl.when(e == 0)
    def _():
        for b in range(X_DEPTH - 1):
            @pl.when(b < n_used)
            def _(b=b):
                x_copy(b).start()

    w1b[...] = w1_ref[0].astype(BF16)
    w3b[...] = w3_ref[0].astype(BF16)
    w2b[...] = w2_ref[0].astype(BF16)
    b0 = b0_ref[e]
    cnt = cnt_ref[e]

    def block(j, carry):
        b = b0 + j
        x_copy(b).wait()

        @pl.when(b + X_DEPTH - 1 < n_used)
        def _():
            x_copy(b + X_DEPTH - 1).start()

        @pl.when(b >= 2)
        def _():
            y_copy(b - 2).wait()

        rows = tm // EXPERT_SPLIT
        for r0 in range(0, tm, rows):
            live = lax.broadcasted_iota(I32, (rows, 1), 0) < cnt - j * tm - r0
            x = jnp.where(live, _unpack_rows(xbuf.at[lax.rem(b, X_DEPTH)], rows, r0), 0.0).astype(BF16)
            a = _dot(x, w1b[...])
            h = (a * jax.nn.sigmoid(a)) * _dot(x, w3b[...])
            _pack_rows(_dot(h.astype(BF16), w2b[...]), ybuf.at[b % 2], r0)
        y_copy(b).start()
        return carry

    lax.fori_loop(0, nb_ref[e], block, 0)

    @pl.when(e == pl.num_programs(0) - 1)
    def _():
        @pl.when(n_used >= 2)
        def _():
            y_copy(n_used - 2).wait()

        y_copy(n_used - 1).wait()
        ybuf[0] = jnp.zeros(ybuf.shape[1:], U32)

        def zero_copy(b):
            return pltpu.make_async_copy(ybuf.at[0], hbm_block(ys_ref, b), out_sem.at[0])

        def zero(b, carry):
            zero_copy(b).start()
            return carry

        lax.fori_loop(n_used, n_blk, zero, 0)

        def zero_wait(b, carry):
            zero_copy(b).wait()
            return carry

        lax.fori_loop(n_used, n_blk, zero_wait, 0)


def _experts(blk0, nblk, counts, nu, xs, w1, w3, w2, *, tm):
    n_slot = xs.shape[0]
    n_blk = n_slot // tm
    n_exp, d, de = w1.shape
    wmap = lambda e, *_: (e, 0, 0)
    ys = pl.pallas_call(
        functools.partial(_experts_kernel, tm=tm, n_blk=n_blk),
        grid_spec=pltpu.PrefetchScalarGridSpec(
            num_scalar_prefetch=4,
            grid=(n_exp,),
            in_specs=[pl.BlockSpec((1, d, de), wmap), pl.BlockSpec((1, d, de), wmap), pl.BlockSpec((1, de, d), wmap),
                      pl.BlockSpec(memory_space=pl.ANY)],
            out_specs=pl.BlockSpec(memory_space=pl.ANY),
            scratch_shapes=[pltpu.VMEM((d, de), BF16), pltpu.VMEM((d, de), BF16), pltpu.VMEM((de, d), BF16),
                            pltpu.VMEM((X_DEPTH, tm * PACK_ROWS, LANES), U32), pltpu.VMEM((2, tm * PACK_ROWS, LANES), U32),
                            pltpu.SemaphoreType.DMA((X_DEPTH,)), pltpu.SemaphoreType.DMA((2,))],
        ),
        out_shape=jax.ShapeDtypeStruct((n_slot * PACK_ROWS, LANES), U32),
        compiler_params=pltpu.CompilerParams(dimension_semantics=("arbitrary",), vmem_limit_bytes=VMEM_LIMIT),
        name="experts",
    )(blk0, nblk, counts, nu, w1, w3, w2, xs.reshape(n_slot * PACK_ROWS, LANES))
    return ys.reshape(n_slot, PACK_ROWS, LANES)


def _combine_kernel(dcur_ref, dnxt_ref, x2_ref, g_ref, w1_ref, w3_ref, w2_ref, lg_ref, lb_ref, ys_ref,
                    out_ref, buf, acc_scr, sem, *, t, nsteps):
    i = pl.program_id(0)

    def fetch(d_ref, s, n):
        for k in range(TOP_K):
            pltpu.make_async_copy(ys_ref.at[d_ref[n * TOP_K + k]], _slots_of(buf.at[s, k], n),
                                  sem.at[s]).start(priority=k % 2)

    def drain(s):
        for k in range(TOP_K):
            pltpu.make_async_copy(ys_ref.at[pl.ds(0, t)], _slots_of(buf.at[s, k], 0, t), sem.at[s]).wait()

    @pl.when(i == 0)
    def _():
        def first(n, carry):
            fetch(dcur_ref, 0, n)
            return carry

        lax.fori_loop(0, t, first, 0)

    def step(s):
        drain(s)

        def group(gi, carry):
            r0 = pl.multiple_of(gi * SUBLANES, SUBLANES)
            for j in range(SUBLANES):
                fetch(dnxt_ref, 1 - s, r0 + j)
            gates = g_ref[pl.ds(r0, SUBLANES), :]
            gk = [jnp.broadcast_to(gates[:, k:k + 1], (SUBLANES, LANES)) for k in range(TOP_K)]
            planes = [buf.at[s, k].reshape(t * SUBLANES, LANES) for k in range(TOP_K)]
            for j in range(PACK_ROWS):
                acc_lo = acc_hi = None
                for k in range(TOP_K):
                    lo, hi = _unpack_words(planes[k][pl.ds(r0 * SUBLANES + j, SUBLANES, stride=SUBLANES), :])
                    acc_lo = gk[k] * lo if acc_lo is None else acc_lo + gk[k] * lo
                    acc_hi = gk[k] * hi if acc_hi is None else acc_hi + gk[k] * hi
                acc_scr[pl.ds(r0, SUBLANES), j * LANES:(j + 1) * LANES] = acc_lo
                acc_scr[pl.ds(r0, SUBLANES), D_MODEL // 2 + j * LANES:D_MODEL // 2 + (j + 1) * LANES] = acc_hi
            return carry

        lax.fori_loop(0, t // SUBLANES, group, 0)

    for s in range(2):
        @pl.when(i % 2 == s)
        def _(s=s):
            step(s)

    @pl.when(i == nsteps - 1)
    def _():
        drain(nsteps % 2)

    x2 = x2_ref[...]
    acc = acc_scr[...]
    xb = x2.astype(BF16)
    a = _dot(xb, w1_ref[...])
    hs = (a * jax.nn.sigmoid(a)) * _dot(xb, w3_ref[...])
    shared = _dot(hs.astype(BF16), w2_ref[...])
    out_ref[...] = _ln(x2 * ALPHA + (acc + shared), lg_ref[...], lb_ref[...])


def _combine(dest, x2, gate_t, w1s, w3s, w2s, lg, lb, ys, *, row_off):
    n_rows, d = x2.shape
    n = dest.shape[0] // TOP_K
    t = 2 * TOKEN_TILE
    if row_off % t or n_rows % t or n % t:
        t = TOKEN_TILE
    assert row_off % t == 0 and n_rows % t == 0 and n % t == 0
    off = row_off // t
    last = n // t - 1
    de = w1s.shape[-1]
    const = lambda shape: pl.BlockSpec(shape, lambda i: (0,) * len(shape))
    return pl.pallas_call(
        functools.partial(_combine_kernel, t=t, nsteps=n_rows // t),
        grid=(n_rows // t,),
        in_specs=[pl.BlockSpec((t * TOP_K,), lambda i: (off + i,), memory_space=pltpu.SMEM),
                  pl.BlockSpec((t * TOP_K,), lambda i: (jnp.minimum(off + i + 1, last),), memory_space=pltpu.SMEM),
                  pl.BlockSpec((t, d), lambda i: (i, 0)),
                  pl.BlockSpec((t, TOP_K), lambda i: (off + i, 0)),
                  const((d, de)), const((d, de)), const((de, d)), const((1, d)), const((1, d)),
                  pl.BlockSpec(memory_space=pl.ANY)],
        out_specs=pl.BlockSpec((t, d), lambda i: (i, 0)),
        out_shape=jax.ShapeDtypeStruct((n_rows, d), F32),
        scratch_shapes=[pltpu.VMEM((2, TOP_K, t, SUBLANES, LANES), U32), pltpu.VMEM((t, d), F32),
                        pltpu.SemaphoreType.DMA((2,))],
        compiler_params=pltpu.CompilerParams(dimension_semantics=("arbitrary",), vmem_limit_bytes=VMEM_LIMIT),
        name="combine",
    )(dest, dest, x2, gate_t, w1s, w3s, w2s, lg, lb, ys)


EXPERT_TM = 512
EXPERT_SPLIT = 2
X_DEPTH = 4


def _moe_plan(cnt):
    tm = EXPERT_TM
    counts = cnt[:, 0].astype(I32)
    padded = (counts + tm - 1) // tm * tm
    pad_end = jnp.cumsum(padded)
    pad_start = pad_end - padded
    n_used = pad_end[-1] // tm
    fill_start = (pad_start + counts).astype(I32)
    fill_len = (padded - counts).astype(I32)
    return (pad_start.astype(I32), (pad_start // tm).astype(I32), (padded // tm).astype(I32), counts,
            n_used.reshape(1).astype(I32), fill_start, fill_len)


def _slots_kernel(ps_ref, e_ref, r_ref, d_ref):
    e = e_ref[...]

    def body(j, base):
        return jnp.where(e == j, ps_ref[j], base)

    base = lax.fori_loop(0, N_EXPERTS, body, jnp.zeros_like(e), unroll=8)
    d_ref[...] = base + r_ref[...]


def _slots(pad_start, top_e, rank):
    k, n = top_e.shape
    tl = next(c for c in (2048, 1536, 1024, 512, 256, 128) if n % c == 0)
    spec = pl.BlockSpec((k, tl), lambda i, ps: (0, i))
    return pl.pallas_call(
        _slots_kernel,
        grid_spec=pltpu.PrefetchScalarGridSpec(num_scalar_prefetch=1, grid=(n // tl,), in_specs=[spec, spec],
                                               out_specs=spec),
        out_shape=jax.ShapeDtypeStruct((k, n), I32),
        compiler_params=pltpu.CompilerParams(dimension_semantics=("arbitrary",)),
        name="slots",
    )(pad_start, top_e, rank)


PAST_LEN = 1024


def kernel(x_prompt, x_sample, state_gla, cache_pool, cache_mem_k, cache_mem_v, mem_prompt, ln_in_g, ln_in_b, w_in, w_gate_up, b_gate, gla_norm_g, pool_w, pool_scale, w_out, ln1_g, ln1_b, wq_mem, wk_mem, wv_mem, wo_mem, ln2_g, ln2_b, w_router, router_bias, w1_exp, w3_exp, w2_exp, w1_sh, w3_sh, w2_sh, ln3_g, ln3_b):
    assert w_in.shape[0] == 1, "single-layer trunk"
    bp, lp, d = x_prompt.shape
    bs, ls, _ = x_sample.shape
    n_p, n_s = bp * lp, bs * ls
    n_all = n_p + n_s

    wts_a = _prep_trunk_a_weights(ln_in_g, ln_in_b, w_in[0], w_gate_up[0], b_gate[0], gla_norm_g[0], pool_w[0],
                                  pool_scale[0], w_out[0], ln1_g[0], ln1_b[0])
    s0 = jnp.zeros((bp, GLA_HEADS, GLA_DK, GLA_DV), F32)
    h0 = jnp.zeros((bp, POOL_HIST, POOL_WIDTH), F32)
    x1p, sp, hp = _trunk_a(x_prompt, s0, h0, wts_a, start_pos=0)
    x1s, ss, hs = _trunk_a(x_sample, state_gla[0], cache_pool[0], wts_a, start_pos=PAST_LEN)

    mk, mv, mkb, mvb = _mem_kv(mem_prompt.reshape(bp * N_MEM, d), wk_mem[0].astype(BF16), wv_mem[0].astype(BF16))
    wq, wo = wq_mem[0].astype(BF16), wo_mem[0].astype(BF16)
    g2, b2 = ln2_g[0].reshape(1, d), ln2_b[0].reshape(1, d)
    x2p = _attn(x1p, mkb.reshape(bp, N_MEM, d), mvb.reshape(bp, N_MEM, d), wq, wo, g2, b2)
    x2s = _attn(x1s, cache_mem_k[0].reshape(bs, N_MEM, d).astype(BF16),
                cache_mem_v[0].reshape(bs, N_MEM, d).astype(BF16), wq, wo, g2, b2)

    wrt = w_router[0].T
    wrt_h = wrt.astype(BF16)
    wrt_m = (wrt - wrt_h.astype(F32)).astype(BF16)
    top_e, gate, rank, cnt = _router(x2p, x2s, jnp.stack([wrt_h, wrt_m]), router_bias[0].reshape(N_EXPERTS, 1))

    assert (n_all * TOP_K) % EXPERT_TM == 0
    n_blk = n_all * TOP_K // EXPERT_TM + N_EXPERTS
    pad_start, blk0, nblk, counts, nu, fill_start, fill_len = _moe_plan(cnt)
    dest = _slots(pad_start, top_e, rank).T.reshape(-1)
    xs = _dispatch(fill_start, fill_len, nu, dest, x2p, x2s, n_blk=n_blk, tm=EXPERT_TM)
    ys = _experts(blk0, nblk, counts, nu, xs, w1_exp[0], w3_exp[0], w2_exp[0], tm=EXPERT_TM)
    sh = (w1_sh[0].astype(BF16), w3_sh[0].astype(BF16), w2_sh[0].astype(BF16),
          ln3_g[0].reshape(1, d), ln3_b[0].reshape(1, d))
    gate_t = gate.T
    yp = _combine(dest, x2p, gate_t, *sh, ys, row_off=0)
    ysm = _combine(dest, x2s, gate_t, *sh, ys, row_off=n_p)

    return (yp.reshape(bp, lp, d), ysm.reshape(bs, ls, d), sp[None], hp[None],
            mk.reshape(1, bp, N_MEM, MEM_HEADS, MEM_DH), mv.reshape(1, bp, N_MEM, MEM_HEADS, MEM_DH),
            ss[None], hs[None])
```

```python
import functools

import jax
import jax.numpy as jnp
from jax import lax
from jax.experimental import pallas as pl
from jax.experimental.pallas import tpu as pltpu

F32 = jnp.float32
BF16 = jnp.bfloat16
I32 = jnp.int32
U32 = jnp.uint32

D_MODEL = 1024
CHUNK = 64
SUB = 16
GLA_HEADS = 4
GLA_DK = 64
GLA_DV = 128
GLA_KEY = GLA_HEADS * GLA_DK
GLA_WIDTH = GLA_HEADS * GLA_DV
GATE_RANK = 16
GATE_PAD = 128
POOL_WIDTH = 512
POOL_WINDOWS = (2, 4, 8, 16)
POOL_GC = 128
POOL_HIST = 15
HIST_ROWS = 16
N_MEM = 256
MEM_HEADS = 4
MEM_DH = 256
N_EXPERTS = 256
N_GROUPS = 8
EXPERTS_PER_GROUP = 32
TOPK_GROUPS = 4
TOP_K = 8
ROUTED_SCALE = 2.5
D_EXPERT = 256
TOKEN_TILE = 512
SUBLANES, LANES = 8, 128
assert D_MODEL == SUBLANES * LANES
ALPHA = 2.0 ** 0.25
EPS = 1e-5
OFF_Q, OFF_K, OFF_V, OFF_G, OFF_U, OFF_GD = 0, 256, 512, 1024, 1536, 2048
D_IN_PAD = OFF_GD + GATE_PAD

VMEM_LIMIT = 56 * 1024 * 1024


def _ln(x, g, b):
    mu = jnp.mean(x, axis=-1, keepdims=True)
    xc = x - mu
    var = jnp.mean(xc * xc, axis=-1, keepdims=True)
    return xc * lax.rsqrt(var + EPS) * g + b


def _dot(a, b):
    return jnp.dot(a, b, preferred_element_type=F32)


def _dot_nt(a, b):
    return lax.dot_general(a, b, (((1,), (1,)), ((), ())), preferred_element_type=F32)


def _dot_tn(a, b):
    return lax.dot_general(a, b, (((0,), (0,)), ((), ())), preferred_element_type=F32)


def _split3(x):
    h = x.astype(BF16)
    r = x - h.astype(F32)
    m = r.astype(BF16)
    l = (r - m.astype(F32)).astype(BF16)
    return h, m, l


def _trunk_a_kernel(x_ref, s0_ref, h0_ref, lng_ref, lnb_ref, win_ref, wgu_ref, bg_ref, gng_ref,
                    pw_ref, ps_ref, wout_ref, l1g_ref, l1b_ref, tri_ref,
                    x1_ref, sn_ref, hn_ref,
                    proj_scr, ext_scr, s_scr, op_scr, *, tl, chunk, start_pos):
    t = pl.program_id(1)
    nt = pl.num_programs(1)

    @pl.when(t == 0)
    def _():
        s_scr[...] = s0_ref[0]
        ext_scr[0:1, :] = jnp.zeros((1, POOL_WIDTH), F32)
        ext_scr[1:HIST_ROWS, :] = h0_ref[0]

    xn = _ln(x_ref[0], lng_ref[...], lnb_ref[...])
    proj_scr[...] = _dot(xn.astype(BF16), win_ref[...])

    c = chunk
    shift = lambda a, n: lax.shift_right_logical(a, n.bit_length() - 1)
    gd = proj_scr[:, OFF_GD:OFF_GD + GATE_PAD]
    z = _dot(gd.astype(BF16), wgu_ref[...]) + bg_ref[...]
    lf = (jnp.minimum(z, 0.0) - jnp.log1p(jnp.exp(-jnp.abs(z)))) * (1.0 / 16.0)
    tri = tri_ref[...]
    lh, lm, ll = _split3(lf)
    cum = _dot(tri, lh) + _dot(tri, lm) + _dot(tri, ll)
    q_all = proj_scr[:, OFF_Q:OFF_Q + GLA_KEY] * (GLA_DK ** -0.5)
    k_all = proj_scr[:, OFF_K:OFF_K + GLA_KEY]
    qs_all = q_all * jnp.exp(cum)

    causal = lax.broadcasted_iota(I32, (c, c), 1) <= lax.broadcasted_iota(I32, (c, c), 0)
    eye_dk = lax.broadcasted_iota(I32, (GLA_DK, GLA_DK), 0) == lax.broadcasted_iota(I32, (GLA_DK, GLA_DK), 1)
    n_sub = c // SUB
    rblk = shift(lax.broadcasted_iota(I32, (c, n_sub * GLA_DK), 0), SUB)
    lblk = shift(lax.broadcasted_iota(I32, (c, n_sub * GLA_DK), 1), GLA_DK)
    lblk_row = shift(lax.broadcasted_iota(I32, (1, n_sub * GLA_DK), 1), GLA_DK)
    mask_q = rblk == lblk
    mask_k = rblk <= lblk

    def tile_lanes(a):
        return jnp.concatenate([a] * n_sub, axis=1)

    states = [s_scr[h] for h in range(GLA_HEADS)]
    for ci in range(tl // c):
        rs = slice(ci * c, (ci + 1) * c)
        cum_c = cum[rs]
        last = cum_c[c - 1:c, :]
        ks_c = k_all[rs] * jnp.exp(last - cum_c)
        for h in range(GLA_HEADS):
            ksl = slice(h * GLA_DK, (h + 1) * GLA_DK)
            vsl = slice(h * GLA_DV, (h + 1) * GLA_DV)
            v_h = proj_scr[rs, OFF_V + h * GLA_DV:OFF_V + (h + 1) * GLA_DV].astype(BF16)
            cum_t = tile_lanes(cum_c[:, ksl])
            q_t = tile_lanes(q_all[rs, ksl])
            k_t = tile_lanes(k_all[rs, ksl])
            ref_row = jnp.zeros((1, n_sub * GLA_DK), F32)
            for i in range(1, n_sub):
                ref_row = jnp.where(lblk_row == i, cum_t[i * SUB - 1:i * SUB, :], ref_row)
            arg = cum_t - ref_row
            lhs = jnp.where(mask_q, q_t * jnp.exp(jnp.where(mask_q, arg, 0.0)), 0.0)
            rhs = jnp.where(mask_k, k_t * jnp.exp(jnp.where(mask_k, -arg, 0.0)), 0.0)
            att = jnp.where(causal, _dot_nt(lhs.astype(BF16), rhs.astype(BF16)), 0.0)
            s_h = states[h]
            o_h = _dot(att.astype(BF16), v_h) + _dot(qs_all[rs, ksl].astype(BF16), s_h.astype(BF16))
            dcol = jnp.sum(jnp.where(eye_dk, jnp.broadcast_to(jnp.exp(last[:, ksl]), (GLA_DK, GLA_DK)), 0.0),
                           axis=1, keepdims=True)
            states[h] = dcol * s_h + _dot_tn(ks_c[:, ksl].astype(BF16), v_h)
            o_h = o_h * lax.rsqrt(jnp.mean(o_h * o_h, axis=-1, keepdims=True) + EPS) * gng_ref[...]
            g_h = proj_scr[rs, OFF_G + h * GLA_DV:OFF_G + (h + 1) * GLA_DV]
            op_scr[rs, vsl] = (o_h * (g_h * jax.nn.sigmoid(g_h))).astype(BF16)
    for h in range(GLA_HEADS):
        s_scr[h] = states[h]

    u = proj_scr[:, OFF_U:OFF_U + POOL_WIDTH]
    ext_scr[HIST_ROWS:HIST_ROWS + tl, :] = u
    n_valid = start_pos + t * tl + lax.broadcasted_iota(I32, (tl, 1), 0) + 1
    for gi, w in enumerate(POOL_WINDOWS):
        lsl = slice(gi * POOL_GC, (gi + 1) * POOL_GC)
        win = ext_scr[HIST_ROWS:HIST_ROWS + tl, lsl]
        for s in range(1, w):
            win = win + ext_scr[HIST_ROWS - s:HIST_ROWS - s + tl, lsl]
        cnt = jnp.minimum(w, n_valid).astype(F32)
        r = win / cnt - u[:, lsl]
        p = _dot(r.astype(BF16), pw_ref[gi]) * ps_ref[:, lsl]
        op_scr[:, GLA_WIDTH + gi * POOL_GC:GLA_WIDTH + (gi + 1) * POOL_GC] = p.astype(BF16)
    tail = ext_scr[tl:tl + HIST_ROWS, :]
    ext_scr[0:HIST_ROWS, :] = tail

    mix = _dot(op_scr[...], wout_ref[...])
    x1_ref[0] = _ln(xn * ALPHA + mix, l1g_ref[...], l1b_ref[...])

    @pl.when(t == nt - 1)
    def _():
        sn_ref[0] = s_scr[...]
        hn_ref[0] = ext_scr[1:HIST_ROWS, :]


def _trunk_a(x, s0, h0, wts, *, start_pos):
    b, l, d = x.shape
    tl = min(l, 256)
    chunk = min(tl, CHUNK)
    assert l % tl == 0 and tl % chunk == 0 and chunk % SUB == 0 and l >= HIST_ROWS
    nt = l // tl
    kern = functools.partial(_trunk_a_kernel, tl=tl, chunk=chunk, start_pos=start_pos)
    const = lambda shape: pl.BlockSpec(shape, lambda bi, ti: (0,) * len(shape))
    pos = jnp.arange(tl)
    tri = ((pos[None, :] <= pos[:, None]) & (pos[None, :] // chunk == pos[:, None] // chunk)).astype(BF16)
    return pl.pallas_call(
        kern,
        grid=(b, nt),
        in_specs=[
            pl.BlockSpec((1, tl, d), lambda bi, ti: (bi, ti, 0)),
            pl.BlockSpec((1, GLA_HEADS, GLA_DK, GLA_DV), lambda bi, ti: (bi, 0, 0, 0)),
            pl.BlockSpec((1, POOL_HIST, POOL_WIDTH), lambda bi, ti: (bi, 0, 0)),
            const((1, d)), const((1, d)),
            const((d, D_IN_PAD)), const((GATE_PAD, GLA_KEY)), const((1, GLA_KEY)), const((1, GLA_DV)),
            const((len(POOL_WINDOWS), POOL_GC, POOL_GC)), const((1, POOL_WIDTH)),
            const((GLA_WIDTH + POOL_WIDTH, d)), const((1, d)), const((1, d)), const((tl, tl)),
        ],
        out_specs=[
            pl.BlockSpec((1, tl, d), lambda bi, ti: (bi, ti, 0)),
            pl.BlockSpec((1, GLA_HEADS, GLA_DK, GLA_DV), lambda bi, ti: (bi, 0, 0, 0)),
            pl.BlockSpec((1, POOL_HIST, POOL_WIDTH), lambda bi, ti: (bi, 0, 0)),
        ],
        out_shape=[
            jax.ShapeDtypeStruct((b, l, d), F32),
            jax.ShapeDtypeStruct((b, GLA_HEADS, GLA_DK, GLA_DV), F32),
            jax.ShapeDtypeStruct((b, POOL_HIST, POOL_WIDTH), F32),
        ],
        scratch_shapes=[
            pltpu.VMEM((tl, D_IN_PAD), F32),
            pltpu.VMEM((HIST_ROWS + tl, POOL_WIDTH), F32),
            pltpu.VMEM((GLA_HEADS, GLA_DK, GLA_DV), F32),
            pltpu.VMEM((tl, GLA_WIDTH + POOL_WIDTH), BF16),
        ],
        compiler_params=pltpu.CompilerParams(
            dimension_semantics=("arbitrary", "arbitrary"), vmem_limit_bytes=VMEM_LIMIT),
        name="trunk_a",
    )(x, s0, h0, *wts, tri)


def _prep_trunk_a_weights(ln_in_g, ln_in_b, w_in, w_gate_up, b_gate, gla_norm_g, pool_w, pool_scale, w_out,
                          ln1_g, ln1_b):
    d = D_MODEL
    p_gd = 2 * GLA_KEY + 2 * GLA_WIDTH
    w_in_r = jnp.concatenate(
        [w_in[:, :p_gd], w_in[:, p_gd + GATE_RANK:], w_in[:, p_gd:p_gd + GATE_RANK],
         jnp.zeros((d, GATE_PAD - GATE_RANK), w_in.dtype)], axis=1).astype(BF16)
    wgu = jnp.concatenate([w_gate_up, jnp.zeros((GATE_PAD - GATE_RANK, GLA_KEY), w_gate_up.dtype)],
                          axis=0).astype(BF16)
    return (ln_in_g.reshape(1, d), ln_in_b.reshape(1, d), w_in_r, wgu, b_gate.reshape(1, GLA_KEY),
            gla_norm_g.reshape(1, GLA_DV), pool_w.astype(BF16), pool_scale.reshape(1, POOL_WIDTH),
            w_out.astype(BF16), ln1_g.reshape(1, d), ln1_b.reshape(1, d))


def _mem_kv_kernel(m_ref, wk_ref, wv_ref, k_ref, v_ref, kb_ref, vb_ref):
    m = m_ref[...].astype(BF16)
    k = _dot(m, wk_ref[...])
    v = _dot(m, wv_ref[...])
    k_ref[...] = k
    v_ref[...] = v
    kb_ref[...] = k.astype(BF16)
    vb_ref[...] = v.astype(BF16)


def _mem_kv(mem, wk, wv):
    m, d = mem.shape
    tm = min(m, 512)
    assert m % tm == 0
    row = pl.BlockSpec((tm, d), lambda i: (i, 0))
    wspec = pl.BlockSpec((d, d), lambda i: (0, 0))
    return pl.pallas_call(
        _mem_kv_kernel,
        grid=(m // tm,),
        in_specs=[row, wspec, wspec],
        out_specs=[row, row, row, row],
        out_shape=[jax.ShapeDtypeStruct((m, d), F32), jax.ShapeDtypeStruct((m, d), F32),
                   jax.ShapeDtypeStruct((m, d), BF16), jax.ShapeDtypeStruct((m, d), BF16)],
        compiler_params=pltpu.CompilerParams(dimension_semantics=("arbitrary",), vmem_limit_bytes=VMEM_LIMIT),
        name="mem_kv",
    )(mem, wk, wv)


def _attn_kernel(x1_ref, k_ref, v_ref, wq_ref, wo_ref, g_ref, b_ref, x2_ref, o_scr):
    x1 = x1_ref[0]
    q = (_dot(x1.astype(BF16), wq_ref[...]) * (MEM_DH ** -0.5)).astype(BF16)
    for h in range(MEM_HEADS):
        hs = slice(h * MEM_DH, (h + 1) * MEM_DH)
        s = _dot_nt(q[:, hs], k_ref[0, :, hs])
        e = jnp.exp(s - jnp.max(s, axis=-1, keepdims=True))
        p = e / jnp.sum(e, axis=-1, keepdims=True)
        o_scr[:, hs] = _dot(p.astype(BF16), v_ref[0, :, hs]).astype(BF16)
    attn = _dot(o_scr[...], wo_ref[...])
    x2_ref[...] = _ln(x1 * ALPHA + attn, g_ref[...], b_ref[...])


def _attn(x1, mem_k, mem_v, wq, wo, g, b):
    bsz, l, d = x1.shape
    tl = min(l, TOKEN_TILE)
    assert l % tl == 0
    nt = l // tl
    const = lambda shape: pl.BlockSpec(shape, lambda bi, ti: (0,) * len(shape))
    return pl.pallas_call(
        _attn_kernel,
        grid=(bsz, nt),
        in_specs=[
            pl.BlockSpec((1, tl, d), lambda bi, ti: (bi, ti, 0)),
            pl.BlockSpec((1, N_MEM, d), lambda bi, ti: (bi, 0, 0)),
            pl.BlockSpec((1, N_MEM, d), lambda bi, ti: (bi, 0, 0)),
            const((d, d)), const((d, d)), const((1, d)), const((1, d)),
        ],
        out_specs=pl.BlockSpec((tl, d), lambda bi, ti: (bi * nt + ti, 0)),
        out_shape=jax.ShapeDtypeStruct((bsz * l, d), F32),
        scratch_shapes=[pltpu.VMEM((tl, d), BF16)],
        compiler_params=pltpu.CompilerParams(
            dimension_semantics=("arbitrary", "arbitrary"), vmem_limit_bytes=VMEM_LIMIT),
        name="attn",
    )(x1, mem_k, mem_v, wq, wo, g, b)


def _two_part_specs(t, d, tiles_a):
    spec_a = pl.BlockSpec((t, d), lambda i, *_: (jnp.minimum(i, tiles_a - 1), 0))
    spec_b = pl.BlockSpec((t, d), lambda i, *_: (jnp.maximum(i - tiles_a, 0), 0))
    return spec_a, spec_b


def _router_kernel(xa_ref, xb_ref, wrt_ref, bias_ref, e_ref, g_ref, r_ref, cnt_ref, cnt_scr, *, tl, tiles_a):
    i = pl.program_id(0)

    @pl.when(i == 0)
    def _():
        cnt_scr[...] = jnp.zeros_like(cnt_scr)

    x = jnp.where(i < tiles_a, xa_ref[...], xb_ref[...])
    xh = x.astype(BF16)
    xm = (x - xh.astype(F32)).astype(BF16)
    wh = wrt_ref[0]
    wm = wrt_ref[1]
    logits = _dot_nt(wh, xh) + (_dot_nt(wh, xm) + _dot_nt(wm, xh))
    scores = jax.nn.sigmoid(logits)
    biased = scores + bias_ref[...]
    ninf = -jnp.inf
    eg = EXPERTS_PER_GROUP
    riota = lax.broadcasted_iota(I32, (eg, tl), 0)
    gs_rows = []
    for g in range(N_GROUPS):
        blk = biased[g * eg:(g + 1) * eg, :]
        m1 = jnp.max(blk, axis=0, keepdims=True)
        i1 = jnp.min(jnp.where(blk == m1, riota, eg), axis=0, keepdims=True)
        m2 = jnp.max(jnp.where(riota == i1, ninf, blk), axis=0, keepdims=True)
        gs_rows.append(m1 + m2)
    gs = jnp.concatenate(gs_rows, axis=0)
    giota = lax.broadcasted_iota(I32, (N_GROUPS, tl), 0)
    sel = jnp.zeros((N_GROUPS, tl), jnp.bool_)
    for _ in range(TOPK_GROUPS):
        m = jnp.max(gs, axis=0, keepdims=True)
        gi = jnp.min(jnp.where(gs == m, giota, N_GROUPS), axis=0, keepdims=True)
        hit = giota == gi
        sel = jnp.logical_or(sel, hit)
        gs = jnp.where(hit, ninf, gs)
    self = jnp.where(sel, 1.0, 0.0)
    masked = jnp.concatenate(
        [jnp.where(self[g:g + 1, :] > 0.5, biased[g * eg:(g + 1) * eg, :], ninf) for g in range(N_GROUPS)], axis=0)
    eiota = lax.broadcasted_iota(I32, (N_EXPERTS, tl), 0)
    idx_rows, sc_rows = [], []
    multi = jnp.zeros((N_EXPERTS, tl), F32)
    for _ in range(TOP_K):
        m = jnp.max(masked, axis=0, keepdims=True)
        idx = jnp.min(jnp.where(masked == m, eiota, N_EXPERTS), axis=0, keepdims=True)
        hit = eiota == idx
        sc_rows.append(jnp.sum(jnp.where(hit, scores, 0.0), axis=0, keepdims=True))
        idx_rows.append(idx)
        multi = jnp.where(hit, 1.0, multi)
        masked = jnp.where(hit, ninf, masked)
    top_e = jnp.concatenate(idx_rows, axis=0)
    sc = jnp.concatenate(sc_rows, axis=0)
    e_ref[...] = top_e
    g_ref[...] = sc / jnp.sum(sc, axis=0, keepdims=True) * ROUTED_SCALE
    mh = multi.astype(BF16)
    before = (lax.broadcasted_iota(I32, (tl, tl), 0) < lax.broadcasted_iota(I32, (tl, tl), 1)).astype(BF16)
    running = cnt_scr[...]
    rankmat = _dot(mh, before) + jnp.concatenate([running] * (tl // 128), axis=1)
    r_rows = [jnp.sum(jnp.where(eiota == idx_rows[k], rankmat, 0.0), axis=0, keepdims=True) for k in range(TOP_K)]
    r_ref[...] = jnp.concatenate(r_rows, axis=0).astype(I32)
    total = running + _dot(mh, jnp.ones((tl, 128), BF16))
    cnt_scr[...] = total
    cnt_ref[...] = total


def _router(x2a, x2b, wrt, bias_col):
    d = x2a.shape[1]
    tl = TOKEN_TILE
    assert x2a.shape[0] % tl == 0 and x2b.shape[0] % tl == 0
    tiles_a = x2a.shape[0] // tl
    n = x2a.shape[0] + x2b.shape[0]
    kspec = pl.BlockSpec((TOP_K, tl), lambda i: (0, i))
    return pl.pallas_call(
        functools.partial(_router_kernel, tl=tl, tiles_a=tiles_a),
        grid=(n // tl,),
        in_specs=[*_two_part_specs(tl, d, tiles_a),
                  pl.BlockSpec((2, N_EXPERTS, d), lambda i: (0, 0, 0)),
                  pl.BlockSpec((N_EXPERTS, 1), lambda i: (0, 0))],
        out_specs=[kspec, kspec, kspec, pl.BlockSpec((N_EXPERTS, 128), lambda i: (0, 0))],
        out_shape=[jax.ShapeDtypeStruct((TOP_K, n), I32), jax.ShapeDtypeStruct((TOP_K, n), F32),
                   jax.ShapeDtypeStruct((TOP_K, n), I32), jax.ShapeDtypeStruct((N_EXPERTS, 128), F32)],
        scratch_shapes=[pltpu.VMEM((N_EXPERTS, 128), F32)],
        compiler_params=pltpu.CompilerParams(dimension_semantics=("arbitrary",), vmem_limit_bytes=VMEM_LIMIT),
        name="router",
    )(x2a, x2b, wrt, bias_col)


PACK_ROWS = D_MODEL // 2 // LANES
HI_MASK = 0xFFFF0000


def _slots_of(buf_ref, start, n=None):
    if n is None:
        return buf_ref.at[start, pl.ds(0, PACK_ROWS)]
    return buf_ref.at[pl.ds(start, n), pl.ds(0, PACK_ROWS)]


def _word_rows(buf_ref):
    if len(buf_ref.shape) == 3:
        return buf_ref.reshape(buf_ref.shape[0] * SUBLANES, LANES), SUBLANES
    return buf_ref, PACK_ROWS


def _pack_rows(x, dst_ref, row0=0):
    t = x.shape[0]
    bits = lax.bitcast_convert_type(x.astype(BF16).astype(F32), U32)
    flat, pitch = _word_rows(dst_ref)
    for j in range(PACK_ROWS):
        lo = lax.shift_right_logical(bits[:, j * LANES:(j + 1) * LANES], jnp.uint32(16))
        hi = bits[:, D_MODEL // 2 + j * LANES:D_MODEL // 2 + (j + 1) * LANES] & jnp.uint32(HI_MASK)
        flat[pl.ds(row0 * pitch + j, t, stride=pitch), :] = hi | lo


def _unpack_words(w):
    lo = lax.bitcast_convert_type(lax.shift_left(w, jnp.uint32(16)), F32)
    hi = lax.bitcast_convert_type(w & jnp.uint32(HI_MASK), F32)
    return lo, hi


def _unpack_rows(src_ref, t, row0=0):
    flat, pitch = _word_rows(src_ref)
    parts = [_unpack_words(flat[pl.ds(row0 * pitch + j, t, stride=pitch), :]) for j in range(PACK_ROWS)]
    return jnp.concatenate([p[0] for p in parts] + [p[1] for p in parts], axis=1)


def _dispatch_kernel(fs_ref, fl_ref, nu_ref, dest_ref, xa_ref, xb_ref, xs_ref, stage, zbuf, sem, zsem,
                     *, t, tm, tiles_a, n_blk):
    i = pl.program_id(0)
    nsteps = pl.num_programs(0)

    def zero_fill(start):
        def go(cp):
            if start:
                cp.start()
            else:
                cp.wait()

        def per_expert(e, carry):
            ln = fl_ref[e]
            off = fs_ref[e]
            bit = tm // 2
            while bit:
                @pl.when((ln & bit) != 0)
                def _(off=off, bit=bit):
                    go(pltpu.make_async_copy(_slots_of(zbuf, 0, bit), xs_ref.at[pl.ds(off, bit)], zsem))
                off = off + (ln & bit)
                bit //= 2
            return carry

        lax.fori_loop(0, N_EXPERTS, per_expert, 0)

        def per_block(b, carry):
            go(pltpu.make_async_copy(_slots_of(zbuf, 0, tm), xs_ref.at[pl.ds(b * tm, tm)], zsem))
            return carry

        lax.fori_loop(nu_ref[0], n_blk, per_block, 0)

    @pl.when(i == 0)
    def _():
        zbuf[...] = jnp.zeros_like(zbuf)
        zero_fill(True)

    def drain(slot):
        for k in range(TOP_K):
            pltpu.make_async_copy(_slots_of(stage.at[slot], 0, t), xs_ref.at[pl.ds(0, t)], sem.at[slot]).wait()

    def step(slot):
        @pl.when(i >= 2)
        def _():
            drain(slot)

        _pack_rows(jnp.where(i < tiles_a, xa_ref[...], xb_ref[...]), stage.at[slot])

        def body(n, carry):
            src = _slots_of(stage.at[slot], n)
            for k in range(TOP_K):
                dst = xs_ref.at[dest_ref[n * TOP_K + k]]
                pltpu.make_async_copy(src, dst, sem.at[slot]).start(priority=k % 2)
            return carry

        lax.fori_loop(0, t, body, 0)

    for slot in range(2):
        @pl.when(i % 2 == slot)
        def _(slot=slot):
            step(slot)

    @pl.when(i == 0)
    def _():
        zero_fill(False)

    @pl.when(i == nsteps - 1)
    def _():
        for slot in range(2):
            @pl.when(jnp.logical_or(nsteps >= 2, (nsteps - 1) % 2 == slot))
            def _(slot=slot):
                drain(slot)


def _dispatch(fill_start, fill_len, nu, dest, x2a, x2b, *, n_blk, tm):
    d = x2a.shape[1]
    t = TOKEN_TILE
    assert x2a.shape[0] % t == 0 and x2b.shape[0] % t == 0
    tiles_a = x2a.shape[0] // t
    n = x2a.shape[0] + x2b.shape[0]
    return pl.pallas_call(
        functools.partial(_dispatch_kernel, t=t, tm=tm, tiles_a=tiles_a, n_blk=n_blk),
        grid_spec=pltpu.PrefetchScalarGridSpec(
            num_scalar_prefetch=3,
            grid=(n // t,),
            in_specs=[pl.BlockSpec((t * TOP_K,), lambda i, *_: (i,), memory_space=pltpu.SMEM),
                      *_two_part_specs(t, d, tiles_a)],
            out_specs=pl.BlockSpec(memory_space=pl.ANY),
            scratch_shapes=[pltpu.VMEM((2, t, SUBLANES, LANES), U32), pltpu.VMEM((tm, SUBLANES, LANES), U32),
                            pltpu.SemaphoreType.DMA((2,)), pltpu.SemaphoreType.DMA(())],
        ),
        out_shape=jax.ShapeDtypeStruct((n_blk * tm, PACK_ROWS, LANES), U32),
        compiler_params=pltpu.CompilerParams(dimension_semantics=("arbitrary",), vmem_limit_bytes=VMEM_LIMIT),
        name="dispatch",
    )(fill_start, fill_len, nu, dest, x2a, x2b)


def _experts_kernel(b0_ref, nb_ref, cnt_ref, nu_ref, w1_ref, w3_ref, w2_ref, xs_ref, ys_ref,
                    w1b, w3b, w2b, xbuf, ybuf, in_sem, out_sem, *, tm, n_blk):
    e = pl.program_id(0)
    n_used = nu_ref[0]

    blk_rows = tm * PACK_ROWS

    def hbm_block(ref, b):
        return ref.at[pl.ds(pl.multiple_of(b * blk_rows, blk_rows), blk_rows)]

    def x_copy(b):
        s = lax.rem(b, X_DEPTH)
        return pltpu.make_async_copy(hbm_block(xs_ref, b), xbuf.at[s], in_sem.at[s])

    def y_copy(b):
        return pltpu.make_async_copy(ybuf.at[b % 2], hbm_block(ys_ref, b), out_sem.at[b % 2])

    @pl.when(e == 0)
    def _():
        for b in range(X_DEPTH - 1):
            @pl.when(b < n_used)
            def _(b=b):
                x_copy(b).start()

    w1b[...] = w1_ref[0].astype(BF16)
    w3b[...] = w3_ref[0].astype(BF16)
    w2b[...] = w2_ref[0].astype(BF16)
    b0 = b0_ref[e]
    cnt = cnt_ref[e]

    def block(j, carry):
        b = b0 + j
        x_copy(b).wait()

        @pl.when(b + X_DEPTH - 1 < n_used)
        def _():
            x_copy(b + X_DEPTH - 1).start()

        @pl.when(b >= 2)
        def _():
            y_copy(b - 2).wait()

        rows = tm // EXPERT_SPLIT
        for r0 in range(0, tm, rows):
            live = lax.broadcasted_iota(I32, (rows, 1), 0) < cnt - j * tm - r0
            x = jnp.where(live, _unpack_rows(xbuf.at[lax.rem(b, X_DEPTH)], rows, r0), 0.0).astype(BF16)
            a = _dot(x, w1b[...])
            h = (a * jax.nn.sigmoid(a)) * _dot(x, w3b[...])
            _pack_rows(_dot(h.astype(BF16), w2b[...]), ybuf.at[b % 2], r0)
        y_copy(b).start()
        return carry

    lax.fori_loop(0, nb_ref[e], block, 0)

    @pl.when(e == pl.num_programs(0) - 1)
    def _():
        @pl.when(n_used >= 2)
        def _():
            y_copy(n_used - 2).wait()

        y_copy(n_used - 1).wait()
        ybuf[0] = jnp.zeros(ybuf.shape[1:], U32)

        def zero_copy(b):
            return pltpu.make_async_copy(ybuf.at[0], hbm_block(ys_ref, b), out_sem.at[0])

        def zero(b, carry):
            zero_copy(b).start()
            return carry

        lax.fori_loop(n_used, n_blk, zero, 0)

        def zero_wait(b, carry):
            zero_copy(b).wait()
            return carry

        lax.fori_loop(n_used, n_blk, zero_wait, 0)


def _experts(blk0, nblk, counts, nu, xs, w1, w3, w2, *, tm):
    n_slot = xs.shape[0]
    n_blk = n_slot // tm
    n_exp, d, de = w1.shape
    wmap = lambda e, *_: (e, 0, 0)
    ys = pl.pallas_call(
        functools.partial(_experts_kernel, tm=tm, n_blk=n_blk),
        grid_spec=pltpu.PrefetchScalarGridSpec(
            num_scalar_prefetch=4,
            grid=(n_exp,),
            in_specs=[pl.BlockSpec((1, d, de), wmap), pl.BlockSpec((1, d, de), wmap), pl.BlockSpec((1, de, d), wmap),
                      pl.BlockSpec(memory_space=pl.ANY)],
            out_specs=pl.BlockSpec(memory_space=pl.ANY),
            scratch_shapes=[pltpu.VMEM((d, de), BF16), pltpu.VMEM((d, de), BF16), pltpu.VMEM((de, d), BF16),
                            pltpu.VMEM((X_DEPTH, tm * PACK_ROWS, LANES), U32), pltpu.VMEM((2, tm * PACK_ROWS, LANES), U32),
                            pltpu.SemaphoreType.DMA((X_DEPTH,)), pltpu.SemaphoreType.DMA((2,))],
        ),
        out_shape=jax.ShapeDtypeStruct((n_slot * PACK_ROWS, LANES), U32),
        compiler_params=pltpu.CompilerParams(dimension_semantics=("arbitrary",), vmem_limit_bytes=VMEM_LIMIT),
        name="experts",
    )(blk0, nblk, counts, nu, w1, w3, w2, xs.reshape(n_slot * PACK_ROWS, LANES))
    return ys.reshape(n_slot, PACK_ROWS, LANES)


def _combine_kernel(dcur_ref, dnxt_ref, x2_ref, g_ref, w1_ref, w3_ref, w2_ref, lg_ref, lb_ref, ys_ref,
                    out_ref, buf, acc_scr, sem, *, t, nsteps):
    i = pl.program_id(0)

    def fetch(d_ref, s, n):
        for k in range(TOP_K):
            pltpu.make_async_copy(ys_ref.at[d_ref[n * TOP_K + k]], _slots_of(buf.at[s, k], n),
                                  sem.at[s]).start(priority=k % 2)

    def drain(s):
        for k in range(TOP_K):
            pltpu.make_async_copy(ys_ref.at[pl.ds(0, t)], _slots_of(buf.at[s, k], 0, t), sem.at[s]).wait()

    @pl.when(i == 0)
    def _():
        def first(n, carry):
            fetch(dcur_ref, 0, n)
            return carry

        lax.fori_loop(0, t, first, 0)

    def step(s):
        drain(s)

        def group(gi, carry):
            r0 = pl.multiple_of(gi * SUBLANES, SUBLANES)
            for j in range(SUBLANES):
                fetch(dnxt_ref, 1 - s, r0 + j)
            gates = g_ref[pl.ds(r0, SUBLANES), :]
            gk = [jnp.broadcast_to(gates[:, k:k + 1], (SUBLANES, LANES)) for k in range(TOP_K)]
            planes = [buf.at[s, k].reshape(t * SUBLANES, LANES) for k in range(TOP_K)]
            for j in range(PACK_ROWS):
                acc_lo = acc_hi = None
                for k in range(TOP_K):
                    lo, hi = _unpack_words(planes[k][pl.ds(r0 * SUBLANES + j, SUBLANES, stride=SUBLANES), :])
                    acc_lo = gk[k] * lo if acc_lo is None else acc_lo + gk[k] * lo
                    acc_hi = gk[k] * hi if acc_hi is None else acc_hi + gk[k] * hi
                acc_scr[pl.ds(r0, SUBLANES), j * LANES:(j + 1) * LANES] = acc_lo
                acc_scr[pl.ds(r0, SUBLANES), D_MODEL // 2 + j * LANES:D_MODEL // 2 + (j + 1) * LANES] = acc_hi
            return carry

        lax.fori_loop(0, t // SUBLANES, group, 0)

    for s in range(2):
        @pl.when(i % 2 == s)
        def _(s=s):
            step(s)

    @pl.when(i == nsteps - 1)
    def _():
        drain(nsteps % 2)

    x2 = x2_ref[...]
    acc = acc_scr[...]
    xb = x2.astype(BF16)
    a = _dot(xb, w1_ref[...])
    hs = (a * jax.nn.sigmoid(a)) * _dot(xb, w3_ref[...])
    shared = _dot(hs.astype(BF16), w2_ref[...])
    out_ref[...] = _ln(x2 * ALPHA + (acc + shared), lg_ref[...], lb_ref[...])


def _combine(dest, x2, gate_t, w1s, w3s, w2s, lg, lb, ys, *, row_off):
    n_rows, d = x2.shape
    n = dest.shape[0] // TOP_K
    t = TOKEN_TILE
    assert row_off % t == 0 and n_rows % t == 0 and n % t == 0
    off = row_off // t
    last = n // t - 1
    de = w1s.shape[-1]
    const = lambda shape: pl.BlockSpec(shape, lambda i: (0,) * len(shape))
    return pl.pallas_call(
        functools.partial(_combine_kernel, t=t, nsteps=n_rows // t),
        grid=(n_rows // t,),
        in_specs=[pl.BlockSpec((t * TOP_K,), lambda i: (off + i,), memory_space=pltpu.SMEM),
                  pl.BlockSpec((t * TOP_K,), lambda i: (jnp.minimum(off + i + 1, last),), memory_space=pltpu.SMEM),
                  pl.BlockSpec((t, d), lambda i: (i, 0)),
                  pl.BlockSpec((t, TOP_K), lambda i: (off + i, 0)),
                  const((d, de)), const((d, de)), const((de, d)), const((1, d)), const((1, d)),
                  pl.BlockSpec(memory_space=pl.ANY)],
        out_specs=pl.BlockSpec((t, d), lambda i: (i, 0)),
        out_shape=jax.ShapeDtypeStruct((n_rows, d), F32),
        scratch_shapes=[pltpu.VMEM((2, TOP_K, t, SUBLANES, LANES), U32), pltpu.VMEM((t, d), F32),
                        pltpu.SemaphoreType.DMA((2,))],
        compiler_params=pltpu.CompilerParams(dimension_semantics=("arbitrary",), vmem_limit_bytes=VMEM_LIMIT),
        name="combine",
    )(dest, dest, x2, gate_t, w1s, w3s, w2s, lg, lb, ys)


EXPERT_TM = 512
EXPERT_SPLIT = 2
X_DEPTH = 4


def _moe_plan(cnt):
    tm = EXPERT_TM
    counts = cnt[:, 0].astype(I32)
    padded = (counts + tm - 1) // tm * tm
    pad_end = jnp.cumsum(padded)
    pad_start = pad_end - padded
    n_used = pad_end[-1] // tm
    fill_start = (pad_start + counts).astype(I32)
    fill_len = (padded - counts).astype(I32)
    return (pad_start.astype(I32), (pad_start // tm).astype(I32), (padded // tm).astype(I32), counts,
            n_used.reshape(1).astype(I32), fill_start, fill_len)


def _slots_kernel(ps_ref, e_ref, r_ref, d_ref):
    e = e_ref[...]

    def body(j, base):
        return jnp.where(e == j, ps_ref[j], base)

    base = lax.fori_loop(0, N_EXPERTS, body, jnp.zeros_like(e), unroll=8)
    d_ref[...] = base + r_ref[...]


def _slots(pad_start, top_e, rank):
    k, n = top_e.shape
    tl = next(c for c in (2048, 1536, 1024, 512, 256, 128) if n % c == 0)
    spec = pl.BlockSpec((k, tl), lambda i, ps: (0, i))
    return pl.pallas_call(
        _slots_kernel,
        grid_spec=pltpu.PrefetchScalarGridSpec(num_scalar_prefetch=1, grid=(n // tl,), in_specs=[spec, spec],
                                               out_specs=spec),
        out_shape=jax.ShapeDtypeStruct((k, n), I32),
        compiler_params=pltpu.CompilerParams(dimension_semantics=("arbitrary",)),
        name="slots",
    )(pad_start, top_e, rank)


PAST_LEN = 1024


def kernel(x_prompt, x_sample, state_gla, cache_pool, cache_mem_k, cache_mem_v, mem_prompt, ln_in_g, ln_in_b, w_in, w_gate_up, b_gate, gla_norm_g, pool_w, pool_scale, w_out, ln1_g, ln1_b, wq_mem, wk_mem, wv_mem, wo_mem, ln2_g, ln2_b, w_router, router_bias, w1_exp, w3_exp, w2_exp, w1_sh, w3_sh, w2_sh, ln3_g, ln3_b):
    assert w_in.shape[0] == 1, "single-layer trunk"
    bp, lp, d = x_prompt.shape
    bs, ls, _ = x_sample.shape
    n_p, n_s = bp * lp, bs * ls
    n_all = n_p + n_s

    wts_a = _prep_trunk_a_weights(ln_in_g, ln_in_b, w_in[0], w_gate_up[0], b_gate[0], gla_norm_g[0], pool_w[0],
                                  pool_scale[0], w_out[0], ln1_g[0], ln1_b[0])
    s0 = jnp.zeros((bp, GLA_HEADS, GLA_DK, GLA_DV), F32)
    h0 = jnp.zeros((bp, POOL_HIST, POOL_WIDTH), F32)
    x1p, sp, hp = _trunk_a(x_prompt, s0, h0, wts_a, start_pos=0)
    x1s, ss, hs = _trunk_a(x_sample, state_gla[0], cache_pool[0], wts_a, start_pos=PAST_LEN)

    mk, mv, mkb, mvb = _mem_kv(mem_prompt.reshape(bp * N_MEM, d), wk_mem[0].astype(BF16), wv_mem[0].astype(BF16))
    wq, wo = wq_mem[0].astype(BF16), wo_mem[0].astype(BF16)
    g2, b2 = ln2_g[0].reshape(1, d), ln2_b[0].reshape(1, d)
    x2p = _attn(x1p, mkb.reshape(bp, N_MEM, d), mvb.reshape(bp, N_MEM, d), wq, wo, g2, b2)
    x2s = _attn(x1s, cache_mem_k[0].reshape(bs, N_MEM, d).astype(BF16),
                cache_mem_v[0].reshape(bs, N_MEM, d).astype(BF16), wq, wo, g2, b2)

    wrt = w_router[0].T
    wrt_h = wrt.astype(BF16)
    wrt_m = (wrt - wrt_h.astype(F32)).astype(BF16)
    top_e, gate, rank, cnt = _router(x2p, x2s, jnp.stack([wrt_h, wrt_m]), router_bias[0].reshape(N_EXPERTS, 1))

    assert (n_all * TOP_K) % EXPERT_TM == 0
    n_blk = n_all * TOP_K // EXPERT_TM + N_EXPERTS
    pad_start, blk0, nblk, counts, nu, fill_start, fill_len = _moe_plan(cnt)
    dest = _slots(pad_start, top_e, rank).T.reshape(-1)
    xs = _dispatch(fill_start, fill_len, nu, dest, x2p, x2s, n_blk=n_blk, tm=EXPERT_TM)
    ys = _experts(blk0, nblk, counts, nu, xs, w1_exp[0], w3_exp[0], w2_exp[0], tm=EXPERT_TM)
    sh = (w1_sh[0].astype(BF16), w3_sh[0].astype(BF16), w2_sh[0].astype(BF16),
          ln3_g[0].reshape(1, d), ln3_b[0].reshape(1, d))
    gate_t = gate.T
    yp = _combine(dest, x2p, gate_t, *sh, ys, row_off=0)
    ysm = _combine(dest, x2s, gate_t, *sh, ys, row_off=n_p)

    return (yp.reshape(bp, lp, d), ysm.reshape(bs, ls, d), sp[None], hp[None],
            mk.reshape(1, bp, N_MEM, MEM_HEADS, MEM_DH), mv.reshape(1, bp, N_MEM, MEM_HEADS, MEM_DH),
            ss[None], hs[None])
```

```python
import functools

import jax
import jax.numpy as jnp
from jax import lax
from jax.experimental import pallas as pl
from jax.experimental.pallas import tpu as pltpu

F32 = jnp.float32
BF16 = jnp.bfloat16
I32 = jnp.int32

D_MODEL = 1024
CHUNK = 64
SUB = 16
GLA_HEADS = 4
GLA_DK = 64
GLA_DV = 128
GLA_KEY = GLA_HEADS * GLA_DK
GLA_WIDTH = GLA_HEADS * GLA_DV
GATE_RANK = 16
GATE_PAD = 128
POOL_WIDTH = 512
POOL_WINDOWS = (2, 4, 8, 16)
POOL_GC = 128
POOL_HIST = 15
HIST_ROWS = 16
N_MEM = 256
MEM_HEADS = 4
MEM_DH = 256
N_EXPERTS = 256
N_GROUPS = 8
EXPERTS_PER_GROUP = 32
TOPK_GROUPS = 4
TOP_K = 8
ROUTED_SCALE = 2.5
TOKEN_TILE = 512
SUBLANES, LANES = 8, 128
assert D_MODEL == SUBLANES * LANES
ALPHA = 2.0 ** 0.25
EPS = 1e-5
OFF_Q, OFF_K, OFF_V, OFF_G, OFF_U, OFF_GD = 0, 256, 512, 1024, 1536, 2048
D_IN_PAD = OFF_GD + GATE_PAD

VMEM_LIMIT = 56 * 1024 * 1024


def _ln(x, g, b):
    mu = jnp.mean(x, axis=-1, keepdims=True)
    xc = x - mu
    var = jnp.mean(xc * xc, axis=-1, keepdims=True)
    return xc * lax.rsqrt(var + EPS) * g + b


def _dot(a, b):
    return jnp.dot(a, b, preferred_element_type=F32)


def _dot_nt(a, b):
    return lax.dot_general(a, b, (((1,), (1,)), ((), ())), preferred_element_type=F32)


def _dot_tn(a, b):
    return lax.dot_general(a, b, (((0,), (0,)), ((), ())), preferred_element_type=F32)


def _split3(x):
    h = x.astype(BF16)
    r = x - h.astype(F32)
    m = r.astype(BF16)
    l = (r - m.astype(F32)).astype(BF16)
    return h, m, l


def _trunk_a_kernel(x_ref, s0_ref, h0_ref, lng_ref, lnb_ref, win_ref, wgu_ref, bg_ref, gng_ref,
                    pw_ref, ps_ref, wout_ref, l1g_ref, l1b_ref, tri_ref,
                    x1_ref, sn_ref, hn_ref,
                    proj_scr, ext_scr, s_scr, op_scr, *, tl, chunk, start_pos):
    t = pl.program_id(1)
    nt = pl.num_programs(1)

    @pl.when(t == 0)
    def _():
        s_scr[...] = s0_ref[0]
        ext_scr[0:1, :] = jnp.zeros((1, POOL_WIDTH), F32)
        ext_scr[1:HIST_ROWS, :] = h0_ref[0]

    xn = _ln(x_ref[0], lng_ref[...], lnb_ref[...])
    proj_scr[...] = _dot(xn.astype(BF16), win_ref[...])

    c = chunk
    shift = lambda a, n: lax.shift_right_logical(a, n.bit_length() - 1)
    gd = proj_scr[:, OFF_GD:OFF_GD + GATE_PAD]
    z = _dot(gd.astype(BF16), wgu_ref[...]) + bg_ref[...]
    lf = (jnp.minimum(z, 0.0) - jnp.log(1.0 + jnp.exp(-jnp.abs(z)))) * (1.0 / 16.0)
    tri = tri_ref[...]
    lh, lm, ll = _split3(lf)
    cum = _dot(tri, lh) + _dot(tri, lm) + _dot(tri, ll)
    q_all = proj_scr[:, OFF_Q:OFF_Q + GLA_KEY] * (GLA_DK ** -0.5)
    k_all = proj_scr[:, OFF_K:OFF_K + GLA_KEY]
    qs_all = q_all * jnp.exp(cum)

    causal = lax.broadcasted_iota(I32, (c, c), 1) <= lax.broadcasted_iota(I32, (c, c), 0)
    eye_dk = lax.broadcasted_iota(I32, (GLA_DK, GLA_DK), 0) == lax.broadcasted_iota(I32, (GLA_DK, GLA_DK), 1)
    n_sub = c // SUB
    rblk = shift(lax.broadcasted_iota(I32, (c, n_sub * GLA_DK), 0), SUB)
    lblk = shift(lax.broadcasted_iota(I32, (c, n_sub * GLA_DK), 1), GLA_DK)
    lblk_row = shift(lax.broadcasted_iota(I32, (1, n_sub * GLA_DK), 1), GLA_DK)
    mask_q = rblk == lblk
    mask_k = rblk <= lblk

    def tile_lanes(a):
        return jnp.concatenate([a] * n_sub, axis=1)

    states = [s_scr[h] for h in range(GLA_HEADS)]
    units = [(ci, h) for ci in range(tl // c) for h in range(GLA_HEADS)]
    att_l, kv_l, dcol_l, v_l = {}, {}, {}, {}
    for ci, h in units:
        rs = slice(ci * c, (ci + 1) * c)
        ksl = slice(h * GLA_DK, (h + 1) * GLA_DK)
        v_l[ci, h] = proj_scr[rs, OFF_V + h * GLA_DV:OFF_V + (h + 1) * GLA_DV].astype(BF16)
        cum_t = tile_lanes(cum[rs, ksl])
        q_t = tile_lanes(q_all[rs, ksl])
        k_t = tile_lanes(k_all[rs, ksl])
        ref_row = jnp.zeros((1, n_sub * GLA_DK), F32)
        for i in range(1, n_sub):
            ref_row = jnp.where(lblk_row == i, cum_t[i * SUB - 1:i * SUB, :], ref_row)
        arg = cum_t - ref_row
        lhs = jnp.where(mask_q, q_t * jnp.exp(jnp.where(mask_q, arg, 0.0)), 0.0)
        rhs = jnp.where(mask_k, k_t * jnp.exp(jnp.where(mask_k, -arg, 0.0)), 0.0)
        att_l[ci, h] = jnp.where(causal, _dot_nt(lhs.astype(BF16), rhs.astype(BF16)), 0.0).astype(BF16)
    for ci, h in units:
        rs = slice(ci * c, (ci + 1) * c)
        ksl = slice(h * GLA_DK, (h + 1) * GLA_DK)
        last = cum[(ci + 1) * c - 1:(ci + 1) * c, ksl]
        ks_h = k_all[rs, ksl] * jnp.exp(last - cum[rs, ksl])
        kv_l[ci, h] = _dot_tn(ks_h.astype(BF16), v_l[ci, h])
        dcol_l[ci, h] = jnp.sum(jnp.where(eye_dk, jnp.broadcast_to(jnp.exp(last), (GLA_DK, GLA_DK)), 0.0),
                                axis=1, keepdims=True)
    for ci, h in units:
        rs = slice(ci * c, (ci + 1) * c)
        ksl = slice(h * GLA_DK, (h + 1) * GLA_DK)
        vsl = slice(h * GLA_DV, (h + 1) * GLA_DV)
        s_h = states[h]
        o_h = _dot(att_l[ci, h], v_l[ci, h]) + _dot(qs_all[rs, ksl].astype(BF16), s_h.astype(BF16))
        states[h] = dcol_l[ci, h] * s_h + kv_l[ci, h]
        o_h = o_h * lax.rsqrt(jnp.mean(o_h * o_h, axis=-1, keepdims=True) + EPS) * gng_ref[...]
        g_h = proj_scr[rs, OFF_G + h * GLA_DV:OFF_G + (h + 1) * GLA_DV]
        op_scr[rs, vsl] = (o_h * (g_h * jax.nn.sigmoid(g_h))).astype(BF16)
    for h in range(GLA_HEADS):
        s_scr[h] = states[h]

    u = proj_scr[:, OFF_U:OFF_U + POOL_WIDTH]
    ext_scr[HIST_ROWS:HIST_ROWS + tl, :] = u
    n_valid = start_pos + t * tl + lax.broadcasted_iota(I32, (tl, 1), 0) + 1
    for gi, w in enumerate(POOL_WINDOWS):
        lsl = slice(gi * POOL_GC, (gi + 1) * POOL_GC)
        win = ext_scr[HIST_ROWS:HIST_ROWS + tl, lsl]
        for s in range(1, w):
            win = win + ext_scr[HIST_ROWS - s:HIST_ROWS - s + tl, lsl]
        cnt = jnp.minimum(w, n_valid).astype(F32)
        r = win / cnt - u[:, lsl]
        p = _dot(r.astype(BF16), pw_ref[gi]) * ps_ref[:, lsl]
        op_scr[:, GLA_WIDTH + gi * POOL_GC:GLA_WIDTH + (gi + 1) * POOL_GC] = p.astype(BF16)
    tail = ext_scr[tl:tl + HIST_ROWS, :]
    ext_scr[0:HIST_ROWS, :] = tail

    mix = _dot(op_scr[...], wout_ref[...])
    x1_ref[0] = _ln(xn * ALPHA + mix, l1g_ref[...], l1b_ref[...])

    @pl.when(t == nt - 1)
    def _():
        sn_ref[0] = s_scr[...]
        hn_ref[0] = ext_scr[1:HIST_ROWS, :]


def _trunk_a(x, s0, h0, wts, *, start_pos):
    b, l, d = x.shape
    tl = min(l, 256)
    chunk = min(tl, CHUNK)
    assert l % tl == 0 and tl % chunk == 0 and chunk % SUB == 0 and l >= HIST_ROWS
    nt = l // tl
    kern = functools.partial(_trunk_a_kernel, tl=tl, chunk=chunk, start_pos=start_pos)
    const = lambda shape: pl.BlockSpec(shape, lambda bi, ti: (0,) * len(shape))
    pos = jnp.arange(tl)
    tri = ((pos[None, :] <= pos[:, None]) & (pos[None, :] // chunk == pos[:, None] // chunk)).astype(BF16)
    return pl.pallas_call(
        kern,
        grid=(b, nt),
        in_specs=[
            pl.BlockSpec((1, tl, d), lambda bi, ti: (bi, ti, 0)),
            pl.BlockSpec((1, GLA_HEADS, GLA_DK, GLA_DV), lambda bi, ti: (bi, 0, 0, 0)),
            pl.BlockSpec((1, POOL_HIST, POOL_WIDTH), lambda bi, ti: (bi, 0, 0)),
            const((1, d)), const((1, d)),
            const((d, D_IN_PAD)), const((GATE_PAD, GLA_KEY)), const((1, GLA_KEY)), const((1, GLA_DV)),
            const((len(POOL_WINDOWS), POOL_GC, POOL_GC)), const((1, POOL_WIDTH)),
            const((GLA_WIDTH + POOL_WIDTH, d)), const((1, d)), const((1, d)), const((tl, tl)),
        ],
        out_specs=[
            pl.BlockSpec((1, tl, d), lambda bi, ti: (bi, ti, 0)),
            pl.BlockSpec((1, GLA_HEADS, GLA_DK, GLA_DV), lambda bi, ti: (bi, 0, 0, 0)),
            pl.BlockSpec((1, POOL_HIST, POOL_WIDTH), lambda bi, ti: (bi, 0, 0)),
        ],
        out_shape=[
            jax.ShapeDtypeStruct((b, l, d), F32),
            jax.ShapeDtypeStruct((b, GLA_HEADS, GLA_DK, GLA_DV), F32),
            jax.ShapeDtypeStruct((b, POOL_HIST, POOL_WIDTH), F32),
        ],
        scratch_shapes=[
            pltpu.VMEM((tl, D_IN_PAD), F32),
            pltpu.VMEM((HIST_ROWS + tl, POOL_WIDTH), F32),
            pltpu.VMEM((GLA_HEADS, GLA_DK, GLA_DV), F32),
            pltpu.VMEM((tl, GLA_WIDTH + POOL_WIDTH), BF16),
        ],
        compiler_params=pltpu.CompilerParams(
            dimension_semantics=("arbitrary", "arbitrary"), vmem_limit_bytes=VMEM_LIMIT),
        name="trunk_a",
    )(x, s0, h0, *wts, tri)


def _prep_trunk_a_weights(ln_in_g, ln_in_b, w_in, w_gate_up, b_gate, gla_norm_g, pool_w, pool_scale, w_out,
                          ln1_g, ln1_b):
    d = D_MODEL
    p_gd = 2 * GLA_KEY + 2 * GLA_WIDTH
    w_in_r = jnp.concatenate(
        [w_in[:, :p_gd], w_in[:, p_gd + GATE_RANK:], w_in[:, p_gd:p_gd + GATE_RANK],
         jnp.zeros((d, GATE_PAD - GATE_RANK), w_in.dtype)], axis=1).astype(BF16)
    wgu = jnp.concatenate([w_gate_up, jnp.zeros((GATE_PAD - GATE_RANK, GLA_KEY), w_gate_up.dtype)],
                          axis=0).astype(BF16)
    return (ln_in_g.reshape(1, d), ln_in_b.reshape(1, d), w_in_r, wgu, b_gate.reshape(1, GLA_KEY),
            gla_norm_g.reshape(1, GLA_DV), pool_w.astype(BF16), pool_scale.reshape(1, POOL_WIDTH),
            w_out.astype(BF16), ln1_g.reshape(1, d), ln1_b.reshape(1, d))


def _mem_kv_kernel(m_ref, wk_ref, wv_ref, k_ref, v_ref, kb_ref, vb_ref):
    m = m_ref[...].astype(BF16)
    k = _dot(m, wk_ref[...])
    v = _dot(m, wv_ref[...])
    k_ref[...] = k
    v_ref[...] = v
    kb_ref[...] = k.astype(BF16)
    vb_ref[...] = v.astype(BF16)


def _mem_kv(mem, wk, wv):
    m, d = mem.shape
    tm = min(m, 512)
    assert m % tm == 0
    row = pl.BlockSpec((tm, d), lambda i: (i, 0))
    wspec = pl.BlockSpec((d, d), lambda i: (0, 0))
    return pl.pallas_call(
        _mem_kv_kernel,
        grid=(m // tm,),
        in_specs=[row, wspec, wspec],
        out_specs=[row, row, row, row],
        out_shape=[jax.ShapeDtypeStruct((m, d), F32), jax.ShapeDtypeStruct((m, d), F32),
                   jax.ShapeDtypeStruct((m, d), BF16), jax.ShapeDtypeStruct((m, d), BF16)],
        compiler_params=pltpu.CompilerParams(dimension_semantics=("arbitrary",), vmem_limit_bytes=VMEM_LIMIT),
        name="mem_kv",
    )(mem, wk, wv)


def _attn_kernel(x1_ref, k_ref, v_ref, wq_ref, wo_ref, g_ref, b_ref, x2_ref, o_scr):
    x1 = x1_ref[0]
    q = (_dot(x1.astype(BF16), wq_ref[...]) * (MEM_DH ** -0.5)).astype(BF16)
    heads = [slice(h * MEM_DH, (h + 1) * MEM_DH) for h in range(MEM_HEADS)]
    scores = [_dot_nt(q[:, hs], k_ref[0, :, hs]) for hs in heads]
    probs = []
    for s in scores:
        e = jnp.exp(s - jnp.max(s, axis=-1, keepdims=True))
        probs.append((e / jnp.sum(e, axis=-1, keepdims=True)).astype(BF16))
    for hs, p in zip(heads, probs):
        o_scr[:, hs] = _dot(p, v_ref[0, :, hs]).astype(BF16)
    attn = _dot(o_scr[...], wo_ref[...])
    x2_ref[...] = _ln(x1 * ALPHA + attn, g_ref[...], b_ref[...])


def _attn(x1, mem_k, mem_v, wq, wo, g, b):
    bsz, l, d = x1.shape
    tl = min(l, TOKEN_TILE)
    assert l % tl == 0
    nt = l // tl
    const = lambda shape: pl.BlockSpec(shape, lambda bi, ti: (0,) * len(shape))
    return pl.pallas_call(
        _attn_kernel,
        grid=(bsz, nt),
        in_specs=[
            pl.BlockSpec((1, tl, d), lambda bi, ti: (bi, ti, 0)),
            pl.BlockSpec((1, N_MEM, d), lambda bi, ti: (bi, 0, 0)),
            pl.BlockSpec((1, N_MEM, d), lambda bi, ti: (bi, 0, 0)),
            const((d, d)), const((d, d)), const((1, d)), const((1, d)),
        ],
        out_specs=pl.BlockSpec((tl, d), lambda bi, ti: (bi * nt + ti, 0)),
        out_shape=jax.ShapeDtypeStruct((bsz * l, d), F32),
        scratch_shapes=[pltpu.VMEM((tl, d), BF16)],
        compiler_params=pltpu.CompilerParams(
            dimension_semantics=("arbitrary", "arbitrary"), vmem_limit_bytes=VMEM_LIMIT),
        name="attn",
    )(x1, mem_k, mem_v, wq, wo, g, b)


def _two_part_specs(t, d, tiles_a):
    spec_a = pl.BlockSpec((t, d), lambda i, *_: (jnp.minimum(i, tiles_a - 1), 0))
    spec_b = pl.BlockSpec((t, d), lambda i, *_: (jnp.maximum(i - tiles_a, 0), 0))
    return spec_a, spec_b


def _router_kernel(xa_ref, xb_ref, wrt_ref, bias_ref, e_ref, g_ref, r_ref, cnt_ref, cnt_scr, *, tl, tiles_a):
    i = pl.program_id(0)

    @pl.when(i == 0)
    def _():
        cnt_scr[...] = jnp.zeros_like(cnt_scr)

    x = jnp.where(i < tiles_a, xa_ref[...], xb_ref[...])
    xh = x.astype(BF16)
    xm = (x - xh.astype(F32)).astype(BF16)
    wh = wrt_ref[0]
    wm = wrt_ref[1]
    logits = _dot_nt(wh, xh) + (_dot_nt(wh, xm) + _dot_nt(wm, xh))
    scores = jax.nn.sigmoid(logits)
    biased = scores + bias_ref[...]
    ninf = -jnp.inf
    eg = EXPERTS_PER_GROUP
    riota = lax.broadcasted_iota(I32, (eg, tl), 0)
    gs_rows = []
    for g in range(N_GROUPS):
        blk = biased[g * eg:(g + 1) * eg, :]
        m1 = jnp.max(blk, axis=0, keepdims=True)
        i1 = jnp.min(jnp.where(blk == m1, riota, eg), axis=0, keepdims=True)
        m2 = jnp.max(jnp.where(riota == i1, ninf, blk), axis=0, keepdims=True)
        gs_rows.append(m1 + m2)
    gs = jnp.concatenate(gs_rows, axis=0)
    giota = lax.broadcasted_iota(I32, (N_GROUPS, tl), 0)
    sel = jnp.zeros((N_GROUPS, tl), jnp.bool_)
    for _ in range(TOPK_GROUPS):
        m = jnp.max(gs, axis=0, keepdims=True)
        gi = jnp.min(jnp.where(gs == m, giota, N_GROUPS), axis=0, keepdims=True)
        hit = giota == gi
        sel = jnp.logical_or(sel, hit)
        gs = jnp.where(hit, ninf, gs)
    self = jnp.where(sel, 1.0, 0.0)
    masked = jnp.concatenate(
        [jnp.where(self[g:g + 1, :] > 0.5, biased[g * eg:(g + 1) * eg, :], ninf) for g in range(N_GROUPS)], axis=0)
    eiota = lax.broadcasted_iota(I32, (N_EXPERTS, tl), 0)
    idx_rows, sc_rows = [], []
    multi = jnp.zeros((N_EXPERTS, tl), F32)
    for _ in range(TOP_K):
        m = jnp.max(masked, axis=0, keepdims=True)
        idx = jnp.min(jnp.where(masked == m, eiota, N_EXPERTS), axis=0, keepdims=True)
        hit = eiota == idx
        sc_rows.append(jnp.sum(jnp.where(hit, scores, 0.0), axis=0, keepdims=True))
        idx_rows.append(idx)
        multi = jnp.where(hit, 1.0, multi)
        masked = jnp.where(hit, ninf, masked)
    top_e = jnp.concatenate(idx_rows, axis=0)
    sc = jnp.concatenate(sc_rows, axis=0)
    e_ref[...] = top_e
    g_ref[...] = sc / jnp.sum(sc, axis=0, keepdims=True) * ROUTED_SCALE
    mh = multi.astype(BF16)
    before = (lax.broadcasted_iota(I32, (tl, tl), 0) < lax.broadcasted_iota(I32, (tl, tl), 1)).astype(BF16)
    running = cnt_scr[...]
    rankmat = _dot(mh, before) + jnp.concatenate([running] * (tl // 128), axis=1)
    r_rows = [jnp.sum(jnp.where(eiota == idx_rows[k], rankmat, 0.0), axis=0, keepdims=True) for k in range(TOP_K)]
    r_ref[...] = jnp.concatenate(r_rows, axis=0).astype(I32)
    total = running + _dot(mh, jnp.ones((tl, 128), BF16))
    cnt_scr[...] = total
    cnt_ref[...] = total


def _router(x2a, x2b, wrt, bias_col):
    d = x2a.shape[1]
    tl = TOKEN_TILE
    assert x2a.shape[0] % tl == 0 and x2b.shape[0] % tl == 0
    tiles_a = x2a.shape[0] // tl
    n = x2a.shape[0] + x2b.shape[0]
    kspec = pl.BlockSpec((TOP_K, tl), lambda i: (0, i))
    return pl.pallas_call(
        functools.partial(_router_kernel, tl=tl, tiles_a=tiles_a),
        grid=(n // tl,),
        in_specs=[*_two_part_specs(tl, d, tiles_a),
                  pl.BlockSpec((2, N_EXPERTS, d), lambda i: (0, 0, 0)),
                  pl.BlockSpec((N_EXPERTS, 1), lambda i: (0, 0))],
        out_specs=[kspec, kspec, kspec, pl.BlockSpec((N_EXPERTS, 128), lambda i: (0, 0))],
        out_shape=[jax.ShapeDtypeStruct((TOP_K, n), I32), jax.ShapeDtypeStruct((TOP_K, n), F32),
                   jax.ShapeDtypeStruct((TOP_K, n), I32), jax.ShapeDtypeStruct((N_EXPERTS, 128), F32)],
        scratch_shapes=[pltpu.VMEM((N_EXPERTS, 128), F32)],
        compiler_params=pltpu.CompilerParams(dimension_semantics=("arbitrary",), vmem_limit_bytes=VMEM_LIMIT),
        name="router",
    )(x2a, x2b, wrt, bias_col)


def _rows_to_slots(x):
    blocks = jnp.stack([x[:, s * LANES:(s + 1) * LANES] for s in range(SUBLANES)], axis=0)
    return pltpu.einshape("s(gt)l->(gt)sl", blocks, t=SUBLANES)


def _slots_to_rows(v):
    blocks = pltpu.einshape("(gt)sl->s(gt)l", v, t=SUBLANES)
    return jnp.concatenate([blocks[s] for s in range(SUBLANES)], axis=1)


def _dispatch_kernel(fs_ref, fl_ref, nu_ref, dest_ref, xa_ref, xb_ref, xs_ref, stage, zbuf, sem, zsem,
                     *, t, tm, tiles_a, n_blk):
    i = pl.program_id(0)
    nsteps = pl.num_programs(0)

    def zero_fill(start):
        def go(cp):
            if start:
                cp.start()
            else:
                cp.wait()

        def per_expert(e, carry):
            ln = fl_ref[e]
            off = fs_ref[e]
            bit = tm // 2
            while bit:
                @pl.when((ln & bit) != 0)
                def _(off=off, bit=bit):
                    go(pltpu.make_async_copy(zbuf.at[pl.ds(0, bit)], xs_ref.at[pl.ds(off, bit)], zsem))
                off = off + (ln & bit)
                bit //= 2
            return carry

        lax.fori_loop(0, N_EXPERTS, per_expert, 0)

        def per_block(b, carry):
            go(pltpu.make_async_copy(zbuf, xs_ref.at[pl.ds(b * tm, tm)], zsem))
            return carry

        lax.fori_loop(nu_ref[0], n_blk, per_block, 0)

    @pl.when(i == 0)
    def _():
        zbuf[...] = jnp.zeros_like(zbuf)
        zero_fill(True)

    def drain(slot):
        for k in range(TOP_K):
            pltpu.make_async_copy(stage.at[slot], xs_ref.at[pl.ds(0, t)], sem.at[slot]).wait()

    def step(slot):
        @pl.when(i >= 2)
        def _():
            drain(slot)

        stage[slot] = _rows_to_slots(jnp.where(i < tiles_a, xa_ref[...], xb_ref[...])).astype(BF16)

        def body(n, carry):
            src = stage.at[slot, n]
            for k in range(TOP_K):
                dst = xs_ref.at[dest_ref[n * TOP_K + k]]
                pltpu.make_async_copy(src, dst, sem.at[slot]).start(priority=k % 2)
            return carry

        lax.fori_loop(0, t, body, 0)

    for slot in range(2):
        @pl.when(i % 2 == slot)
        def _(slot=slot):
            step(slot)

    @pl.when(i == 0)
    def _():
        zero_fill(False)

    @pl.when(i == nsteps - 1)
    def _():
        for slot in range(2):
            @pl.when(jnp.logical_or(nsteps >= 2, (nsteps - 1) % 2 == slot))
            def _(slot=slot):
                drain(slot)


def _dispatch(fill_start, fill_len, nu, dest, x2a, x2b, *, n_blk, tm):
    d = x2a.shape[1]
    t = TOKEN_TILE
    assert x2a.shape[0] % t == 0 and x2b.shape[0] % t == 0
    tiles_a = x2a.shape[0] // t
    n = x2a.shape[0] + x2b.shape[0]
    return pl.pallas_call(
        functools.partial(_dispatch_kernel, t=t, tm=tm, tiles_a=tiles_a, n_blk=n_blk),
        grid_spec=pltpu.PrefetchScalarGridSpec(
            num_scalar_prefetch=3,
            grid=(n // t,),
            in_specs=[pl.BlockSpec((t * TOP_K,), lambda i, *_: (i,), memory_space=pltpu.SMEM),
                      *_two_part_specs(t, d, tiles_a)],
            out_specs=pl.BlockSpec(memory_space=pl.ANY),
            scratch_shapes=[pltpu.VMEM((2, t, SUBLANES, LANES), BF16), pltpu.VMEM((tm, SUBLANES, LANES), BF16),
                            pltpu.SemaphoreType.DMA((2,)), pltpu.SemaphoreType.DMA(())],
        ),
        out_shape=jax.ShapeDtypeStruct((n_blk * tm, SUBLANES, LANES), BF16),
        compiler_params=pltpu.CompilerParams(dimension_semantics=("arbitrary",), vmem_limit_bytes=VMEM_LIMIT),
        name="dispatch",
    )(fill_start, fill_len, nu, dest, x2a, x2b)


def _experts_kernel(b0_ref, nb_ref, cnt_ref, nu_ref, w1_ref, w3_ref, w2_ref, xs_ref, ys_ref,
                    w1b, w3b, w2b, xbuf, ybuf, in_sem, out_sem, *, tm, n_blk):
    e = pl.program_id(0)
    n_used = nu_ref[0]
    blk_rows = tm * SUBLANES

    def hbm_block(ref, b):
        return ref.at[pl.ds(pl.multiple_of(b * blk_rows, blk_rows), blk_rows)]

    def x_copy(b):
        s = lax.rem(b, X_DEPTH)
        return pltpu.make_async_copy(hbm_block(xs_ref, b), xbuf.at[s], in_sem.at[s])

    def y_copy(b):
        return pltpu.make_async_copy(ybuf.at[b % 2], hbm_block(ys_ref, b), out_sem.at[b % 2])

    @pl.when(e == 0)
    def _():
        for b in range(X_DEPTH - 1):
            @pl.when(b < n_used)
            def _(b=b):
                x_copy(b).start()

    w1b[...] = w1_ref[0].astype(BF16)
    w3b[...] = w3_ref[0].astype(BF16)
    w2b[...] = w2_ref[0].astype(BF16)
    b0 = b0_ref[e]
    cnt = cnt_ref[e]

    def block(j, carry):
        b = b0 + j
        x_copy(b).wait()

        @pl.when(b + X_DEPTH - 1 < n_used)
        def _():
            x_copy(b + X_DEPTH - 1).start()

        @pl.when(b >= 2)
        def _():
            y_copy(b - 2).wait()

        rows = tm // EXPERT_SPLIT
        starts = range(0, tm, rows)
        xs_l, a_l, g_l = [], [], []
        for r0 in starts:
            live = lax.broadcasted_iota(I32, (rows, 1), 0) < cnt - j * tm - r0
            part = xbuf[lax.rem(b, X_DEPTH), pl.ds(r0 * SUBLANES, rows * SUBLANES), :].astype(F32)
            x = _slots_to_rows(part.reshape(rows, SUBLANES, LANES))
            xs_l.append(jnp.where(live, x, 0.0).astype(BF16))
        for x in xs_l:
            a_l.append(_dot(x, w1b[...]))
            g_l.append(_dot(x, w3b[...]))
        h_l = [((a * jax.nn.sigmoid(a)) * g).astype(BF16) for a, g in zip(a_l, g_l)]
        for r0, h in zip(starts, h_l):
            y = _rows_to_slots(_dot(h, w2b[...])).reshape(rows * SUBLANES, LANES)
            ybuf[b % 2, pl.ds(r0 * SUBLANES, rows * SUBLANES), :] = y.astype(BF16)
        y_copy(b).start()
        return carry

    lax.fori_loop(0, nb_ref[e], block, 0)

    @pl.when(e == pl.num_programs(0) - 1)
    def _():
        @pl.when(n_used >= 2)
        def _():
            y_copy(n_used - 2).wait()

        y_copy(n_used - 1).wait()
        ybuf[0] = jnp.zeros(ybuf.shape[1:], BF16)

        def zero_copy(b):
            return pltpu.make_async_copy(ybuf.at[0], hbm_block(ys_ref, b), out_sem.at[0])

        def zero(b, carry):
            zero_copy(b).start()
            return carry

        lax.fori_loop(n_used, n_blk, zero, 0)

        def zero_wait(b, carry):
            zero_copy(b).wait()
            return carry

        lax.fori_loop(n_used, n_blk, zero_wait, 0)


def _experts(blk0, nblk, counts, nu, xs, w1, w3, w2, *, tm):
    n_slot = xs.shape[0]
    n_blk = n_slot // tm
    n_exp, d, de = w1.shape
    wmap = lambda e, *_: (e, 0, 0)
    ys = pl.pallas_call(
        functools.partial(_experts_kernel, tm=tm, n_blk=n_blk),
        grid_spec=pltpu.PrefetchScalarGridSpec(
            num_scalar_prefetch=4,
            grid=(n_exp,),
            in_specs=[pl.BlockSpec((1, d, de), wmap), pl.BlockSpec((1, d, de), wmap), pl.BlockSpec((1, de, d), wmap),
                      pl.BlockSpec(memory_space=pl.ANY)],
            out_specs=pl.BlockSpec(memory_space=pl.ANY),
            scratch_shapes=[pltpu.VMEM((d, de), BF16), pltpu.VMEM((d, de), BF16), pltpu.VMEM((de, d), BF16),
                            pltpu.VMEM((X_DEPTH, tm * SUBLANES, LANES), BF16), pltpu.VMEM((2, tm * SUBLANES, LANES), BF16),
                            pltpu.SemaphoreType.DMA((X_DEPTH,)), pltpu.SemaphoreType.DMA((2,))],
        ),
        out_shape=jax.ShapeDtypeStruct((n_slot * SUBLANES, LANES), BF16),
        compiler_params=pltpu.CompilerParams(dimension_semantics=("arbitrary",), vmem_limit_bytes=VMEM_LIMIT),
        name="experts",
    )(blk0, nblk, counts, nu, w1, w3, w2, xs.reshape(n_slot * SUBLANES, LANES))
    return ys.reshape(n_slot, SUBLANES, LANES)


def _combine_kernel(dcur_ref, dnxt_ref, x2_ref, g_ref, w1_ref, w3_ref, w2_ref, lg_ref, lb_ref, ys_ref,
                    out_ref, buf, acc_scr, sem, *, t, nsteps):
    i = pl.program_id(0)

    def fetch(d_ref, s, n):
        for k in range(TOP_K):
            pltpu.make_async_copy(ys_ref.at[d_ref[n * TOP_K + k]], buf.at[s, k, n],
                                  sem.at[s]).start(priority=k % 2)

    def drain(s):
        for k in range(TOP_K):
            pltpu.make_async_copy(ys_ref.at[pl.ds(0, t)], buf.at[s, k], sem.at[s]).wait()

    @pl.when(i == 0)
    def _():
        def first(n, carry):
            fetch(dcur_ref, 0, n)
            return carry

        lax.fori_loop(0, t, first, 0)

    def step(s):
        drain(s)

        def group(gi, carry):
            r0 = pl.multiple_of(gi * SUBLANES, SUBLANES)
            for j in range(SUBLANES):
                fetch(dnxt_ref, 1 - s, r0 + j)
            gates = g_ref[pl.ds(r0, SUBLANES), :]
            gk = [jnp.broadcast_to(gates[:, k:k + 1], (SUBLANES, LANES)) for k in range(TOP_K)]
            acc = [None] * SUBLANES
            for k in range(TOP_K):
                rows8 = buf[s, k, pl.ds(r0, SUBLANES)].astype(F32)
                blocks = pltpu.einshape("(gt)sl->s(gt)l", rows8, t=SUBLANES)
                for lb in range(SUBLANES):
                    term = gk[k] * blocks[lb]
                    acc[lb] = term if acc[lb] is None else acc[lb] + term
            for lb in range(SUBLANES):
                acc_scr[pl.ds(r0, SUBLANES), lb * LANES:(lb + 1) * LANES] = acc[lb]
            return carry

        lax.fori_loop(0, t // SUBLANES, group, 0)

    for s in range(2):
        @pl.when(i % 2 == s)
        def _(s=s):
            step(s)

    @pl.when(i == nsteps - 1)
    def _():
        drain(nsteps % 2)

    x2 = x2_ref[...]
    acc = acc_scr[...]
    xb = x2.astype(BF16)
    a = _dot(xb, w1_ref[...])
    hs = (a * jax.nn.sigmoid(a)) * _dot(xb, w3_ref[...])
    shared = _dot(hs.astype(BF16), w2_ref[...])
    out_ref[...] = _ln(x2 * ALPHA + (acc + shared), lg_ref[...], lb_ref[...])


def _combine(dest, x2, gate_t, w1s, w3s, w2s, lg, lb, ys, *, row_off):
    n_rows, d = x2.shape
    n = dest.shape[0] // TOP_K
    t = TOKEN_TILE
    assert row_off % t == 0 and n_rows % t == 0 and n % t == 0
    off = row_off // t
    last = n // t - 1
    de = w1s.shape[-1]
    const = lambda shape: pl.BlockSpec(shape, lambda i: (0,) * len(shape))
    return pl.pallas_call(
        functools.partial(_combine_kernel, t=t, nsteps=n_rows // t),
        grid=(n_rows // t,),
        in_specs=[pl.BlockSpec((t * TOP_K,), lambda i: (off + i,), memory_space=pltpu.SMEM),
                  pl.BlockSpec((t * TOP_K,), lambda i: (jnp.minimum(off + i + 1, last),), memory_space=pltpu.SMEM),
                  pl.BlockSpec((t, d), lambda i: (i, 0)),
                  pl.BlockSpec((t, TOP_K), lambda i: (off + i, 0)),
                  const((d, de)), const((d, de)), const((de, d)), const((1, d)), const((1, d)),
                  pl.BlockSpec(memory_space=pl.ANY)],
        out_specs=pl.BlockSpec((t, d), lambda i: (i, 0)),
        out_shape=jax.ShapeDtypeStruct((n_rows, d), F32),
        scratch_shapes=[pltpu.VMEM((2, TOP_K, t, SUBLANES, LANES), BF16), pltpu.VMEM((t, d), F32),
                        pltpu.SemaphoreType.DMA((2,))],
        compiler_params=pltpu.CompilerParams(dimension_semantics=("arbitrary",), vmem_limit_bytes=VMEM_LIMIT),
        name="combine",
    )(dest, dest, x2, gate_t, w1s, w3s, w2s, lg, lb, ys)


EXPERT_TM = 512
EXPERT_SPLIT = 2
X_DEPTH = 4


def _moe_plan(cnt):
    tm = EXPERT_TM
    counts = cnt[:, 0].astype(I32)
    padded = (counts + tm - 1) // tm * tm
    pad_end = jnp.cumsum(padded)
    pad_start = pad_end - padded
    n_used = pad_end[-1] // tm
    fill_start = (pad_start + counts).astype(I32)
    fill_len = (padded - counts).astype(I32)
    return (pad_start.astype(I32), (pad_start // tm).astype(I32), (padded // tm).astype(I32), counts,
            n_used.reshape(1).astype(I32), fill_start, fill_len)


def _slots_kernel(ps_ref, e_ref, r_ref, d_ref):
    e = e_ref[...]

    def body(j, base):
        return jnp.where(e == j, ps_ref[j], base)

    base = lax.fori_loop(0, N_EXPERTS, body, jnp.zeros_like(e), unroll=8)
    d_ref[...] = base + r_ref[...]


def _slots(pad_start, top_e, rank):
    k, n = top_e.shape
    tl = next(c for c in (2048, 1536, 1024, 512, 256, 128) if n % c == 0)
    spec = pl.BlockSpec((k, tl), lambda i, ps: (0, i))
    return pl.pallas_call(
        _slots_kernel,
        grid_spec=pltpu.PrefetchScalarGridSpec(num_scalar_prefetch=1, grid=(n // tl,), in_specs=[spec, spec],
                                               out_specs=spec),
        out_shape=jax.ShapeDtypeStruct((k, n), I32),
        compiler_params=pltpu.CompilerParams(dimension_semantics=("arbitrary",)),
        name="slots",
    )(pad_start, top_e, rank)


PAST_LEN = 1024


def kernel(x_prompt, x_sample, state_gla, cache_pool, cache_mem_k, cache_mem_v, mem_prompt, ln_in_g, ln_in_b, w_in, w_gate_up, b_gate, gla_norm_g, pool_w, pool_scale, w_out, ln1_g, ln1_b, wq_mem, wk_mem, wv_mem, wo_mem, ln2_g, ln2_b, w_router, router_bias, w1_exp, w3_exp, w2_exp, w1_sh, w3_sh, w2_sh, ln3_g, ln3_b):
    assert w_in.shape[0] == 1, "single-layer trunk"
    bp, lp, d = x_prompt.shape
    bs, ls, _ = x_sample.shape
    n_p, n_s = bp * lp, bs * ls
    n_all = n_p + n_s

    wts_a = _prep_trunk_a_weights(ln_in_g, ln_in_b, w_in[0], w_gate_up[0], b_gate[0], gla_norm_g[0], pool_w[0],
                                  pool_scale[0], w_out[0], ln1_g[0], ln1_b[0])
    s0 = jnp.zeros((bp, GLA_HEADS, GLA_DK, GLA_DV), F32)
    h0 = jnp.zeros((bp, POOL_HIST, POOL_WIDTH), F32)
    x1p, sp, hp = _trunk_a(x_prompt, s0, h0, wts_a, start_pos=0)
    x1s, ss, hs = _trunk_a(x_sample, state_gla[0], cache_pool[0], wts_a, start_pos=PAST_LEN)

    mk, mv, mkb, mvb = _mem_kv(mem_prompt.reshape(bp * N_MEM, d), wk_mem[0].astype(BF16), wv_mem[0].astype(BF16))
    wq, wo = wq_mem[0].astype(BF16), wo_mem[0].astype(BF16)
    g2, b2 = ln2_g[0].reshape(1, d), ln2_b[0].reshape(1, d)
    x2p = _attn(x1p, mkb.reshape(bp, N_MEM, d), mvb.reshape(bp, N_MEM, d), wq, wo, g2, b2)
    x2s = _attn(x1s, cache_mem_k[0].reshape(bs, N_MEM, d).astype(BF16),
                cache_mem_v[0].reshape(bs, N_MEM, d).astype(BF16), wq, wo, g2, b2)

    wrt = w_router[0].T
    wrt_h = wrt.astype(BF16)
    wrt_m = (wrt - wrt_h.astype(F32)).astype(BF16)
    top_e, gate, rank, cnt = _router(x2p, x2s, jnp.stack([wrt_h, wrt_m]), router_bias[0].reshape(N_EXPERTS, 1))

    assert (n_all * TOP_K) % EXPERT_TM == 0
    n_blk = n_all * TOP_K // EXPERT_TM + N_EXPERTS
    pad_start, blk0, nblk, counts, nu, fill_start, fill_len = _moe_plan(cnt)
    dest = _slots(pad_start, top_e, rank).T.reshape(-1)
    xs = _dispatch(fill_start, fill_len, nu, dest, x2p, x2s, n_blk=n_blk, tm=EXPERT_TM)
    ys = _experts(blk0, nblk, counts, nu, xs, w1_exp[0], w3_exp[0], w2_exp[0], tm=EXPERT_TM)
    sh = (w1_sh[0].astype(BF16), w3_sh[0].astype(BF16), w2_sh[0].astype(BF16),
          ln3_g[0].reshape(1, d), ln3_b[0].reshape(1, d))
    gate_t = gate.T
    yp = _combine(dest, x2p, gate_t, *sh, ys, row_off=0)
    ysm = _combine(dest, x2s, gate_t, *sh, ys, row_off=n_p)

    return (yp.reshape(bp, lp, d), ysm.reshape(bs, ls, d), sp[None], hp[None],
            mk.reshape(1, bp, N_MEM, MEM_HEADS, MEM_DH), mv.reshape(1, bp, N_MEM, MEM_HEADS, MEM_DH),
            ss[None], hs[None])
```

```python
import functools

import jax
import jax.numpy as jnp
from jax import lax
from jax.experimental import pallas as pl
from jax.experimental.pallas import tpu as pltpu

F32 = jnp.float32
BF16 = jnp.bfloat16
I32 = jnp.int32

D_MODEL = 1024
CHUNK = 64
SUB = 16
GLA_HEADS = 4
GLA_DK = 64
GLA_DV = 128
GLA_KEY = GLA_HEADS * GLA_DK
GLA_WIDTH = GLA_HEADS * GLA_DV
GATE_RANK = 16
GATE_PAD = 128
POOL_WIDTH = 512
POOL_WINDOWS = (2, 4, 8, 16)
POOL_GC = 128
POOL_HIST = 15
HIST_ROWS = 16
N_MEM = 256
MEM_HEADS = 4
MEM_DH = 256
N_EXPERTS = 256
N_GROUPS = 8
EXPERTS_PER_GROUP = 32
TOPK_GROUPS = 4
TOP_K = 8
ROUTED_SCALE = 2.5
TOKEN_TILE = 512
SUBLANES, LANES = 8, 128
assert D_MODEL == SUBLANES * LANES
ALPHA = 2.0 ** 0.25
EPS = 1e-5
OFF_Q, OFF_K, OFF_V, OFF_G, OFF_U, OFF_GD = 0, 256, 512, 1024, 1536, 2048
D_IN_PAD = OFF_GD + GATE_PAD

VMEM_LIMIT = 56 * 1024 * 1024


def _ln(x, g, b):
    mu = jnp.mean(x, axis=-1, keepdims=True)
    xc = x - mu
    var = jnp.mean(xc * xc, axis=-1, keepdims=True)
    return xc * lax.rsqrt(var + EPS) * g + b


def _dot(a, b):
    return jnp.dot(a, b, preferred_element_type=F32)


def _dot_nt(a, b):
    return lax.dot_general(a, b, (((1,), (1,)), ((), ())), preferred_element_type=F32)


def _dot_tn(a, b):
    return lax.dot_general(a, b, (((0,), (0,)), ((), ())), preferred_element_type=F32)


def _split3(x):
    h = x.astype(BF16)
    r = x - h.astype(F32)
    m = r.astype(BF16)
    l = (r - m.astype(F32)).astype(BF16)
    return h, m, l


def _trunk_a_kernel(x_ref, s0_ref, h0_ref, lng_ref, lnb_ref, win_ref, wgu_ref, bg_ref, gng_ref,
                    pw_ref, ps_ref, wout_ref, l1g_ref, l1b_ref, tri_ref,
                    x1_ref, sn_ref, hn_ref,
                    proj_scr, ext_scr, s_scr, op_scr, *, tl, chunk, start_pos):
    t = pl.program_id(1)
    nt = pl.num_programs(1)

    @pl.when(t == 0)
    def _():
        s_scr[...] = s0_ref[0]
        ext_scr[0:1, :] = jnp.zeros((1, POOL_WIDTH), F32)
        ext_scr[1:HIST_ROWS, :] = h0_ref[0]

    xn = _ln(x_ref[0], lng_ref[...], lnb_ref[...])
    proj_scr[...] = _dot(xn.astype(BF16), win_ref[...])

    c = chunk
    shift = lambda a, n: lax.shift_right_logical(a, n.bit_length() - 1)
    gd = proj_scr[:, OFF_GD:OFF_GD + GATE_PAD]
    z = _dot(gd.astype(BF16), wgu_ref[...]) + bg_ref[...]
    lf = (jnp.minimum(z, 0.0) - jnp.log(1.0 + jnp.exp(-jnp.abs(z)))) * (1.0 / 16.0)
    tri = tri_ref[...]
    lh, lm, ll = _split3(lf)
    cum = _dot(tri, lh) + _dot(tri, lm) + _dot(tri, ll)
    q_all = proj_scr[:, OFF_Q:OFF_Q + GLA_KEY] * (GLA_DK ** -0.5)
    k_all = proj_scr[:, OFF_K:OFF_K + GLA_KEY]
    qs_all = q_all * jnp.exp(cum)

    causal = lax.broadcasted_iota(I32, (c, c), 1) <= lax.broadcasted_iota(I32, (c, c), 0)
    eye_dk = lax.broadcasted_iota(I32, (GLA_DK, GLA_DK), 0) == lax.broadcasted_iota(I32, (GLA_DK, GLA_DK), 1)
    n_sub = c // SUB
    rblk = shift(lax.broadcasted_iota(I32, (c, n_sub * GLA_DK), 0), SUB)
    lblk = shift(lax.broadcasted_iota(I32, (c, n_sub * GLA_DK), 1), GLA_DK)
    lblk_row = shift(lax.broadcasted_iota(I32, (1, n_sub * GLA_DK), 1), GLA_DK)
    mask_q = rblk == lblk
    mask_k = rblk <= lblk

    def tile_lanes(a):
        return jnp.concatenate([a] * n_sub, axis=1)

    states = [s_scr[h] for h in range(GLA_HEADS)]
    units = [(ci, h) for ci in range(tl // c) for h in range(GLA_HEADS)]
    att_l, kv_l, dcol_l, v_l = {}, {}, {}, {}
    for ci, h in units:
        rs = slice(ci * c, (ci + 1) * c)
        ksl = slice(h * GLA_DK, (h + 1) * GLA_DK)
        v_l[ci, h] = proj_scr[rs, OFF_V + h * GLA_DV:OFF_V + (h + 1) * GLA_DV].astype(BF16)
        cum_t = tile_lanes(cum[rs, ksl])
        q_t = tile_lanes(q_all[rs, ksl])
        k_t = tile_lanes(k_all[rs, ksl])
        ref_row = jnp.zeros((1, n_sub * GLA_DK), F32)
        for i in range(1, n_sub):
            ref_row = jnp.where(lblk_row == i, cum_t[i * SUB - 1:i * SUB, :], ref_row)
        arg = cum_t - ref_row
        lhs = jnp.where(mask_q, q_t * jnp.exp(jnp.where(mask_q, arg, 0.0)), 0.0)
        rhs = jnp.where(mask_k, k_t * jnp.exp(jnp.where(mask_k, -arg, 0.0)), 0.0)
        att_l[ci, h] = jnp.where(causal, _dot_nt(lhs.astype(BF16), rhs.astype(BF16)), 0.0).astype(BF16)
    for ci, h in units:
        rs = slice(ci * c, (ci + 1) * c)
        ksl = slice(h * GLA_DK, (h + 1) * GLA_DK)
        last = cum[(ci + 1) * c - 1:(ci + 1) * c, ksl]
        ks_h = k_all[rs, ksl] * jnp.exp(last - cum[rs, ksl])
        kv_l[ci, h] = _dot_tn(ks_h.astype(BF16), v_l[ci, h])
        dcol_l[ci, h] = jnp.sum(jnp.where(eye_dk, jnp.broadcast_to(jnp.exp(last), (GLA_DK, GLA_DK)), 0.0),
                                axis=1, keepdims=True)
    for ci, h in units:
        rs = slice(ci * c, (ci + 1) * c)
        ksl = slice(h * GLA_DK, (h + 1) * GLA_DK)
        vsl = slice(h * GLA_DV, (h + 1) * GLA_DV)
        s_h = states[h]
        o_h = _dot(att_l[ci, h], v_l[ci, h]) + _dot(qs_all[rs, ksl].astype(BF16), s_h.astype(BF16))
        states[h] = dcol_l[ci, h] * s_h + kv_l[ci, h]
        o_h = o_h * lax.rsqrt(jnp.mean(o_h * o_h, axis=-1, keepdims=True) + EPS) * gng_ref[...]
        g_h = proj_scr[rs, OFF_G + h * GLA_DV:OFF_G + (h + 1) * GLA_DV]
        op_scr[rs, vsl] = (o_h * (g_h * jax.nn.sigmoid(g_h))).astype(BF16)
    for h in range(GLA_HEADS):
        s_scr[h] = states[h]

    u = proj_scr[:, OFF_U:OFF_U + POOL_WIDTH]
    ext_scr[HIST_ROWS:HIST_ROWS + tl, :] = u
    n_valid = start_pos + t * tl + lax.broadcasted_iota(I32, (tl, 1), 0) + 1
    for gi, w in enumerate(POOL_WINDOWS):
        lsl = slice(gi * POOL_GC, (gi + 1) * POOL_GC)
        win = ext_scr[HIST_ROWS:HIST_ROWS + tl, lsl]
        for s in range(1, w):
            win = win + ext_scr[HIST_ROWS - s:HIST_ROWS - s + tl, lsl]
        cnt = jnp.minimum(w, n_valid).astype(F32)
        r = win / cnt - u[:, lsl]
        p = _dot(r.astype(BF16), pw_ref[gi]) * ps_ref[:, lsl]
        op_scr[:, GLA_WIDTH + gi * POOL_GC:GLA_WIDTH + (gi + 1) * POOL_GC] = p.astype(BF16)
    tail = ext_scr[tl:tl + HIST_ROWS, :]
    ext_scr[0:HIST_ROWS, :] = tail

    mix = _dot(op_scr[...], wout_ref[...])
    x1_ref[0] = _ln(xn * ALPHA + mix, l1g_ref[...], l1b_ref[...])

    @pl.when(t == nt - 1)
    def _():
        sn_ref[0] = s_scr[...]
        hn_ref[0] = ext_scr[1:HIST_ROWS, :]


def _trunk_a(x, s0, h0, wts, *, start_pos):
    b, l, d = x.shape
    tl = min(l, 256)
    chunk = min(tl, CHUNK)
    assert l % tl == 0 and tl % chunk == 0 and chunk % SUB == 0 and l >= HIST_ROWS
    nt = l // tl
    kern = functools.partial(_trunk_a_kernel, tl=tl, chunk=chunk, start_pos=start_pos)
    const = lambda shape: pl.BlockSpec(shape, lambda bi, ti: (0,) * len(shape))
    pos = jnp.arange(tl)
    tri = ((pos[None, :] <= pos[:, None]) & (pos[None, :] // chunk == pos[:, None] // chunk)).astype(BF16)
    return pl.pallas_call(
        kern,
        grid=(b, nt),
        in_specs=[
            pl.BlockSpec((1, tl, d), lambda bi, ti: (bi, ti, 0)),
            pl.BlockSpec((1, GLA_HEADS, GLA_DK, GLA_DV), lambda bi, ti: (bi, 0, 0, 0)),
            pl.BlockSpec((1, POOL_HIST, POOL_WIDTH), lambda bi, ti: (bi, 0, 0)),
            const((1, d)), const((1, d)),
            const((d, D_IN_PAD)), const((GATE_PAD, GLA_KEY)), const((1, GLA_KEY)), const((1, GLA_DV)),
            const((len(POOL_WINDOWS), POOL_GC, POOL_GC)), const((1, POOL_WIDTH)),
            const((GLA_WIDTH + POOL_WIDTH, d)), const((1, d)), const((1, d)), const((tl, tl)),
        ],
        out_specs=[
            pl.BlockSpec((1, tl, d), lambda bi, ti: (bi, ti, 0)),
            pl.BlockSpec((1, GLA_HEADS, GLA_DK, GLA_DV), lambda bi, ti: (bi, 0, 0, 0)),
            pl.BlockSpec((1, POOL_HIST, POOL_WIDTH), lambda bi, ti: (bi, 0, 0)),
        ],
        out_shape=[
            jax.ShapeDtypeStruct((b, l, d), F32),
            jax.ShapeDtypeStruct((b, GLA_HEADS, GLA_DK, GLA_DV), F32),
            jax.ShapeDtypeStruct((b, POOL_HIST, POOL_WIDTH), F32),
        ],
        scratch_shapes=[
            pltpu.VMEM((tl, D_IN_PAD), F32),
            pltpu.VMEM((HIST_ROWS + tl, POOL_WIDTH), F32),
            pltpu.VMEM((GLA_HEADS, GLA_DK, GLA_DV), F32),
            pltpu.VMEM((tl, GLA_WIDTH + POOL_WIDTH), BF16),
        ],
        compiler_params=pltpu.CompilerParams(
            dimension_semantics=("arbitrary", "arbitrary"), vmem_limit_bytes=VMEM_LIMIT),
        name="trunk_a",
    )(x, s0, h0, *wts, tri)


def _prep_trunk_a_weights(ln_in_g, ln_in_b, w_in, w_gate_up, b_gate, gla_norm_g, pool_w, pool_scale, w_out,
                          ln1_g, ln1_b):
    d = D_MODEL
    p_gd = 2 * GLA_KEY + 2 * GLA_WIDTH
    w_in_r = jnp.concatenate(
        [w_in[:, :p_gd], w_in[:, p_gd + GATE_RANK:], w_in[:, p_gd:p_gd + GATE_RANK],
         jnp.zeros((d, GATE_PAD - GATE_RANK), w_in.dtype)], axis=1).astype(BF16)
    wgu = jnp.concatenate([w_gate_up, jnp.zeros((GATE_PAD - GATE_RANK, GLA_KEY), w_gate_up.dtype)],
                          axis=0).astype(BF16)
    return (ln_in_g.reshape(1, d), ln_in_b.reshape(1, d), w_in_r, wgu, b_gate.reshape(1, GLA_KEY),
            gla_norm_g.reshape(1, GLA_DV), pool_w.astype(BF16), pool_scale.reshape(1, POOL_WIDTH),
            w_out.astype(BF16), ln1_g.reshape(1, d), ln1_b.reshape(1, d))


def _mem_kv_kernel(m_ref, wk_ref, wv_ref, k_ref, v_ref, kb_ref, vb_ref):
    m = m_ref[...].astype(BF16)
    k = _dot(m, wk_ref[...])
    v = _dot(m, wv_ref[...])
    k_ref[...] = k
    v_ref[...] = v
    kb_ref[...] = k.astype(BF16)
    vb_ref[...] = v.astype(BF16)


def _mem_kv(mem, wk, wv):
    m, d = mem.shape
    tm = min(m, 512)
    assert m % tm == 0
    row = pl.BlockSpec((tm, d), lambda i: (i, 0))
    wspec = pl.BlockSpec((d, d), lambda i: (0, 0))
    return pl.pallas_call(
        _mem_kv_kernel,
        grid=(m // tm,),
        in_specs=[row, wspec, wspec],
        out_specs=[row, row, row, row],
        out_shape=[jax.ShapeDtypeStruct((m, d), F32), jax.ShapeDtypeStruct((m, d), F32),
                   jax.ShapeDtypeStruct((m, d), BF16), jax.ShapeDtypeStruct((m, d), BF16)],
        compiler_params=pltpu.CompilerParams(dimension_semantics=("arbitrary",), vmem_limit_bytes=VMEM_LIMIT),
        name="mem_kv",
    )(mem, wk, wv)


def _attn_kernel(x1_ref, k_ref, v_ref, wq_ref, wo_ref, g_ref, b_ref, x2_ref, o_scr):
    x1 = x1_ref[0]
    q = (_dot(x1.astype(BF16), wq_ref[...]) * (MEM_DH ** -0.5)).astype(BF16)
    heads = [slice(h * MEM_DH, (h + 1) * MEM_DH) for h in range(MEM_HEADS)]
    scores = [_dot_nt(q[:, hs], k_ref[0, :, hs]) for hs in heads]
    probs = []
    for s in scores:
        e = jnp.exp(s - jnp.max(s, axis=-1, keepdims=True))
        probs.append((e / jnp.sum(e, axis=-1, keepdims=True)).astype(BF16))
    for hs, p in zip(heads, probs):
        o_scr[:, hs] = _dot(p, v_ref[0, :, hs]).astype(BF16)
    attn = _dot(o_scr[...], wo_ref[...])
    x2_ref[...] = _ln(x1 * ALPHA + attn, g_ref[...], b_ref[...])


def _attn(x1, mem_k, mem_v, wq, wo, g, b):
    bsz, l, d = x1.shape
    tl = min(l, TOKEN_TILE)
    assert l % tl == 0
    nt = l // tl
    const = lambda shape: pl.BlockSpec(shape, lambda bi, ti: (0,) * len(shape))
    return pl.pallas_call(
        _attn_kernel,
        grid=(bsz, nt),
        in_specs=[
            pl.BlockSpec((1, tl, d), lambda bi, ti: (bi, ti, 0)),
            pl.BlockSpec((1, N_MEM, d), lambda bi, ti: (bi, 0, 0)),
            pl.BlockSpec((1, N_MEM, d), lambda bi, ti: (bi, 0, 0)),
            const((d, d)), const((d, d)), const((1, d)), const((1, d)),
        ],
        out_specs=pl.BlockSpec((tl, d), lambda bi, ti: (bi * nt + ti, 0)),
        out_shape=jax.ShapeDtypeStruct((bsz * l, d), F32),
        scratch_shapes=[pltpu.VMEM((tl, d), BF16)],
        compiler_params=pltpu.CompilerParams(
            dimension_semantics=("arbitrary", "arbitrary"), vmem_limit_bytes=VMEM_LIMIT),
        name="attn",
    )(x1, mem_k, mem_v, wq, wo, g, b)


def _two_part_specs(t, d, tiles_a):
    spec_a = pl.BlockSpec((t, d), lambda i, *_: (jnp.minimum(i, tiles_a - 1), 0))
    spec_b = pl.BlockSpec((t, d), lambda i, *_: (jnp.maximum(i - tiles_a, 0), 0))
    return spec_a, spec_b


def _router_kernel(xa_ref, xb_ref, wrt_ref, bias_ref, e_ref, g_ref, r_ref, cnt_ref, cnt_scr, *, tl, tiles_a):
    i = pl.program_id(0)

    @pl.when(i == 0)
    def _():
        cnt_scr[...] = jnp.zeros_like(cnt_scr)

    x = jnp.where(i < tiles_a, xa_ref[...], xb_ref[...])
    xh = x.astype(BF16)
    xm = (x - xh.astype(F32)).astype(BF16)
    wh = wrt_ref[0]
    wm = wrt_ref[1]
    logits = _dot_nt(wh, xh) + (_dot_nt(wh, xm) + _dot_nt(wm, xh))
    scores = jax.nn.sigmoid(logits)
    biased = scores + bias_ref[...]
    ninf = -jnp.inf
    eg = EXPERTS_PER_GROUP
    riota = lax.broadcasted_iota(I32, (eg, tl), 0)
    gs_rows = []
    for g in range(N_GROUPS):
        blk = biased[g * eg:(g + 1) * eg, :]
        m1 = jnp.max(blk, axis=0, keepdims=True)
        i1 = jnp.min(jnp.where(blk == m1, riota, eg), axis=0, keepdims=True)
        m2 = jnp.max(jnp.where(riota == i1, ninf, blk), axis=0, keepdims=True)
        gs_rows.append(m1 + m2)
    gs = jnp.concatenate(gs_rows, axis=0)
    giota = lax.broadcasted_iota(I32, (N_GROUPS, tl), 0)
    sel = jnp.zeros((N_GROUPS, tl), jnp.bool_)
    for _ in range(TOPK_GROUPS):
        m = jnp.max(gs, axis=0, keepdims=True)
        gi = jnp.min(jnp.where(gs == m, giota, N_GROUPS), axis=0, keepdims=True)
        hit = giota == gi
        sel = jnp.logical_or(sel, hit)
        gs = jnp.where(hit, ninf, gs)
    self = jnp.where(sel, 1.0, 0.0)
    masked = jnp.concatenate(
        [jnp.where(self[g:g + 1, :] > 0.5, biased[g * eg:(g + 1) * eg, :], ninf) for g in range(N_GROUPS)], axis=0)
    eiota = lax.broadcasted_iota(I32, (N_EXPERTS, tl), 0)
    idx_rows, sc_rows = [], []
    multi = jnp.zeros((N_EXPERTS, tl), F32)
    for _ in range(TOP_K):
        m = jnp.max(masked, axis=0, keepdims=True)
        idx = jnp.min(jnp.where(masked == m, eiota, N_EXPERTS), axis=0, keepdims=True)
        hit = eiota == idx
        sc_rows.append(jnp.sum(jnp.where(hit, scores, 0.0), axis=0, keepdims=True))
        idx_rows.append(idx)
        multi = jnp.where(hit, 1.0, multi)
        masked = jnp.where(hit, ninf, masked)
    top_e = jnp.concatenate(idx_rows, axis=0)
    sc = jnp.concatenate(sc_rows, axis=0)
    e_ref[...] = top_e
    g_ref[...] = sc / jnp.sum(sc, axis=0, keepdims=True) * ROUTED_SCALE
    mh = multi.astype(BF16)
    before = (lax.broadcasted_iota(I32, (tl, tl), 0) < lax.broadcasted_iota(I32, (tl, tl), 1)).astype(BF16)
    running = cnt_scr[...]
    rankmat = _dot(mh, before) + jnp.concatenate([running] * (tl // 128), axis=1)
    r_rows = [jnp.sum(jnp.where(eiota == idx_rows[k], rankmat, 0.0), axis=0, keepdims=True) for k in range(TOP_K)]
    r_ref[...] = jnp.concatenate(r_rows, axis=0).astype(I32)
    total = running + _dot(mh, jnp.ones((tl, 128), BF16))
    cnt_scr[...] = total
    cnt_ref[...] = total


def _router(x2a, x2b, wrt, bias_col):
    d = x2a.shape[1]
    tl = TOKEN_TILE
    assert x2a.shape[0] % tl == 0 and x2b.shape[0] % tl == 0
    tiles_a = x2a.shape[0] // tl
    n = x2a.shape[0] + x2b.shape[0]
    kspec = pl.BlockSpec((TOP_K, tl), lambda i: (0, i))
    return pl.pallas_call(
        functools.partial(_router_kernel, tl=tl, tiles_a=tiles_a),
        grid=(n // tl,),
        in_specs=[*_two_part_specs(tl, d, tiles_a),
                  pl.BlockSpec((2, N_EXPERTS, d), lambda i: (0, 0, 0)),
                  pl.BlockSpec((N_EXPERTS, 1), lambda i: (0, 0))],
        out_specs=[kspec, kspec, kspec, pl.BlockSpec((N_EXPERTS, 128), lambda i: (0, 0))],
        out_shape=[jax.ShapeDtypeStruct((TOP_K, n), I32), jax.ShapeDtypeStruct((TOP_K, n), F32),
                   jax.ShapeDtypeStruct((TOP_K, n), I32), jax.ShapeDtypeStruct((N_EXPERTS, 128), F32)],
        scratch_shapes=[pltpu.VMEM((N_EXPERTS, 128), F32)],
        compiler_params=pltpu.CompilerParams(dimension_semantics=("arbitrary",), vmem_limit_bytes=VMEM_LIMIT),
        name="router",
    )(x2a, x2b, wrt, bias_col)


def _slots_to_blocks(v):
    t = v.shape[0]
    return jnp.transpose(v.reshape(t // SUBLANES, SUBLANES, SUBLANES, LANES), (2, 0, 1, 3)).reshape(SUBLANES, t, LANES)


def _rows_to_slots(x):
    t = x.shape[0]
    blocks = jnp.stack([x[:, s * LANES:(s + 1) * LANES] for s in range(SUBLANES)], axis=0)
    return jnp.transpose(blocks.reshape(SUBLANES, t // SUBLANES, SUBLANES, LANES), (1, 2, 0, 3)).reshape(
        t, SUBLANES, LANES)


def _slots_to_rows(v):
    blocks = _slots_to_blocks(v)
    return jnp.concatenate([blocks[s] for s in range(SUBLANES)], axis=1)


def _dispatch_kernel(fs_ref, fl_ref, nu_ref, dest_ref, xa_ref, xb_ref, xs_ref, stage, zbuf, sem, zsem,
                     *, t, tm, tiles_a, n_blk):
    i = pl.program_id(0)
    nsteps = pl.num_programs(0)

    def zero_fill(start):
        def go(cp):
            if start:
                cp.start()
            else:
                cp.wait()

        def per_expert(e, carry):
            ln = fl_ref[e]
            off = fs_ref[e]
            bit = tm // 2
            while bit:
                @pl.when((ln & bit) != 0)
                def _(off=off, bit=bit):
                    go(pltpu.make_async_copy(zbuf.at[pl.ds(0, bit)], xs_ref.at[pl.ds(off, bit)], zsem))
                off = off + (ln & bit)
                bit //= 2
            return carry

        lax.fori_loop(0, N_EXPERTS, per_expert, 0)

        def per_block(b, carry):
            go(pltpu.make_async_copy(zbuf, xs_ref.at[pl.ds(b * tm, tm)], zsem))
            return carry

        lax.fori_loop(nu_ref[0], n_blk, per_block, 0)

    @pl.when(i == 0)
    def _():
        zbuf[...] = jnp.zeros_like(zbuf)
        zero_fill(True)

    def drain(slot):
        for k in range(TOP_K):
            pltpu.make_async_copy(stage.at[slot], xs_ref.at[pl.ds(0, t)], sem.at[slot]).wait()

    def step(slot):
        @pl.when(i >= 2)
        def _():
            drain(slot)

        stage[slot] = _rows_to_slots(jnp.where(i < tiles_a, xa_ref[...], xb_ref[...])).astype(BF16)

        def body(n, carry):
            src = stage.at[slot, n]
            for k in range(TOP_K):
                dst = xs_ref.at[dest_ref[n * TOP_K + k]]
                pltpu.make_async_copy(src, dst, sem.at[slot]).start(priority=k % 2)
            return carry

        lax.fori_loop(0, t, body, 0)

    for slot in range(2):
        @pl.when(i % 2 == slot)
        def _(slot=slot):
            step(slot)

    @pl.when(i == 0)
    def _():
        zero_fill(False)

    @pl.when(i == nsteps - 1)
    def _():
        for slot in range(2):
            @pl.when(jnp.logical_or(nsteps >= 2, (nsteps - 1) % 2 == slot))
            def _(slot=slot):
                drain(slot)


def _dispatch(fill_start, fill_len, nu, dest, x2a, x2b, *, n_blk, tm):
    d = x2a.shape[1]
    t = TOKEN_TILE
    assert x2a.shape[0] % t == 0 and x2b.shape[0] % t == 0
    tiles_a = x2a.shape[0] // t
    n = x2a.shape[0] + x2b.shape[0]
    return pl.pallas_call(
        functools.partial(_dispatch_kernel, t=t, tm=tm, tiles_a=tiles_a, n_blk=n_blk),
        grid_spec=pltpu.PrefetchScalarGridSpec(
            num_scalar_prefetch=3,
            grid=(n // t,),
            in_specs=[pl.BlockSpec((t * TOP_K,), lambda i, *_: (i,), memory_space=pltpu.SMEM),
                      *_two_part_specs(t, d, tiles_a)],
            out_specs=pl.BlockSpec(memory_space=pl.ANY),
            scratch_shapes=[pltpu.VMEM((2, t, SUBLANES, LANES), BF16), pltpu.VMEM((tm, SUBLANES, LANES), BF16),
                            pltpu.SemaphoreType.DMA((2,)), pltpu.SemaphoreType.DMA(())],
        ),
        out_shape=jax.ShapeDtypeStruct((n_blk * tm, SUBLANES, LANES), BF16),
        compiler_params=pltpu.CompilerParams(dimension_semantics=("arbitrary",), vmem_limit_bytes=VMEM_LIMIT),
        name="dispatch",
    )(fill_start, fill_len, nu, dest, x2a, x2b)


def _experts_kernel(b0_ref, nb_ref, cnt_ref, nu_ref, w1_ref, w3_ref, w2_ref, xs_ref, ys_ref,
                    w1b, w3b, w2b, xbuf, ybuf, in_sem, out_sem, *, tm, n_blk):
    e = pl.program_id(0)
    n_used = nu_ref[0]
    blk_rows = tm * SUBLANES

    def hbm_block(ref, b):
        return ref.at[pl.ds(pl.multiple_of(b * blk_rows, blk_rows), blk_rows)]

    def x_copy(b):
        s = lax.rem(b, X_DEPTH)
        return pltpu.make_async_copy(hbm_block(xs_ref, b), xbuf.at[s], in_sem.at[s])

    def y_copy(b):
        return pltpu.make_async_copy(ybuf.at[b % 2], hbm_block(ys_ref, b), out_sem.at[b % 2])

    @pl.when(e == 0)
    def _():
        for b in range(X_DEPTH - 1):
            @pl.when(b < n_used)
            def _(b=b):
                x_copy(b).start()

    w1b[...] = w1_ref[0].astype(BF16)
    w3b[...] = w3_ref[0].astype(BF16)
    w2b[...] = w2_ref[0].astype(BF16)
    b0 = b0_ref[e]
    cnt = cnt_ref[e]

    def block(j, carry):
        b = b0 + j
        x_copy(b).wait()

        @pl.when(b + X_DEPTH - 1 < n_used)
        def _():
            x_copy(b + X_DEPTH - 1).start()

        @pl.when(b >= 2)
        def _():
            y_copy(b - 2).wait()

        rows = tm // EXPERT_SPLIT
        starts = range(0, tm, rows)
        xs_l, a_l, g_l = [], [], []
        for r0 in starts:
            live = lax.broadcasted_iota(I32, (rows, 1), 0) < cnt - j * tm - r0
            part = xbuf[lax.rem(b, X_DEPTH), pl.ds(r0 * SUBLANES, rows * SUBLANES), :].astype(F32)
            x = _slots_to_rows(part.reshape(rows, SUBLANES, LANES))
            xs_l.append(jnp.where(live, x, 0.0).astype(BF16))
        for x in xs_l:
            a_l.append(_dot(x, w1b[...]))
            g_l.append(_dot(x, w3b[...]))
        h_l = [((a * jax.nn.sigmoid(a)) * g).astype(BF16) for a, g in zip(a_l, g_l)]
        for r0, h in zip(starts, h_l):
            y = _rows_to_slots(_dot(h, w2b[...])).reshape(rows * SUBLANES, LANES)
            ybuf[b % 2, pl.ds(r0 * SUBLANES, rows * SUBLANES), :] = y.astype(BF16)
        y_copy(b).start()
        return carry

    lax.fori_loop(0, nb_ref[e], block, 0)

    @pl.when(e == pl.num_programs(0) - 1)
    def _():
        @pl.when(n_used >= 2)
        def _():
            y_copy(n_used - 2).wait()

        y_copy(n_used - 1).wait()
        ybuf[0] = jnp.zeros(ybuf.shape[1:], BF16)

        def zero_copy(b):
            return pltpu.make_async_copy(ybuf.at[0], hbm_block(ys_ref, b), out_sem.at[0])

        def zero(b, carry):
            zero_copy(b).start()
            return carry

        lax.fori_loop(n_used, n_blk, zero, 0)

        def zero_wait(b, carry):
            zero_copy(b).wait()
            return carry

        lax.fori_loop(n_used, n_blk, zero_wait, 0)


def _experts(blk0, nblk, counts, nu, xs, w1, w3, w2, *, tm):
    n_slot = xs.shape[0]
    n_blk = n_slot // tm
    n_exp, d, de = w1.shape
    wmap = lambda e, *_: (e, 0, 0)
    ys = pl.pallas_call(
        functools.partial(_experts_kernel, tm=tm, n_blk=n_blk),
        grid_spec=pltpu.PrefetchScalarGridSpec(
            num_scalar_prefetch=4,
            grid=(n_exp,),
            in_specs=[pl.BlockSpec((1, d, de), wmap), pl.BlockSpec((1, d, de), wmap), pl.BlockSpec((1, de, d), wmap),
                      pl.BlockSpec(memory_space=pl.ANY)],
            out_specs=pl.BlockSpec(memory_space=pl.ANY),
            scratch_shapes=[pltpu.VMEM((d, de), BF16), pltpu.VMEM((d, de), BF16), pltpu.VMEM((de, d), BF16),
                            pltpu.VMEM((X_DEPTH, tm * SUBLANES, LANES), BF16), pltpu.VMEM((2, tm * SUBLANES, LANES), BF16),
                            pltpu.SemaphoreType.DMA((X_DEPTH,)), pltpu.SemaphoreType.DMA((2,))],
        ),
        out_shape=jax.ShapeDtypeStruct((n_slot * SUBLANES, LANES), BF16),
        compiler_params=pltpu.CompilerParams(dimension_semantics=("arbitrary",), vmem_limit_bytes=VMEM_LIMIT),
        name="experts",
    )(blk0, nblk, counts, nu, w1, w3, w2, xs.reshape(n_slot * SUBLANES, LANES))
    return ys.reshape(n_slot, SUBLANES, LANES)


def _combine_kernel(dcur_ref, dnxt_ref, x2_ref, g_ref, w1_ref, w3_ref, w2_ref, lg_ref, lb_ref, ys_ref,
                    out_ref, buf, acc_scr, sem, *, t, nsteps):
    i = pl.program_id(0)

    def fetch(d_ref, s, n):
        for k in range(TOP_K):
            pltpu.make_async_copy(ys_ref.at[d_ref[n * TOP_K + k]], buf.at[s, k, n],
                                  sem.at[s]).start(priority=k % 2)

    def drain(s):
        for k in range(TOP_K):
            pltpu.make_async_copy(ys_ref.at[pl.ds(0, t)], buf.at[s, k], sem.at[s]).wait()

    @pl.when(i == 0)
    def _():
        def first(n, carry):
            fetch(dcur_ref, 0, n)
            return carry

        lax.fori_loop(0, t, first, 0)

    def step(s):
        drain(s)

        def group(gi, carry):
            r0 = pl.multiple_of(gi * SUBLANES, SUBLANES)
            for j in range(SUBLANES):
                fetch(dnxt_ref, 1 - s, r0 + j)
            gates = g_ref[pl.ds(r0, SUBLANES), :]
            gk = [jnp.broadcast_to(gates[:, k:k + 1], (SUBLANES, LANES)) for k in range(TOP_K)]
            acc = [None] * SUBLANES
            for k in range(TOP_K):
                rows8 = buf[s, k, pl.ds(r0, SUBLANES)].astype(F32)
                blocks = _slots_to_blocks(rows8)
                for lb in range(SUBLANES):
                    term = gk[k] * blocks[lb]
                    acc[lb] = term if acc[lb] is None else acc[lb] + term
            for lb in range(SUBLANES):
                acc_scr[pl.ds(r0, SUBLANES), lb * LANES:(lb + 1) * LANES] = acc[lb]
            return carry

        lax.fori_loop(0, t // SUBLANES, group, 0)

    for s in range(2):
        @pl.when(i % 2 == s)
        def _(s=s):
            step(s)

    @pl.when(i == nsteps - 1)
    def _():
        drain(nsteps % 2)

    x2 = x2_ref[...]
    acc = acc_scr[...]
    xb = x2.astype(BF16)
    a = _dot(xb, w1_ref[...])
    hs = (a * jax.nn.sigmoid(a)) * _dot(xb, w3_ref[...])
    shared = _dot(hs.astype(BF16), w2_ref[...])
    out_ref[...] = _ln(x2 * ALPHA + (acc + shared), lg_ref[...], lb_ref[...])


def _combine(dest, x2, gate_t, w1s, w3s, w2s, lg, lb, ys, *, row_off):
    n_rows, d = x2.shape
    n = dest.shape[0] // TOP_K
    t = TOKEN_TILE
    assert row_off % t == 0 and n_rows % t == 0 and n % t == 0
    off = row_off // t
    last = n // t - 1
    de = w1s.shape[-1]
    const = lambda shape: pl.BlockSpec(shape, lambda i: (0,) * len(shape))
    return pl.pallas_call(
        functools.partial(_combine_kernel, t=t, nsteps=n_rows // t),
        grid=(n_rows // t,),
        in_specs=[pl.BlockSpec((t * TOP_K,), lambda i: (off + i,), memory_space=pltpu.SMEM),
                  pl.BlockSpec((t * TOP_K,), lambda i: (jnp.minimum(off + i + 1, last),), memory_space=pltpu.SMEM),
                  pl.BlockSpec((t, d), lambda i: (i, 0)),
                  pl.BlockSpec((t, TOP_K), lambda i: (off + i, 0)),
                  const((d, de)), const((d, de)), const((de, d)), const((1, d)), const((1, d)),
                  pl.BlockSpec(memory_space=pl.ANY)],
        out_specs=pl.BlockSpec((t, d), lambda i: (i, 0)),
        out_shape=jax.ShapeDtypeStruct((n_rows, d), F32),
        scratch_shapes=[pltpu.VMEM((2, TOP_K, t, SUBLANES, LANES), BF16), pltpu.VMEM((t, d), F32),
                        pltpu.SemaphoreType.DMA((2,))],
        compiler_params=pltpu.CompilerParams(dimension_semantics=("arbitrary",), vmem_limit_bytes=VMEM_LIMIT),
        name="combine",
    )(dest, dest, x2, gate_t, w1s, w3s, w2s, lg, lb, ys)


EXPERT_TM = 512
EXPERT_SPLIT = 2
X_DEPTH = 4


def _moe_plan(cnt):
    tm = EXPERT_TM
    counts = cnt[:, 0].astype(I32)
    padded = (counts + tm - 1) // tm * tm
    pad_end = jnp.cumsum(padded)
    pad_start = pad_end - padded
    n_used = pad_end[-1] // tm
    fill_start = (pad_start + counts).astype(I32)
    fill_len = (padded - counts).astype(I32)
    return (pad_start.astype(I32), (pad_start // tm).astype(I32), (padded // tm).astype(I32), counts,
            n_used.reshape(1).astype(I32), fill_start, fill_len)


def _slots_kernel(ps_ref, e_ref, r_ref, d_ref):
    e = e_ref[...]

    def body(j, base):
        return jnp.where(e == j, ps_ref[j], base)

    base = lax.fori_loop(0, N_EXPERTS, body, jnp.zeros_like(e), unroll=8)
    d_ref[...] = base + r_ref[...]


def _slots(pad_start, top_e, rank):
    k, n = top_e.shape
    tl = next(c for c in (2048, 1536, 1024, 512, 256, 128) if n % c == 0)
    spec = pl.BlockSpec((k, tl), lambda i, ps: (0, i))
    return pl.pallas_call(
        _slots_kernel,
        grid_spec=pltpu.PrefetchScalarGridSpec(num_scalar_prefetch=1, grid=(n // tl,), in_specs=[spec, spec],
                                               out_specs=spec),
        out_shape=jax.ShapeDtypeStruct((k, n), I32),
        compiler_params=pltpu.CompilerParams(dimension_semantics=("arbitrary",)),
        name="slots",
    )(pad_start, top_e, rank)


PAST_LEN = 1024


def kernel(x_prompt, x_sample, state_gla, cache_pool, cache_mem_k, cache_mem_v, mem_prompt, ln_in_g, ln_in_b, w_in, w_gate_up, b_gate, gla_norm_g, pool_w, pool_scale, w_out, ln1_g, ln1_b, wq_mem, wk_mem, wv_mem, wo_mem, ln2_g, ln2_b, w_router, router_bias, w1_exp, w3_exp, w2_exp, w1_sh, w3_sh, w2_sh, ln3_g, ln3_b):
    assert w_in.shape[0] == 1, "single-layer trunk"
    bp, lp, d = x_prompt.shape
    bs, ls, _ = x_sample.shape
    n_p, n_s = bp * lp, bs * ls
    n_all = n_p + n_s

    wts_a = _prep_trunk_a_weights(ln_in_g, ln_in_b, w_in[0], w_gate_up[0], b_gate[0], gla_norm_g[0], pool_w[0],
                                  pool_scale[0], w_out[0], ln1_g[0], ln1_b[0])
    s0 = jnp.zeros((bp, GLA_HEADS, GLA_DK, GLA_DV), F32)
    h0 = jnp.zeros((bp, POOL_HIST, POOL_WIDTH), F32)
    x1p, sp, hp = _trunk_a(x_prompt, s0, h0, wts_a, start_pos=0)
    x1s, ss, hs = _trunk_a(x_sample, state_gla[0], cache_pool[0], wts_a, start_pos=PAST_LEN)

    mk, mv, mkb, mvb = _mem_kv(mem_prompt.reshape(bp * N_MEM, d), wk_mem[0].astype(BF16), wv_mem[0].astype(BF16))
    wq, wo = wq_mem[0].astype(BF16), wo_mem[0].astype(BF16)
    g2, b2 = ln2_g[0].reshape(1, d), ln2_b[0].reshape(1, d)
    x2p = _attn(x1p, mkb.reshape(bp, N_MEM, d), mvb.reshape(bp, N_MEM, d), wq, wo, g2, b2)
    x2s = _attn(x1s, cache_mem_k[0].reshape(bs, N_MEM, d).astype(BF16),
                cache_mem_v[0].reshape(bs, N_MEM, d).astype(BF16), wq, wo, g2, b2)

    wrt = w_router[0].T
    wrt_h = wrt.astype(BF16)
    wrt_m = (wrt - wrt_h.astype(F32)).astype(BF16)
    top_e, gate, rank, cnt = _router(x2p, x2s, jnp.stack([wrt_h, wrt_m]), router_bias[0].reshape(N_EXPERTS, 1))

    assert (n_all * TOP_K) % EXPERT_TM == 0
    n_blk = n_all * TOP_K // EXPERT_TM + N_EXPERTS
    pad_start, blk0, nblk, counts, nu, fill_start, fill_len = _moe_plan(cnt)
    dest = _slots(pad_start, top_e, rank).T.reshape(-1)
    xs = _dispatch(fill_start, fill_len, nu, dest, x2p, x2s, n_blk=n_blk, tm=EXPERT_TM)
    ys = _experts(blk0, nblk, counts, nu, xs, w1_exp[0], w3_exp[0], w2_exp[0], tm=EXPERT_TM)
    sh = (w1_sh[0].astype(BF16), w3_sh[0].astype(BF16), w2_sh[0].astype(BF16),
          ln3_g[0].reshape(1, d), ln3_b[0].reshape(1, d))
    gate_t = gate.T
    yp = _combine(dest, x2p, gate_t, *sh, ys, row_off=0)
    ysm = _combine(dest, x2s, gate_t, *sh, ys, row_off=n_p)

    return (yp.reshape(bp, lp, d), ysm.reshape(bs, ls, d), sp[None], hp[None],
            mk.reshape(1, bp, N_MEM, MEM_HEADS, MEM_DH), mv.reshape(1, bp, N_MEM, MEM_HEADS, MEM_DH),
            ss[None], hs[None])
```

```python
import functools

import jax
import jax.numpy as jnp
from jax import lax
from jax.experimental import pallas as pl
from jax.experimental.pallas import tpu as pltpu

F32 = jnp.float32
BF16 = jnp.bfloat16
I32 = jnp.int32

D_MODEL = 1024
CHUNK = 64
SUB = 16
GLA_HEADS = 4
GLA_DK = 64
GLA_DV = 128
GLA_KEY = GLA_HEADS * GLA_DK
GLA_WIDTH = GLA_HEADS * GLA_DV
GATE_RANK = 16
GATE_PAD = 128
POOL_WIDTH = 512
POOL_WINDOWS = (2, 4, 8, 16)
POOL_GC = 128
POOL_HIST = 15
HIST_ROWS = 16
N_MEM = 256
MEM_HEADS = 4
MEM_DH = 256
N_EXPERTS = 256
N_GROUPS = 8
EXPERTS_PER_GROUP = 32
TOPK_GROUPS = 4
TOP_K = 8
ROUTED_SCALE = 2.5
TOKEN_TILE = 512
SUBLANES, LANES = 8, 128
assert D_MODEL == SUBLANES * LANES
ALPHA = 2.0 ** 0.25
EPS = 1e-5
OFF_Q, OFF_K, OFF_V, OFF_G, OFF_U, OFF_GD = 0, 256, 512, 1024, 1536, 2048
D_IN_PAD = OFF_GD + GATE_PAD

VMEM_LIMIT = 56 * 1024 * 1024


def _ln(x, g, b):
    mu = jnp.mean(x, axis=-1, keepdims=True)
    xc = x - mu
    var = jnp.mean(xc * xc, axis=-1, keepdims=True)
    return xc * lax.rsqrt(var + EPS) * g + b


def _dot(a, b):
    return jnp.dot(a, b, preferred_element_type=F32)


def _dot_nt(a, b):
    return lax.dot_general(a, b, (((1,), (1,)), ((), ())), preferred_element_type=F32)


def _dot_tn(a, b):
    return lax.dot_general(a, b, (((0,), (0,)), ((), ())), preferred_element_type=F32)


def _split3(x):
    h = x.astype(BF16)
    r = x - h.astype(F32)
    m = r.astype(BF16)
    l = (r - m.astype(F32)).astype(BF16)
    return h, m, l


def _trunk_a_kernel(x_ref, s0_ref, h0_ref, lng_ref, lnb_ref, win_ref, wgu_ref, bg_ref, gng_ref,
                    pw_ref, ps_ref, wout_ref, l1g_ref, l1b_ref, tri_ref,
                    x1_ref, sn_ref, hn_ref,
                    proj_scr, ext_scr, s_scr, op_scr, *, tl, chunk, start_pos):
    t = pl.program_id(1)
    nt = pl.num_programs(1)

    @pl.when(t == 0)
    def _():
        s_scr[...] = s0_ref[0]
        ext_scr[0:1, :] = jnp.zeros((1, POOL_WIDTH), F32)
        ext_scr[1:HIST_ROWS, :] = h0_ref[0]

    xn = _ln(x_ref[0], lng_ref[...], lnb_ref[...])
    proj_scr[...] = _dot(xn.astype(BF16), win_ref[...])

    c = chunk
    shift = lambda a, n: lax.shift_right_logical(a, n.bit_length() - 1)
    gd = proj_scr[:, OFF_GD:OFF_GD + GATE_PAD]
    z = _dot(gd.astype(BF16), wgu_ref[...]) + bg_ref[...]
    lf = (jnp.minimum(z, 0.0) - jnp.log(1.0 + jnp.exp(-jnp.abs(z)))) * (1.0 / 16.0)
    tri = tri_ref[...]
    lh, lm, ll = _split3(lf)
    cum = _dot(tri, lh) + _dot(tri, lm) + _dot(tri, ll)
    q_all = proj_scr[:, OFF_Q:OFF_Q + GLA_KEY] * (GLA_DK ** -0.5)
    k_all = proj_scr[:, OFF_K:OFF_K + GLA_KEY]
    qs_all = q_all * jnp.exp(cum)

    causal = lax.broadcasted_iota(I32, (c, c), 1) <= lax.broadcasted_iota(I32, (c, c), 0)
    eye_dk = lax.broadcasted_iota(I32, (GLA_DK, GLA_DK), 0) == lax.broadcasted_iota(I32, (GLA_DK, GLA_DK), 1)
    n_sub = c // SUB
    rblk = shift(lax.broadcasted_iota(I32, (c, n_sub * GLA_DK), 0), SUB)
    lblk = shift(lax.broadcasted_iota(I32, (c, n_sub * GLA_DK), 1), GLA_DK)
    lblk_row = shift(lax.broadcasted_iota(I32, (1, n_sub * GLA_DK), 1), GLA_DK)
    mask_q = rblk == lblk
    mask_k = rblk <= lblk

    def tile_lanes(a):
        return jnp.concatenate([a] * n_sub, axis=1)

    states = [s_scr[h] for h in range(GLA_HEADS)]
    units = [(ci, h) for ci in range(tl // c) for h in range(GLA_HEADS)]
    att_l, kv_l, dcol_l, v_l = {}, {}, {}, {}
    for ci, h in units:
        rs = slice(ci * c, (ci + 1) * c)
        ksl = slice(h * GLA_DK, (h + 1) * GLA_DK)
        v_l[ci, h] = proj_scr[rs, OFF_V + h * GLA_DV:OFF_V + (h + 1) * GLA_DV].astype(BF16)
        cum_t = tile_lanes(cum[rs, ksl])
        q_t = tile_lanes(q_all[rs, ksl])
        k_t = tile_lanes(k_all[rs, ksl])
        ref_row = jnp.zeros((1, n_sub * GLA_DK), F32)
        for i in range(1, n_sub):
            ref_row = jnp.where(lblk_row == i, cum_t[i * SUB - 1:i * SUB, :], ref_row)
        arg = cum_t - ref_row
        lhs = jnp.where(mask_q, q_t * jnp.exp(jnp.where(mask_q, arg, 0.0)), 0.0)
        rhs = jnp.where(mask_k, k_t * jnp.exp(jnp.where(mask_k, -arg, 0.0)), 0.0)
        att_l[ci, h] = jnp.where(causal, _dot_nt(lhs.astype(BF16), rhs.astype(BF16)), 0.0).astype(BF16)
    for ci, h in units:
        rs = slice(ci * c, (ci + 1) * c)
        ksl = slice(h * GLA_DK, (h + 1) * GLA_DK)
        last = cum[(ci + 1) * c - 1:(ci + 1) * c, ksl]
        ks_h = k_all[rs, ksl] * jnp.exp(last - cum[rs, ksl])
        kv_l[ci, h] = _dot_tn(ks_h.astype(BF16), v_l[ci, h])
        dcol_l[ci, h] = jnp.sum(jnp.where(eye_dk, jnp.broadcast_to(jnp.exp(last), (GLA_DK, GLA_DK)), 0.0),
                                axis=1, keepdims=True)
    for ci, h in units:
        rs = slice(ci * c, (ci + 1) * c)
        ksl = slice(h * GLA_DK, (h + 1) * GLA_DK)
        vsl = slice(h * GLA_DV, (h + 1) * GLA_DV)
        s_h = states[h]
        o_h = _dot(att_l[ci, h], v_l[ci, h]) + _dot(qs_all[rs, ksl].astype(BF16), s_h.astype(BF16))
        states[h] = dcol_l[ci, h] * s_h + kv_l[ci, h]
        o_h = o_h * lax.rsqrt(jnp.mean(o_h * o_h, axis=-1, keepdims=True) + EPS) * gng_ref[...]
        g_h = proj_scr[rs, OFF_G + h * GLA_DV:OFF_G + (h + 1) * GLA_DV]
        op_scr[rs, vsl] = (o_h * (g_h * jax.nn.sigmoid(g_h))).astype(BF16)
    for h in range(GLA_HEADS):
        s_scr[h] = states[h]

    u = proj_scr[:, OFF_U:OFF_U + POOL_WIDTH]
    ext_scr[HIST_ROWS:HIST_ROWS + tl, :] = u
    n_valid = start_pos + t * tl + lax.broadcasted_iota(I32, (tl, 1), 0) + 1
    for gi, w in enumerate(POOL_WINDOWS):
        lsl = slice(gi * POOL_GC, (gi + 1) * POOL_GC)
        win = ext_scr[HIST_ROWS:HIST_ROWS + tl, lsl]
        for s in range(1, w):
            win = win + ext_scr[HIST_ROWS - s:HIST_ROWS - s + tl, lsl]
        cnt = jnp.minimum(w, n_valid).astype(F32)
        r = win / cnt - u[:, lsl]
        p = _dot(r.astype(BF16), pw_ref[gi]) * ps_ref[:, lsl]
        op_scr[:, GLA_WIDTH + gi * POOL_GC:GLA_WIDTH + (gi + 1) * POOL_GC] = p.astype(BF16)
    tail = ext_scr[tl:tl + HIST_ROWS, :]
    ext_scr[0:HIST_ROWS, :] = tail

    mix = _dot(op_scr[...], wout_ref[...])
    x1_ref[0] = _ln(xn * ALPHA + mix, l1g_ref[...], l1b_ref[...])

    @pl.when(t == nt - 1)
    def _():
        sn_ref[0] = s_scr[...]
        hn_ref[0] = ext_scr[1:HIST_ROWS, :]


def _trunk_a(x, s0, h0, wts, *, start_pos):
    b, l, d = x.shape
    tl = min(l, 256)
    chunk = min(tl, CHUNK)
    assert l % tl == 0 and tl % chunk == 0 and chunk % SUB == 0 and l >= HIST_ROWS
    nt = l // tl
    kern = functools.partial(_trunk_a_kernel, tl=tl, chunk=chunk, start_pos=start_pos)
    const = lambda shape: pl.BlockSpec(shape, lambda bi, ti: (0,) * len(shape))
    pos = jnp.arange(tl)
    tri = ((pos[None, :] <= pos[:, None]) & (pos[None, :] // chunk == pos[:, None] // chunk)).astype(BF16)
    return pl.pallas_call(
        kern,
        grid=(b, nt),
        in_specs=[
            pl.BlockSpec((1, tl, d), lambda bi, ti: (bi, ti, 0)),
            pl.BlockSpec((1, GLA_HEADS, GLA_DK, GLA_DV), lambda bi, ti: (bi, 0, 0, 0)),
            pl.BlockSpec((1, POOL_HIST, POOL_WIDTH), lambda bi, ti: (bi, 0, 0)),
            const((1, d)), const((1, d)),
            const((d, D_IN_PAD)), const((GATE_PAD, GLA_KEY)), const((1, GLA_KEY)), const((1, GLA_DV)),
            const((len(POOL_WINDOWS), POOL_GC, POOL_GC)), const((1, POOL_WIDTH)),
            const((GLA_WIDTH + POOL_WIDTH, d)), const((1, d)), const((1, d)), const((tl, tl)),
        ],
        out_specs=[
            pl.BlockSpec((1, tl, d), lambda bi, ti: (bi, ti, 0)),
            pl.BlockSpec((1, GLA_HEADS, GLA_DK, GLA_DV), lambda bi, ti: (bi, 0, 0, 0)),
            pl.BlockSpec((1, POOL_HIST, POOL_WIDTH), lambda bi, ti: (bi, 0, 0)),
        ],
        out_shape=[
            jax.ShapeDtypeStruct((b, l, d), F32),
            jax.ShapeDtypeStruct((b, GLA_HEADS, GLA_DK, GLA_DV), F32),
            jax.ShapeDtypeStruct((b, POOL_HIST, POOL_WIDTH), F32),
        ],
        scratch_shapes=[
            pltpu.VMEM((tl, D_IN_PAD), F32),
            pltpu.VMEM((HIST_ROWS + tl, POOL_WIDTH), F32),
            pltpu.VMEM((GLA_HEADS, GLA_DK, GLA_DV), F32),
            pltpu.VMEM((tl, GLA_WIDTH + POOL_WIDTH), BF16),
        ],
        compiler_params=pltpu.CompilerParams(
            dimension_semantics=("arbitrary", "arbitrary"), vmem_limit_bytes=VMEM_LIMIT),
        name="trunk_a",
    )(x, s0, h0, *wts, tri)


def _prep_trunk_a_weights(ln_in_g, ln_in_b, w_in, w_gate_up, b_gate, gla_norm_g, pool_w, pool_scale, w_out,
                          ln1_g, ln1_b):
    d = D_MODEL
    p_gd = 2 * GLA_KEY + 2 * GLA_WIDTH
    w_in_r = jnp.concatenate(
        [w_in[:, :p_gd], w_in[:, p_gd + GATE_RANK:], w_in[:, p_gd:p_gd + GATE_RANK],
         jnp.zeros((d, GATE_PAD - GATE_RANK), w_in.dtype)], axis=1).astype(BF16)
    wgu = jnp.concatenate([w_gate_up, jnp.zeros((GATE_PAD - GATE_RANK, GLA_KEY), w_gate_up.dtype)],
                          axis=0).astype(BF16)
    return (ln_in_g.reshape(1, d), ln_in_b.reshape(1, d), w_in_r, wgu, b_gate.reshape(1, GLA_KEY),
            gla_norm_g.reshape(1, GLA_DV), pool_w.astype(BF16), pool_scale.reshape(1, POOL_WIDTH),
            w_out.astype(BF16), ln1_g.reshape(1, d), ln1_b.reshape(1, d))


def _mem_kv_kernel(m_ref, wk_ref, wv_ref, k_ref, v_ref, kb_ref, vb_ref):
    m = m_ref[...].astype(BF16)
    k = _dot(m, wk_ref[...])
    v = _dot(m, wv_ref[...])
    k_ref[...] = k
    v_ref[...] = v
    kb_ref[...] = k.astype(BF16)
    vb_ref[...] = v.astype(BF16)


def _mem_kv(mem, wk, wv):
    m, d = mem.shape
    tm = min(m, 512)
    assert m % tm == 0
    row = pl.BlockSpec((tm, d), lambda i: (i, 0))
    wspec = pl.BlockSpec((d, d), lambda i: (0, 0))
    return pl.pallas_call(
        _mem_kv_kernel,
        grid=(m // tm,),
        in_specs=[row, wspec, wspec],
        out_specs=[row, row, row, row],
        out_shape=[jax.ShapeDtypeStruct((m, d), F32), jax.ShapeDtypeStruct((m, d), F32),
                   jax.ShapeDtypeStruct((m, d), BF16), jax.ShapeDtypeStruct((m, d), BF16)],
        compiler_params=pltpu.CompilerParams(dimension_semantics=("arbitrary",), vmem_limit_bytes=VMEM_LIMIT),
        name="mem_kv",
    )(mem, wk, wv)


def _attn_kernel(x1_ref, k_ref, v_ref, wq_ref, wo_ref, g_ref, b_ref, x2_ref, o_scr):
    x1 = x1_ref[0]
    q = (_dot(x1.astype(BF16), wq_ref[...]) * (MEM_DH ** -0.5)).astype(BF16)
    heads = [slice(h * MEM_DH, (h + 1) * MEM_DH) for h in range(MEM_HEADS)]
    scores = [_dot_nt(q[:, hs], k_ref[0, :, hs]) for hs in heads]
    probs = []
    for s in scores:
        e = jnp.exp(s - jnp.max(s, axis=-1, keepdims=True))
        probs.append((e / jnp.sum(e, axis=-1, keepdims=True)).astype(BF16))
    for hs, p in zip(heads, probs):
        o_scr[:, hs] = _dot(p, v_ref[0, :, hs]).astype(BF16)
    attn = _dot(o_scr[...], wo_ref[...])
    x2_ref[...] = _ln(x1 * ALPHA + attn, g_ref[...], b_ref[...])


def _attn(x1, mem_k, mem_v, wq, wo, g, b):
    bsz, l, d = x1.shape
    tl = min(l, TOKEN_TILE)
    assert l % tl == 0
    nt = l // tl
    const = lambda shape: pl.BlockSpec(shape, lambda bi, ti: (0,) * len(shape))
    return pl.pallas_call(
        _attn_kernel,
        grid=(bsz, nt),
        in_specs=[
            pl.BlockSpec((1, tl, d), lambda bi, ti: (bi, ti, 0)),
            pl.BlockSpec((1, N_MEM, d), lambda bi, ti: (bi, 0, 0)),
            pl.BlockSpec((1, N_MEM, d), lambda bi, ti: (bi, 0, 0)),
            const((d, d)), const((d, d)), const((1, d)), const((1, d)),
        ],
        out_specs=pl.BlockSpec((tl, d), lambda bi, ti: (bi * nt + ti, 0)),
        out_shape=jax.ShapeDtypeStruct((bsz * l, d), F32),
        scratch_shapes=[pltpu.VMEM((tl, d), BF16)],
        compiler_params=pltpu.CompilerParams(
            dimension_semantics=("arbitrary", "arbitrary"), vmem_limit_bytes=VMEM_LIMIT),
        name="attn",
    )(x1, mem_k, mem_v, wq, wo, g, b)


def _two_part_specs(t, d, tiles_a):
    spec_a = pl.BlockSpec((t, d), lambda i, *_: (jnp.minimum(i, tiles_a - 1), 0))
    spec_b = pl.BlockSpec((t, d), lambda i, *_: (jnp.maximum(i - tiles_a, 0), 0))
    return spec_a, spec_b


def _router_kernel(xa_ref, xb_ref, wrt_ref, bias_ref, e_ref, g_ref, r_ref, cnt_ref, cnt_scr, *, tl, tiles_a):
    i = pl.program_id(0)

    @pl.when(i == 0)
    def _():
        cnt_scr[...] = jnp.zeros_like(cnt_scr)

    x = jnp.where(i < tiles_a, xa_ref[...], xb_ref[...])
    xh = x.astype(BF16)
    xm = (x - xh.astype(F32)).astype(BF16)
    wh = wrt_ref[0]
    wm = wrt_ref[1]
    logits = _dot_nt(wh, xh) + (_dot_nt(wh, xm) + _dot_nt(wm, xh))
    scores = jax.nn.sigmoid(logits)
    biased = scores + bias_ref[...]
    ninf = -jnp.inf
    eg = EXPERTS_PER_GROUP
    riota = lax.broadcasted_iota(I32, (eg, tl), 0)
    gs_rows = []
    for g in range(N_GROUPS):
        blk = biased[g * eg:(g + 1) * eg, :]
        m1 = jnp.max(blk, axis=0, keepdims=True)
        i1 = jnp.min(jnp.where(blk == m1, riota, eg), axis=0, keepdims=True)
        m2 = jnp.max(jnp.where(riota == i1, ninf, blk), axis=0, keepdims=True)
        gs_rows.append(m1 + m2)
    gs = jnp.concatenate(gs_rows, axis=0)
    giota = lax.broadcasted_iota(I32, (N_GROUPS, tl), 0)
    sel = jnp.zeros((N_GROUPS, tl), jnp.bool_)
    for _ in range(TOPK_GROUPS):
        m = jnp.max(gs, axis=0, keepdims=True)
        gi = jnp.min(jnp.where(gs == m, giota, N_GROUPS), axis=0, keepdims=True)
        hit = giota == gi
        sel = jnp.logical_or(sel, hit)
        gs = jnp.where(hit, ninf, gs)
    self = jnp.where(sel, 1.0, 0.0)
    masked = jnp.concatenate(
        [jnp.where(self[g:g + 1, :] > 0.5, biased[g * eg:(g + 1) * eg, :], ninf) for g in range(N_GROUPS)], axis=0)
    eiota = lax.broadcasted_iota(I32, (N_EXPERTS, tl), 0)
    idx_rows, sc_rows = [], []
    multi = jnp.zeros((N_EXPERTS, tl), F32)
    for _ in range(TOP_K):
        m = jnp.max(masked, axis=0, keepdims=True)
        idx = jnp.min(jnp.where(masked == m, eiota, N_EXPERTS), axis=0, keepdims=True)
        hit = eiota == idx
        sc_rows.append(jnp.sum(jnp.where(hit, scores, 0.0), axis=0, keepdims=True))
        idx_rows.append(idx)
        multi = jnp.where(hit, 1.0, multi)
        masked = jnp.where(hit, ninf, masked)
    top_e = jnp.concatenate(idx_rows, axis=0)
    sc = jnp.concatenate(sc_rows, axis=0)
    e_ref[...] = top_e
    g_ref[...] = sc / jnp.sum(sc, axis=0, keepdims=True) * ROUTED_SCALE
    mh = multi.astype(BF16)
    before = (lax.broadcasted_iota(I32, (tl, tl), 0) < lax.broadcasted_iota(I32, (tl, tl), 1)).astype(BF16)
    running = cnt_scr[...]
    rankmat = _dot(mh, before) + jnp.concatenate([running] * (tl // 128), axis=1)
    r_rows = [jnp.sum(jnp.where(eiota == idx_rows[k], rankmat, 0.0), axis=0, keepdims=True) for k in range(TOP_K)]
    r_ref[...] = jnp.concatenate(r_rows, axis=0).astype(I32)
    total = running + _dot(mh, jnp.ones((tl, 128), BF16))
    cnt_scr[...] = total
    cnt_ref[...] = total


def _router(x2a, x2b, wrt, bias_col):
    d = x2a.shape[1]
    tl = TOKEN_TILE
    assert x2a.shape[0] % tl == 0 and x2b.shape[0] % tl == 0
    tiles_a = x2a.shape[0] // tl
    n = x2a.shape[0] + x2b.shape[0]
    kspec = pl.BlockSpec((TOP_K, tl), lambda i: (0, i))
    return pl.pallas_call(
        functools.partial(_router_kernel, tl=tl, tiles_a=tiles_a),
        grid=(n // tl,),
        in_specs=[*_two_part_specs(tl, d, tiles_a),
                  pl.BlockSpec((2, N_EXPERTS, d), lambda i: (0, 0, 0)),
                  pl.BlockSpec((N_EXPERTS, 1), lambda i: (0, 0))],
        out_specs=[kspec, kspec, kspec, pl.BlockSpec((N_EXPERTS, 128), lambda i: (0, 0))],
        out_shape=[jax.ShapeDtypeStruct((TOP_K, n), I32), jax.ShapeDtypeStruct((TOP_K, n), F32),
                   jax.ShapeDtypeStruct((TOP_K, n), I32), jax.ShapeDtypeStruct((N_EXPERTS, 128), F32)],
        scratch_shapes=[pltpu.VMEM((N_EXPERTS, 128), F32)],
        compiler_params=pltpu.CompilerParams(dimension_semantics=("arbitrary",), vmem_limit_bytes=VMEM_LIMIT),
        name="router",
    )(x2a, x2b, wrt, bias_col)


def _slots_to_blocks(v):
    t = v.shape[0]
    return jnp.transpose(v.reshape(t // SUBLANES, SUBLANES, SUBLANES, LANES), (2, 0, 1, 3)).reshape(SUBLANES, t, LANES)


def _rows_to_slots(x):
    t = x.shape[0]
    blocks = jnp.stack([x[:, s * LANES:(s + 1) * LANES] for s in range(SUBLANES)], axis=0)
    return jnp.transpose(blocks.reshape(SUBLANES, t // SUBLANES, SUBLANES, LANES), (1, 2, 0, 3)).reshape(
        t, SUBLANES, LANES)


def _slots_to_rows(v):
    blocks = _slots_to_blocks(v)
    return jnp.concatenate([blocks[s] for s in range(SUBLANES)], axis=1)


def _dispatch_kernel(fs_ref, fl_ref, nu_ref, dest_ref, xa_ref, xb_ref, xs_ref, stage, zbuf, sem, zsem,
                     *, t, tm, tiles_a, n_blk):
    i = pl.program_id(0)
    nsteps = pl.num_programs(0)

    def zero_fill(start):
        def go(cp):
            if start:
                cp.start()
            else:
                cp.wait()

        def per_expert(e, carry):
            ln = fl_ref[e]
            off = fs_ref[e]
            bit = tm // 2
            while bit:
                @pl.when((ln & bit) != 0)
                def _(off=off, bit=bit):
                    go(pltpu.make_async_copy(zbuf.at[pl.ds(0, bit)], xs_ref.at[pl.ds(off, bit)], zsem))
                off = off + (ln & bit)
                bit //= 2
            return carry

        lax.fori_loop(0, N_EXPERTS, per_expert, 0)

        def per_block(b, carry):
            go(pltpu.make_async_copy(zbuf, xs_ref.at[pl.ds(b * tm, tm)], zsem))
            return carry

        lax.fori_loop(nu_ref[0], n_blk, per_block, 0)

    @pl.when(i == 0)
    def _():
        zbuf[...] = jnp.zeros_like(zbuf)
        zero_fill(True)

    def drain(slot):
        for k in range(TOP_K):
            pltpu.make_async_copy(stage.at[slot], xs_ref.at[pl.ds(0, t)], sem.at[slot]).wait()

    def step(slot):
        @pl.when(i >= 2)
        def _():
            drain(slot)

        stage[slot] = _rows_to_slots(jnp.where(i < tiles_a, xa_ref[...], xb_ref[...])).astype(BF16)

        def body(n, carry):
            src = stage.at[slot, n]
            for k in range(TOP_K):
                dst = xs_ref.at[dest_ref[n * TOP_K + k]]
                pltpu.make_async_copy(src, dst, sem.at[slot]).start(priority=k % 2)
            return carry

        lax.fori_loop(0, t, body, 0)

    for slot in range(2):
        @pl.when(i % 2 == slot)
        def _(slot=slot):
            step(slot)

    @pl.when(i == 0)
    def _():
        zero_fill(False)

    @pl.when(i == nsteps - 1)
    def _():
        for slot in range(2):
            @pl.when(jnp.logical_or(nsteps >= 2, (nsteps - 1) % 2 == slot))
            def _(slot=slot):
                drain(slot)


def _dispatch(fill_start, fill_len, nu, dest, x2a, x2b, *, n_blk, tm):
    d = x2a.shape[1]
    t = TOKEN_TILE
    assert x2a.shape[0] % t == 0 and x2b.shape[0] % t == 0
    tiles_a = x2a.shape[0] // t
    n = x2a.shape[0] + x2b.shape[0]
    return pl.pallas_call(
        functools.partial(_dispatch_kernel, t=t, tm=tm, tiles_a=tiles_a, n_blk=n_blk),
        grid_spec=pltpu.PrefetchScalarGridSpec(
            num_scalar_prefetch=3,
            grid=(n // t,),
            in_specs=[pl.BlockSpec((t * TOP_K,), lambda i, *_: (i,), memory_space=pltpu.SMEM),
                      *_two_part_specs(t, d, tiles_a)],
            out_specs=pl.BlockSpec(memory_space=pl.ANY),
            scratch_shapes=[pltpu.VMEM((2, t, SUBLANES, LANES), BF16), pltpu.VMEM((tm, SUBLANES, LANES), BF16),
                            pltpu.SemaphoreType.DMA((2,)), pltpu.SemaphoreType.DMA(())],
        ),
        out_shape=jax.ShapeDtypeStruct((n_blk * tm, SUBLANES, LANES), BF16),
        compiler_params=pltpu.CompilerParams(dimension_semantics=("arbitrary",), vmem_limit_bytes=VMEM_LIMIT),
        name="dispatch",
    )(fill_start, fill_len, nu, dest, x2a, x2b)


def _experts_kernel(b0_ref, nb_ref, cnt_ref, nu_ref, w1_ref, w3_ref, w2_ref, xs_ref, ys_ref,
                    w1b, w3b, w2b, xbuf, ybuf, in_sem, out_sem, *, tm, n_blk):
    e = pl.program_id(0)
    n_used = nu_ref[0]
    blk_rows = tm * SUBLANES

    def hbm_block(ref, b):
        return ref.at[pl.ds(pl.multiple_of(b * blk_rows, blk_rows), blk_rows)]

    def x_copy(b):
        s = lax.rem(b, X_DEPTH)
        return pltpu.make_async_copy(hbm_block(xs_ref, b), xbuf.at[s], in_sem.at[s])

    def y_copy(b):
        return pltpu.make_async_copy(ybuf.at[b % 2], hbm_block(ys_ref, b), out_sem.at[b % 2])

    @pl.when(e == 0)
    def _():
        for b in range(X_DEPTH - 1):
            @pl.when(b < n_used)
            def _(b=b):
                x_copy(b).start()

    w1b[...] = w1_ref[0].astype(BF16)
    w3b[...] = w3_ref[0].astype(BF16)
    w2b[...] = w2_ref[0].astype(BF16)
    b0 = b0_ref[e]
    cnt = cnt_ref[e]

    def block(j, carry):
        b = b0 + j
        x_copy(b).wait()

        @pl.when(b + X_DEPTH - 1 < n_used)
        def _():
            x_copy(b + X_DEPTH - 1).start()

        @pl.when(b >= 2)
        def _():
            y_copy(b - 2).wait()

        rows = tm // EXPERT_SPLIT
        starts = range(0, tm, rows)
        xs_l, a_l, g_l = [], [], []
        for r0 in starts:
            live = lax.broadcasted_iota(I32, (rows, 1), 0) < cnt - j * tm - r0
            part = xbuf[lax.rem(b, X_DEPTH), pl.ds(r0 * SUBLANES, rows * SUBLANES), :].astype(F32)
            x = _slots_to_rows(part.reshape(rows, SUBLANES, LANES))
            xs_l.append(jnp.where(live, x, 0.0).astype(BF16))
        for x in xs_l:
            a_l.append(_dot(x, w1b[...]))
            g_l.append(_dot(x, w3b[...]))
        h_l = [((a * jax.nn.sigmoid(a)) * g).astype(BF16) for a, g in zip(a_l, g_l)]
        for r0, h in zip(starts, h_l):
            y = _rows_to_slots(_dot(h, w2b[...])).reshape(rows * SUBLANES, LANES)
            ybuf[b % 2, pl.ds(r0 * SUBLANES, rows * SUBLANES), :] = y.astype(BF16)
        y_copy(b).start()
        return carry

    lax.fori_loop(0, nb_ref[e], block, 0)

    @pl.when(e == pl.num_programs(0) - 1)
    def _():
        @pl.when(n_used >= 2)
        def _():
            y_copy(n_used - 2).wait()

        y_copy(n_used - 1).wait()
        ybuf[0] = jnp.zeros(ybuf.shape[1:], BF16)

        def zero_copy(b):
            return pltpu.make_async_copy(ybuf.at[0], hbm_block(ys_ref, b), out_sem.at[0])

        def zero(b, carry):
            zero_copy(b).start()
            return carry

        lax.fori_loop(n_used, n_blk, zero, 0)

        def zero_wait(b, carry):
            zero_copy(b).wait()
            return carry

        lax.fori_loop(n_used, n_blk, zero_wait, 0)


def _experts(blk0, nblk, counts, nu, xs, w1, w3, w2, *, tm):
    n_slot = xs.shape[0]
    n_blk = n_slot // tm
    n_exp, d, de = w1.shape
    wmap = lambda e, *_: (e, 0, 0)
    ys = pl.pallas_call(
        functools.partial(_experts_kernel, tm=tm, n_blk=n_blk),
        grid_spec=pltpu.PrefetchScalarGridSpec(
            num_scalar_prefetch=4,
            grid=(n_exp,),
            in_specs=[pl.BlockSpec((1, d, de), wmap), pl.BlockSpec((1, d, de), wmap), pl.BlockSpec((1, de, d), wmap),
                      pl.BlockSpec(memory_space=pl.ANY)],
            out_specs=pl.BlockSpec(memory_space=pl.ANY),
            scratch_shapes=[pltpu.VMEM((d, de), BF16), pltpu.VMEM((d, de), BF16), pltpu.VMEM((de, d), BF16),
                            pltpu.VMEM((X_DEPTH, tm * SUBLANES, LANES), BF16), pltpu.VMEM((2, tm * SUBLANES, LANES), BF16),
                            pltpu.SemaphoreType.DMA((X_DEPTH,)), pltpu.SemaphoreType.DMA((2,))],
        ),
        out_shape=jax.ShapeDtypeStruct((n_slot * SUBLANES, LANES), BF16),
        compiler_params=pltpu.CompilerParams(dimension_semantics=("arbitrary",), vmem_limit_bytes=VMEM_LIMIT),
        name="experts",
    )(blk0, nblk, counts, nu, w1, w3, w2, xs.reshape(n_slot * SUBLANES, LANES))
    return ys.reshape(n_slot, SUBLANES, LANES)


def _combine_kernel(dcur_ref, dnxt_ref, x2_ref, g_ref, w1_ref, w3_ref, w2_ref, lg_ref, lb_ref, ys_ref,
                    out_ref, buf, acc_scr, sem, *, t, nsteps):
    i = pl.program_id(0)

    def fetch(d_ref, s, n):
        for k in range(TOP_K):
            pltpu.make_async_copy(ys_ref.at[d_ref[n * TOP_K + k]], buf.at[s, k, n],
                                  sem.at[s]).start(priority=k % 2)

    def drain(s):
        for k in range(TOP_K):
            pltpu.make_async_copy(ys_ref.at[pl.ds(0, t)], buf.at[s, k], sem.at[s]).wait()

    @pl.when(i == 0)
    def _():
        def first(n, carry):
            fetch(dcur_ref, 0, n)
            return carry

        lax.fori_loop(0, t, first, 0)

    def step(s):
        drain(s)

        def group(gi, carry):
            r0 = pl.multiple_of(gi * SUBLANES, SUBLANES)
            for j in range(SUBLANES):
                fetch(dnxt_ref, 1 - s, r0 + j)
            gates = g_ref[pl.ds(r0, SUBLANES), :]
            gk = [jnp.broadcast_to(gates[:, k:k + 1], (SUBLANES, LANES)) for k in range(TOP_K)]
            acc3 = None
            for k in range(TOP_K):
                rows8 = buf[s, k, pl.ds(r0, SUBLANES)].astype(F32)
                term = gk[k][:, None, :] * rows8
                acc3 = term if acc3 is None else acc3 + term
            blocks = _slots_to_blocks(acc3)
            for lb in range(SUBLANES):
                acc_scr[pl.ds(r0, SUBLANES), lb * LANES:(lb + 1) * LANES] = blocks[lb]
            return carry

        lax.fori_loop(0, t // SUBLANES, group, 0)

    for s in range(2):
        @pl.when(i % 2 == s)
        def _(s=s):
            step(s)

    @pl.when(i == nsteps - 1)
    def _():
        drain(nsteps % 2)

    x2 = x2_ref[...]
    acc = acc_scr[...]
    xb = x2.astype(BF16)
    a = _dot(xb, w1_ref[...])
    hs = (a * jax.nn.sigmoid(a)) * _dot(xb, w3_ref[...])
    shared = _dot(hs.astype(BF16), w2_ref[...])
    out_ref[...] = _ln(x2 * ALPHA + (acc + shared), lg_ref[...], lb_ref[...])


def _combine(dest, x2, gate_t, w1s, w3s, w2s, lg, lb, ys, *, row_off):
    n_rows, d = x2.shape
    n = dest.shape[0] // TOP_K
    t = TOKEN_TILE
    assert row_off % t == 0 and n_rows % t == 0 and n % t == 0
    off = row_off // t
    last = n // t - 1
    de = w1s.shape[-1]
    const = lambda shape: pl.BlockSpec(shape, lambda i: (0,) * len(shape))
    return pl.pallas_call(
        functools.partial(_combine_kernel, t=t, nsteps=n_rows // t),
        grid=(n_rows // t,),
        in_specs=[pl.BlockSpec((t * TOP_K,), lambda i: (off + i,), memory_space=pltpu.SMEM),
                  pl.BlockSpec((t * TOP_K,), lambda i: (jnp.minimum(off + i + 1, last),), memory_space=pltpu.SMEM),
                  pl.BlockSpec((t, d), lambda i: (i, 0)),
                  pl.BlockSpec((t, TOP_K), lambda i: (off + i, 0)),
                  const((d, de)), const((d, de)), const((de, d)), const((1, d)), const((1, d)),
                  pl.BlockSpec(memory_space=pl.ANY)],
        out_specs=pl.BlockSpec((t, d), lambda i: (i, 0)),
        out_shape=jax.ShapeDtypeStruct((n_rows, d), F32),
        scratch_shapes=[pltpu.VMEM((2, TOP_K, t, SUBLANES, LANES), BF16), pltpu.VMEM((t, d), F32),
                        pltpu.SemaphoreType.DMA((2,))],
        compiler_params=pltpu.CompilerParams(dimension_semantics=("arbitrary",), vmem_limit_bytes=VMEM_LIMIT),
        name="combine",
    )(dest, dest, x2, gate_t, w1s, w3s, w2s, lg, lb, ys)


EXPERT_TM = 512
EXPERT_SPLIT = 2
X_DEPTH = 4


def _moe_plan(cnt):
    tm = EXPERT_TM
    counts = cnt[:, 0].astype(I32)
    padded = (counts + tm - 1) // tm * tm
    pad_end = jnp.cumsum(padded)
    pad_start = pad_end - padded
    n_used = pad_end[-1] // tm
    fill_start = (pad_start + counts).astype(I32)
    fill_len = (padded - counts).astype(I32)
    return (pad_start.astype(I32), (pad_start // tm).astype(I32), (padded // tm).astype(I32), counts,
            n_used.reshape(1).astype(I32), fill_start, fill_len)


def _slots_kernel(ps_ref, e_ref, r_ref, d_ref):
    e = e_ref[...]

    def body(j, base):
        return jnp.where(e == j, ps_ref[j], base)

    base = lax.fori_loop(0, N_EXPERTS, body, jnp.zeros_like(e), unroll=8)
    d_ref[...] = base + r_ref[...]


def _slots(pad_start, top_e, rank):
    k, n = top_e.shape
    tl = next(c for c in (2048, 1536, 1024, 512, 256, 128) if n % c == 0)
    spec = pl.BlockSpec((k, tl), lambda i, ps: (0, i))
    return pl.pallas_call(
        _slots_kernel,
        grid_spec=pltpu.PrefetchScalarGridSpec(num_scalar_prefetch=1, grid=(n // tl,), in_specs=[spec, spec],
                                               out_specs=spec),
        out_shape=jax.ShapeDtypeStruct((k, n), I32),
        compiler_params=pltpu.CompilerParams(dimension_semantics=("arbitrary",)),
        name="slots",
    )(pad_start, top_e, rank)


PAST_LEN = 1024


def kernel(x_prompt, x_sample, state_gla, cache_pool, cache_mem_k, cache_mem_v, mem_prompt, ln_in_g, ln_in_b, w_in, w_gate_up, b_gate, gla_norm_g, pool_w, pool_scale, w_out, ln1_g, ln1_b, wq_mem, wk_mem, wv_mem, wo_mem, ln2_g, ln2_b, w_router, router_bias, w1_exp, w3_exp, w2_exp, w1_sh, w3_sh, w2_sh, ln3_g, ln3_b):
    assert w_in.shape[0] == 1, "single-layer trunk"
    bp, lp, d = x_prompt.shape
    bs, ls, _ = x_sample.shape
    n_p, n_s = bp * lp, bs * ls
    n_all = n_p + n_s

    wts_a = _prep_trunk_a_weights(ln_in_g, ln_in_b, w_in[0], w_gate_up[0], b_gate[0], gla_norm_g[0], pool_w[0],
                                  pool_scale[0], w_out[0], ln1_g[0], ln1_b[0])
    s0 = jnp.zeros((bp, GLA_HEADS, GLA_DK, GLA_DV), F32)
    h0 = jnp.zeros((bp, POOL_HIST, POOL_WIDTH), F32)
    x1p, sp, hp = _trunk_a(x_prompt, s0, h0, wts_a, start_pos=0)
    x1s, ss, hs = _trunk_a(x_sample, state_gla[0], cache_pool[0], wts_a, start_pos=PAST_LEN)

    mk, mv, mkb, mvb = _mem_kv(mem_prompt.reshape(bp * N_MEM, d), wk_mem[0].astype(BF16), wv_mem[0].astype(BF16))
    wq, wo = wq_mem[0].astype(BF16), wo_mem[0].astype(BF16)
    g2, b2 = ln2_g[0].reshape(1, d), ln2_b[0].reshape(1, d)
    x2p = _attn(x1p, mkb.reshape(bp, N_MEM, d), mvb.reshape(bp, N_MEM, d), wq, wo, g2, b2)
    x2s = _attn(x1s, cache_mem_k[0].reshape(bs, N_MEM, d).astype(BF16),
                cache_mem_v[0].reshape(bs, N_MEM, d).astype(BF16), wq, wo, g2, b2)

    wrt = w_router[0].T
    wrt_h = wrt.astype(BF16)
    wrt_m = (wrt - wrt_h.astype(F32)).astype(BF16)
    top_e, gate, rank, cnt = _router(x2p, x2s, jnp.stack([wrt_h, wrt_m]), router_bias[0].reshape(N_EXPERTS, 1))

    assert (n_all * TOP_K) % EXPERT_TM == 0
    n_blk = n_all * TOP_K // EXPERT_TM + N_EXPERTS
    pad_start, blk0, nblk, counts, nu, fill_start, fill_len = _moe_plan(cnt)
    dest = _slots(pad_start, top_e, rank).T.reshape(-1)
    xs = _dispatch(fill_start, fill_len, nu, dest, x2p, x2s, n_blk=n_blk, tm=EXPERT_TM)
    ys = _experts(blk0, nblk, counts, nu, xs, w1_exp[0], w3_exp[0], w2_exp[0], tm=EXPERT_TM)
    sh = (w1_sh[0].astype(BF16), w3_sh[0].astype(BF16), w2_sh[0].astype(BF16),
          ln3_g[0].reshape(1, d), ln3_b[0].reshape(1, d))
    gate_t = gate.T
    yp = _combine(dest, x2p, gate_t, *sh, ys, row_off=0)
    ysm = _combine(dest, x2s, gate_t, *sh, ys, row_off=n_p)

    return (yp.reshape(bp, lp, d), ysm.reshape(bs, ls, d), sp[None], hp[None],
            mk.reshape(1, bp, N_MEM, MEM_HEADS, MEM_DH), mv.reshape(1, bp, N_MEM, MEM_HEADS, MEM_DH),
            ss[None], hs[None])
```
